```python
import jax, jax.numpy as jnp
from jax import lax
import numpy as np

D_MODEL = 1024
BATCH = 8
SEQ = 4096
DEPTH = 2

GRID_W = 64
CTX_LEN = 256
HEAD_DIM = 64
D_MIX = D_MODEL
CONV_CH = D_MIX // 4
NA_HEADS = (3 * D_MIX // 8) // HEAD_DIM
NA_DIM = NA_HEADS * HEAD_DIM
SW_HEADS = (D_MIX - CONV_CH - NA_DIM) // HEAD_DIM
SW_KV_HEADS = 2
SW_DIM = SW_HEADS * HEAD_DIM
SW_KV_DIM = SW_KV_HEADS * HEAD_DIM
CONV_WIDTH = 3
NA_WIN_R = 8
NA_WIN_C = 16
NA_QB = 16
NA_KB = NA_QB + NA_WIN_C
SW_RADIUS = 128
SW_BLOCK = 128
ROPE_THETA = 10000.0
D_FF = 2816
N_EXPERTS = 8
TOP_K = 2
D_FF_EXPERT = 3584
N_DENSE = (DEPTH + 1) // 2
N_MOE = DEPTH // 2
EPS = 1e-6
NEG_INF = -1e30
Q_SPLITS = [CONV_CH, 2 * CONV_CH, 3 * CONV_CH, 3 * CONV_CH + NA_DIM]
KV_OFF = 3 * CONV_CH + NA_DIM + SW_DIM
KV_SPLITS = [NA_DIM, 2 * NA_DIM, 2 * NA_DIM + SW_KV_DIM]
KV_WIDTH = 2 * NA_DIM + 2 * SW_KV_DIM
D_IN = KV_OFF + KV_WIDTH

kernel_name = "hybrid_ctxprefix_conv_natten_swa_moe"


def rms_norm(x, g):
    xf = x.astype(jnp.float32)
    y = xf * lax.rsqrt(jnp.mean(xf * xf, axis=-1, keepdims=True) + EPS)
    return (y * g.astype(jnp.float32)).astype(x.dtype)


def modulate(h, shift, scale):
    return h * (1 + scale) + shift


def heads(t, n_heads):
    return t.reshape(t.shape[:-1] + (n_heads, HEAD_DIM))


def short_conv(u, w):
    return lax.conv_general_dilated(u, w[:, None, :].astype(u.dtype), window_strides=(1,),
                                    padding=[(CONV_WIDTH // 2, CONV_WIDTH // 2)],
                                    dimension_numbers=('NWC', 'WIO', 'NWC'),
                                    feature_group_count=u.shape[-1])


def axial_rope_tables(n):
    t = jnp.arange(n)
    row = (t // GRID_W).astype(jnp.float32)
    col = (t % GRID_W).astype(jnp.float32)
    n_freq = HEAD_DIM // 4
    inv = ROPE_THETA ** (-jnp.arange(n_freq, dtype=jnp.float32) / n_freq)
    ang = jnp.stack([row[:, None] * inv, col[:, None] * inv], axis=1)
    return jnp.cos(ang), jnp.sin(ang)


def apply_axial_rope(x, cos, sin):
    xf = x.astype(jnp.float32).reshape(x.shape[:-1] + (2, 2, HEAD_DIM // 4))
    x1, x2 = xf[..., 0, :], xf[..., 1, :]
    cs, sn = cos[None, :, None], sin[None, :, None]
    out = jnp.stack([x1 * cs - x2 * sn, x2 * cs + x1 * sn], axis=-2)
    return out.reshape(x.shape).astype(x.dtype)


def softmax_with_sink(s, sink):
    m = jnp.maximum(jnp.max(s, axis=-1, keepdims=True), sink)
    p = jnp.exp(s - m)
    return p / (jnp.sum(p, axis=-1, keepdims=True) + jnp.exp(sink - m))


def neighbourhood_attention(q, k, v, kc, vc, rpb):
    bsz, n, h, dh = q.shape
    rows = n // GRID_W
    kr = min(NA_WIN_R, rows)
    nqb = GRID_W // NA_QB
    scale = dh ** -0.5
    qcol = np.arange(GRID_W).reshape(nqb, NA_QB)
    cstart = np.clip(qcol - NA_WIN_C // 2, 0, GRID_W - NA_WIN_C)
    kc0 = np.minimum(cstart[:, 0], GRID_W - NA_KB)
    kcol = kc0[:, None] + np.arange(NA_KB)
    col_valid = (kcol[:, None, :] >= cstart[..., None]) & (kcol[:, None, :] < cstart[..., None] + NA_WIN_C)
    dcol_idx = np.clip(kcol[:, None, :] - qcol[..., None] + NA_WIN_C - 1, 0, 2 * NA_WIN_C - 2)
    rpb_c = rpb.astype(jnp.float32)[:, :, dcol_idx]
    kg = k.reshape(bsz, rows, GRID_W, h, dh)
    vg = v.reshape(bsz, rows, GRID_W, h, dh)
    qrows = jnp.moveaxis(q.reshape(bsz, rows, nqb, NA_QB, h, dh), 1, 0)

    def one_row(args):
        qb, r = args
        rs = jnp.clip(r - NA_WIN_R // 2, 0, rows - kr)
        kb = lax.dynamic_slice_in_dim(kg, rs, kr, axis=1)[:, :, kcol]
        vb = lax.dynamic_slice_in_dim(vg, rs, kr, axis=1)[:, :, kcol]
        bias = jnp.take(rpb_c, rs + jnp.arange(kr) - r + NA_WIN_R - 1, axis=1)
        bias = bias.transpose(0, 2, 3, 1, 4)
        s_loc = jnp.einsum('bnqhd,brnkhd->bhnqrk', qb, kb, preferred_element_type=jnp.float32) * scale + bias
        s_loc = jnp.where(col_valid[:, :, None, :], s_loc, NEG_INF).reshape(bsz, h, nqb, NA_QB, kr * NA_KB)
        s_ctx = jnp.einsum('bnqhd,bchd->bhnqc', qb, kc, preferred_element_type=jnp.float32) * scale
        p = jax.nn.softmax(jnp.concatenate([s_loc, s_ctx], axis=-1), axis=-1).astype(v.dtype)
        p_loc = p[..., :kr * NA_KB].reshape(bsz, h, nqb, NA_QB, kr, NA_KB)
        out = (jnp.einsum('bhnqrk,brnkhd->bnqhd', p_loc, vb)
               + jnp.einsum('bhnqc,bchd->bnqhd', p[..., kr * NA_KB:], vc))
        return out.reshape(bsz, GRID_W, h * dh)

    out = lax.map(one_row, (qrows, jnp.arange(rows)))
    return jnp.moveaxis(out, 0, 1).reshape(bsz, n, h * dh)


def window_attention(q, k, v, kc, vc, sink):
    bsz, n, hq, dh = q.shape
    g = hq // SW_KV_HEADS
    nblk = n // SW_BLOCK
    span = SW_BLOCK + 2 * SW_RADIUS
    scale = dh ** -0.5
    kp = jnp.pad(k, ((0, 0), (SW_RADIUS, SW_RADIUS), (0, 0), (0, 0)))
    vp = jnp.pad(v, ((0, 0), (SW_RADIUS, SW_RADIUS), (0, 0), (0, 0)))
    rel = np.arange(span)[None, :] - SW_RADIUS - np.arange(SW_BLOCK)[:, None]
    band = np.abs(rel) <= SW_RADIUS
    sink_b = sink.astype(jnp.float32).reshape(SW_KV_HEADS, g)[None, :, :, None, None]
    qblk = jnp.moveaxis(q.reshape(bsz, nblk, SW_BLOCK, SW_KV_HEADS, g, dh), 1, 0)

    def one_block(args):
        qb, i = args
        s0 = i * SW_BLOCK
        kb = lax.dynamic_slice_in_dim(kp, s0, span, axis=1)
        vb = lax.dynamic_slice_in_dim(vp, s0, span, axis=1)
        kabs = s0 - SW_RADIUS + jnp.arange(span)
        valid = band & ((kabs >= 0) & (kabs < n))[None, :]
        s_loc = jnp.einsum('bqkgd,bskd->bkgqs', qb, kb, preferred_element_type=jnp.float32) * scale
        s_loc = jnp.where(valid, s_loc, NEG_INF)
        s_ctx = jnp.einsum('bqkgd,bckd->bkgqc', qb, kc, preferred_element_type=jnp.float32) * scale
        p = softmax_with_sink(jnp.concatenate([s_loc, s_ctx], axis=-1), sink_b).astype(v.dtype)
        out = (jnp.einsum('bkgqs,bskd->bqkgd', p[..., :span], vb)
               + jnp.einsum('bkgqc,bckd->bqkgd', p[..., span:], vc))
        return out.reshape(bsz, SW_BLOCK, hq * dh)

    out = lax.map(one_block, (qblk, jnp.arange(nblk)))
    return jnp.moveaxis(out, 0, 1).reshape(bsz, n, hq * dh)


def context_mh_attention(q, k, v):
    bsz, m, h, dh = q.shape
    s = jnp.einsum('bqhd,bkhd->bhqk', q, k, preferred_element_type=jnp.float32) * dh ** -0.5
    p = jax.nn.softmax(s, axis=-1).astype(v.dtype)
    return jnp.einsum('bhqk,bkhd->bqhd', p, v).reshape(bsz, m, h * dh)


def context_gqa_sink_attention(q, k, v, sink):
    bsz, m, hq, dh = q.shape
    g = hq // SW_KV_HEADS
    qg = q.reshape(bsz, m, SW_KV_HEADS, g, dh)
    s = jnp.einsum('bqkgd,bskd->bkgqs', qg, k, preferred_element_type=jnp.float32) * dh ** -0.5
    sink_b = sink.astype(jnp.float32).reshape(SW_KV_HEADS, g)[None, :, :, None, None]
    p = softmax_with_sink(s, sink_b).astype(v.dtype)
    return jnp.einsum('bkgqs,bskd->bqkgd', p, v).reshape(bsz, m, hq * dh)


def merge_groups(y_a, y_b, y_c, g):
    g_a, g_b, g_c = jnp.split(g, [CONV_CH, CONV_CH + NA_DIM])
    return jnp.concatenate([rms_norm(y_a, g_a), rms_norm(y_b, g_b), rms_norm(y_c, g_c)], axis=-1)


def swiglu(h, wg, wu, wd):
    return (jax.nn.silu(h @ wg) * (h @ wu)) @ wd


def moe_swiglu(h, w_router, wg, wu, wd):
    logits = jnp.einsum('btd,de->bte', h, w_router, preferred_element_type=jnp.float32)
    top_v, top_i = lax.top_k(logits, TOP_K)
    gates = jax.nn.softmax(top_v, axis=-1)
    combine = jnp.sum(jax.nn.one_hot(top_i, N_EXPERTS, dtype=jnp.float32) * gates[..., None], axis=-2)
    combine = combine.astype(h.dtype)
    out = jnp.zeros_like(h)
    for e in range(N_EXPERTS):
        out = out + combine[..., e:e + 1] * swiglu(h, wg[e], wu[e], wd[e])
    return out


def channel_mixer(h, l, ffn_w_gate, ffn_w_up, ffn_w_down, w_router, moe_w_gate, moe_w_up, moe_w_down):
    i = l // 2
    if l % 2 == 0:
        return swiglu(h, ffn_w_gate[i], ffn_w_up[i], ffn_w_down[i])
    return moe_swiglu(h, w_router[i], moe_w_gate[i], moe_w_up[i], moe_w_down[i])


def setup_inputs(seed: int = 0) -> dict:
    key = jax.random.key(seed)
    ks = iter(jax.random.split(key, 24))

    def nrm(shape, scale):
        return jax.random.normal(next(ks), shape, jnp.float32) * scale

    return {
        'x': nrm((BATCH, SEQ, D_MODEL), 1.0),
        'c': nrm((BATCH, D_MODEL), 1.0),
        'ctx': nrm((BATCH, CTX_LEN, D_MODEL), 1.0),
        'c_ctx': nrm((D_MODEL,), 1.0),
        'w_ada': nrm((DEPTH, D_MODEL, 6 * D_MODEL), 0.5 * D_MODEL ** -0.5),
        'b_ada': nrm((DEPTH, 6 * D_MODEL), 0.02),
        'g_norm1': 1.0 + nrm((DEPTH, D_MODEL), 0.02),
        'g_norm2': 1.0 + nrm((DEPTH, D_MODEL), 0.02),
        'w_in': nrm((DEPTH, D_MODEL, D_IN), D_MODEL ** -0.5),
        'conv_w': nrm((DEPTH, CONV_WIDTH, CONV_CH), CONV_WIDTH ** -0.5),
        'na_rpb': nrm((DEPTH, NA_HEADS, 2 * NA_WIN_R - 1, 2 * NA_WIN_C - 1), 0.1),
        'sw_sink': nrm((DEPTH, SW_HEADS), 0.5),
        'g_mix': 1.0 + nrm((DEPTH, D_MIX), 0.02),
        'w_out': nrm((DEPTH, D_MIX, D_MODEL), D_MIX ** -0.5),
        'ffn_w_gate': nrm((N_DENSE, D_MODEL, D_FF), D_MODEL ** -0.5),
        'ffn_w_up': nrm((N_DENSE, D_MODEL, D_FF), D_MODEL ** -0.5),
        'ffn_w_down': nrm((N_DENSE, D_FF, D_MODEL), D_FF ** -0.5),
        'w_router': nrm((N_MOE, D_MODEL, N_EXPERTS), D_MODEL ** -0.5),
        'moe_w_gate': nrm((N_MOE, N_EXPERTS, D_MODEL, D_FF_EXPERT), D_MODEL ** -0.5),
        'moe_w_up': nrm((N_MOE, N_EXPERTS, D_MODEL, D_FF_EXPERT), D_MODEL ** -0.5),
        'moe_w_down': nrm((N_MOE, N_EXPERTS, D_FF_EXPERT, D_MODEL), D_FF_EXPERT ** -0.5),
        'g_final': 1.0 + nrm((D_MODEL,), 0.02),
    }


def reference(x, c, ctx, c_ctx, w_ada, b_ada, g_norm1, g_norm2, w_in, conv_w, na_rpb, sw_sink, g_mix,
              w_out, ffn_w_gate, ffn_w_up, ffn_w_down, w_router, moe_w_gate, moe_w_up, moe_w_down, g_final):
    n = x.shape[1]
    cos, sin = axial_rope_tables(n)
    c_act = jax.nn.silu(c)
    cc_act = jax.nn.silu(c_ctx)
    xl, xc = x, ctx
    for l in range(DEPTH):
        last = l == DEPTH - 1
        sh1, sc1, g1, sh2, sc2, g2 = jnp.split((c_act @ w_ada[l] + b_ada[l])[:, None, :], 6, axis=-1)
        csh1, csc1, cg1, csh2, csc2, cg2 = jnp.split(cc_act @ w_ada[l] + b_ada[l], 6, axis=-1)

        hl = modulate(rms_norm(xl, g_norm1[l]), sh1, sc1)
        hc = modulate(rms_norm(xc, g_norm1[l]), csh1, csc1)
        pl = hl @ w_in[l]
        pc = hc @ (w_in[l][:, KV_OFF:] if last else w_in[l])
        a_h, a_b, a_c, na_q, sw_q = jnp.split(pl[..., :KV_OFF], Q_SPLITS, axis=-1)
        na_k, na_v, sw_k, sw_v = jnp.split(pl[..., KV_OFF:], KV_SPLITS, axis=-1)
        cna_k, cna_v, csw_k, csw_v = jnp.split(pc[..., -KV_WIDTH:], KV_SPLITS, axis=-1)
        cna_k, cna_v = heads(cna_k, NA_HEADS), heads(cna_v, NA_HEADS)
        csw_k, csw_v = heads(csw_k, SW_KV_HEADS), heads(csw_v, SW_KV_HEADS)

        y_a = a_b * short_conv(a_c * a_h, conv_w[l])
        y_b = neighbourhood_attention(heads(na_q, NA_HEADS), heads(na_k, NA_HEADS), heads(na_v, NA_HEADS),
                                      cna_k, cna_v, na_rpb[l])
        y_c = window_attention(apply_axial_rope(heads(sw_q, SW_HEADS), cos, sin),
                               apply_axial_rope(heads(sw_k, SW_KV_HEADS), cos, sin),
                               heads(sw_v, SW_KV_HEADS), csw_k, csw_v, sw_sink[l])
        mix_l = merge_groups(y_a, y_b, y_c, g_mix[l]) @ w_out[l]

        if not last:
            ca_h, ca_b, ca_c, cna_q, csw_q = jnp.split(pc[..., :KV_OFF], Q_SPLITS, axis=-1)
            yc_a = ca_b * short_conv(ca_c * ca_h, conv_w[l])
            yc_b = context_mh_attention(heads(cna_q, NA_HEADS), cna_k, cna_v)
            yc_c = context_gqa_sink_attention(heads(csw_q, SW_HEADS), csw_k, csw_v, sw_sink[l])
            xc = xc + cg1 * (merge_groups(yc_a, yc_b, yc_c, g_mix[l]) @ w_out[l])
        xl = xl + g1 * mix_l

        hl2 = modulate(rms_norm(xl, g_norm2[l]), sh2, sc2)
        xl = xl + g2 * channel_mixer(hl2, l, ffn_w_gate, ffn_w_up, ffn_w_down, w_router,
                                     moe_w_gate, moe_w_up, moe_w_down)
        if not last:
            hc2 = modulate(rms_norm(xc, g_norm2[l]), csh2, csc2)
            xc = xc + cg2 * channel_mixer(hc2, l, ffn_w_gate, ffn_w_up, ffn_w_down, w_router,
                                          moe_w_gate, moe_w_up, moe_w_down)
    return rms_norm(xl, g_final)
```

```python
import functools

import numpy as np
import jax
import jax.numpy as jnp
from jax import lax
from jax.experimental import pallas as pl
from jax.experimental.pallas import tpu as pltpu

F32 = jnp.float32
BF16 = jnp.bfloat16

GRID_W = 64
HEAD_DIM = 64
CONV_CH = 256
NA_HEADS = 6
NA_DIM = NA_HEADS * HEAD_DIM
SW_HEADS = 6
SW_KV_HEADS = 2
SW_DIM = SW_HEADS * HEAD_DIM
SW_KV_DIM = SW_KV_HEADS * HEAD_DIM
NA_WIN_R = 8
NA_WIN_C = 16
SW_RADIUS = 128
ROPE_THETA = 10000.0
N_EXPERTS = 8
EPS = 1e-6
NEG_INF = -1e30
OFF_H, OFF_B, OFF_C = 0, CONV_CH, 2 * CONV_CH
OFF_NQ = 3 * CONV_CH
OFF_SQ = OFF_NQ + NA_DIM
OFF_NK = OFF_SQ + SW_DIM
OFF_NV = OFF_NK + NA_DIM
OFF_SK = OFF_NV + NA_DIM
OFF_SV = OFF_SK + SW_KV_DIM
D_IN = OFF_SV + SW_KV_DIM
Q_SCALE = HEAD_DIM ** -0.5

LANES = 128
SUBLANES = 8
VMEM_BYTES_V7X = 64 * 1024 * 1024
VMEM_LIMIT = VMEM_BYTES_V7X * 7 // 8

NA_ROWS_PER_STEP = 4
NA_KEY_ROWS = NA_ROWS_PER_STEP + NA_WIN_R - 1
SW_QBLOCK = 256
SW_SPAN = SW_QBLOCK + 2 * SW_RADIUS
LOGIT_LANES = LANES


def _params(semantics):
    return pltpu.CompilerParams(dimension_semantics=semantics, vmem_limit_bytes=VMEM_LIMIT)


def _const_spec(shape):
    nd = len(shape)
    return pl.BlockSpec(shape, lambda *_: (0,) * nd)


def _resident_spec(shape):
    nd = len(shape)
    return pl.BlockSpec(shape, lambda *_: (0,) * nd, pipeline_mode=pl.Buffered(1))


def _rms(x, g):
    y = x * lax.rsqrt(jnp.mean(x * x, axis=-1, keepdims=True) + EPS)
    return y * g


def _silu(x):
    return x / (1.0 + jnp.exp(-x))


def _dot(a, b):
    return jnp.dot(a, b, preferred_element_type=F32)


def _dot_nt(a, b):
    return lax.dot_general(a, b, (((1,), (1,)), ((), ())), preferred_element_type=F32)


def _ada_kernel(c_ref, w_ref, b_ref, o_ref):
    a = _silu(c_ref[...])
    o_ref[0] = jnp.dot(a, w_ref[0], preferred_element_type=F32,
                       precision=lax.Precision.HIGHEST) + b_ref[0]


def _ada(cond, w_ada, b_ada):
    depth, d, n_out = w_ada.shape
    rows = cond.shape[0]
    tn = n_out // 6
    return pl.pallas_call(
        _ada_kernel,
        grid=(depth, n_out // tn),
        in_specs=[pl.BlockSpec((rows, d), lambda l, j: (0, 0)),
                  pl.BlockSpec((1, d, tn), lambda l, j: (l, 0, j)),
                  pl.BlockSpec((1, 1, tn), lambda l, j: (l, 0, j))],
        out_specs=pl.BlockSpec((1, rows, tn), lambda l, j: (l, 0, j)),
        out_shape=jax.ShapeDtypeStruct((depth, rows, n_out), F32),
        compiler_params=_params(("arbitrary", "arbitrary")),
    )(cond, w_ada, b_ada.reshape(depth, 1, n_out))


def _rope(x, cos, sin, lane_lo):
    rot = jnp.where(lane_lo, pltpu.roll(x, LANES - 16, 1), pltpu.roll(x, 16, 1))
    return x * cos + rot * sin


def _inproj_kernel(x_ref, xp_ref, xn_ref, sh_ref, sc_ref, gn_ref, w_ref, cw_ref, ga_ref, cos_ref, sin_ref,
                   ya_ref, qn_ref, qs_ref, kn_ref, vn_ref, ks_ref, vs_ref, u_s, *, tm, tiles_per_seq, rope):
    i = pl.program_id(0)
    g = gn_ref[...]
    sh = sh_ref[0]
    sc = sc_ref[0]

    def normmod(x):
        return _rms(x, g) * (1.0 + sc) + sh

    h = normmod(x_ref[...]).astype(BF16)
    p = _dot(h, w_ref[...])

    xh = jnp.concatenate([xp_ref[...], xn_ref[...]], axis=0)
    ph = _dot(normmod(xh).astype(BF16), w_ref[:, :OFF_NQ])
    uh = ph[:, OFF_C:OFF_C + CONV_CH] * ph[:, OFF_H:OFF_H + CONV_CH]
    pos = i % tiles_per_seq
    u = p[:, OFF_C:OFF_C + CONV_CH] * p[:, OFF_H:OFF_H + CONV_CH]
    u_s[0:SUBLANES] = jnp.where(pos == 0, 0.0, uh[0:SUBLANES])
    u_s[SUBLANES:SUBLANES + tm] = u
    u_s[SUBLANES + tm:2 * SUBLANES + tm] = jnp.where(pos == tiles_per_seq - 1, 0.0, uh[SUBLANES:])
    cw = cw_ref[...]
    conv = (u_s[SUBLANES - 1:SUBLANES - 1 + tm] * cw[0:1] + u * cw[1:2]
            + u_s[SUBLANES + 1:SUBLANES + 1 + tm] * cw[2:3])
    ya = p[:, OFF_B:OFF_B + CONV_CH] * conv
    ya_ref[...] = _rms(ya, ga_ref[...]).astype(BF16)

    qn_ref[...] = (p[:, OFF_NQ:OFF_NQ + NA_DIM] * Q_SCALE).astype(BF16)
    kn_ref[...] = p[:, OFF_NK:OFF_NK + NA_DIM].astype(BF16)
    vn_ref[...] = p[:, OFF_NV:OFF_NV + NA_DIM].astype(BF16)
    vs_ref[...] = p[:, OFF_SV:OFF_SV + SW_KV_DIM].astype(BF16)
    if rope:
        cos = cos_ref[...]
        sin = sin_ref[...]
        lane_lo = (lax.broadcasted_iota(jnp.int32, (tm, LANES), 1) % 32) < 16
        for j in range(SW_DIM // LANES):
            q = p[:, OFF_SQ + j * LANES:OFF_SQ + (j + 1) * LANES]
            qs_ref[:, j * LANES:(j + 1) * LANES] = (_rope(q, cos, sin, lane_lo) * Q_SCALE).astype(BF16)
        ks_ref[...] = _rope(p[:, OFF_SK:OFF_SK + SW_KV_DIM], cos, sin, lane_lo).astype(BF16)
    else:
        qs_ref[...] = (p[:, OFF_SQ:OFF_SQ + SW_DIM] * Q_SCALE).astype(BF16)
        ks_ref[...] = p[:, OFF_SK:OFF_SK + SW_KV_DIM].astype(BF16)


def _inproj(x2d, mods, group_of_tile, g_norm, w, conv_w, g_a, cos_t, sin_t, *, seq_len, tm, rope):
    t, d = x2d.shape
    tiles_per_seq = seq_len // tm
    blk8 = tm // SUBLANES
    last8 = t // SUBLANES - 1
    kern = functools.partial(_inproj_kernel, tm=tm, tiles_per_seq=tiles_per_seq, rope=rope)
    tok = lambda width: pl.BlockSpec((tm, width), lambda i: (i, 0))
    out_widths = (CONV_CH, NA_DIM, SW_DIM, NA_DIM, NA_DIM, SW_KV_DIM, SW_KV_DIM)
    return pl.pallas_call(
        kern,
        grid=(t // tm,),
        in_specs=[tok(d),
                  pl.BlockSpec((SUBLANES, d), lambda i: (jnp.maximum(i * blk8 - 1, 0), 0)),
                  pl.BlockSpec((SUBLANES, d), lambda i: (jnp.minimum((i + 1) * blk8, last8), 0)),
                  pl.BlockSpec((1, 1, d), lambda i: (group_of_tile(i), 0, 0)),
                  pl.BlockSpec((1, 1, d), lambda i: (group_of_tile(i), 0, 1)),
                  _const_spec((1, d)),
                  _resident_spec(w.shape),
                  _const_spec(conv_w.shape),
                  _const_spec((1, CONV_CH)),
                  pl.BlockSpec((tm, LANES), lambda i: (i % tiles_per_seq, 0)),
                  pl.BlockSpec((tm, LANES), lambda i: (i % tiles_per_seq, 0))],
        out_specs=[tok(wd) for wd in out_widths],
        out_shape=[jax.ShapeDtypeStruct((t, wd), BF16) for wd in out_widths],
        scratch_shapes=[pltpu.VMEM((tm + 2 * SUBLANES, CONV_CH), F32)],
        compiler_params=_params(("arbitrary",)),
    )(x2d, x2d, x2d, mods, mods, g_norm, w, conv_w, g_a, cos_t, sin_t)


def _head_attention(qm, k_loc, v_loc, add_loc, k_ctx, v_ctx, sink):
    s_ctx = _dot_nt(qm, k_ctx)
    m = jnp.max(s_ctx, axis=-1, keepdims=True)
    if k_loc is not None:
        s_loc = _dot_nt(qm, k_loc) + add_loc
        m = jnp.maximum(m, jnp.max(s_loc, axis=-1, keepdims=True))
    if sink is not None:
        m = jnp.maximum(m, sink)
    p_ctx = jnp.exp(s_ctx - m)
    den = jnp.sum(p_ctx, axis=-1, keepdims=True)
    o = _dot(p_ctx.astype(BF16), v_ctx)
    if k_loc is not None:
        p_loc = jnp.exp(s_loc - m)
        den = den + jnp.sum(p_loc, axis=-1, keepdims=True)
        o = o + _dot(p_loc.astype(BF16), v_loc)
    if sink is not None:
        den = den + jnp.exp(sink - m)
    return o / den


def _pair_attention(q, k_loc, v_loc, add_loc, k_ctx, v_ctx, sinks, lane_first):
    outs = []
    for half in range(2):
        keep = lane_first if half == 0 else jnp.logical_not(lane_first)
        qm = jnp.where(keep, q, jnp.zeros_like(q))
        outs.append(_head_attention(qm, k_loc, v_loc, None if add_loc is None else add_loc[half],
                                    k_ctx, v_ctx, None if sinks is None else sinks[half]))
    return jnp.where(lane_first, outs[0], outs[1])


def _lane_first(m):
    return lax.broadcasted_iota(jnp.int32, (m, LANES), 1) < HEAD_DIM


def _na_kernel(q_ref, k_ref, v_ref, kc_ref, vc_ref, bias_ref, g_ref, o_ref, y_s, *, rows):
    gidx = pl.program_id(1)
    base = jnp.clip(gidx * NA_ROWS_PER_STEP - NA_WIN_R // 2, 0, rows - NA_KEY_ROWS)
    start = pl.multiple_of(base * GRID_W, GRID_W)
    nk = NA_KEY_ROWS * GRID_W
    mq = NA_ROWS_PER_STEP * GRID_W
    lane_first = _lane_first(mq)
    for j in range(NA_DIM // LANES):
        cols = slice(j * LANES, (j + 1) * LANES)
        k_loc = k_ref[pl.ds(start, nk), cols]
        v_loc = v_ref[pl.ds(start, nk), cols]
        bias = (bias_ref[0, 2 * j], bias_ref[0, 2 * j + 1])
        y_s[:, cols] = _pair_attention(q_ref[:, cols], k_loc, v_loc, bias, kc_ref[:, cols],
                                       vc_ref[:, cols], None, lane_first)
    o_ref[...] = _rms(y_s[...], g_ref[...]).astype(BF16)


def _na_bias(rpb, rows):
    n_groups = rows // NA_ROWS_PER_STEP
    reps = np.array([0, 1, n_groups - 1])
    qr = reps[:, None] * NA_ROWS_PER_STEP + np.arange(NA_ROWS_PER_STEP)[None, :]
    base = np.clip(reps * NA_ROWS_PER_STEP - NA_WIN_R // 2, 0, rows - NA_KEY_ROWS)
    kr = base[:, None] + np.arange(NA_KEY_ROWS)[None, :]
    rs = np.clip(qr - NA_WIN_R // 2, 0, rows - NA_WIN_R)
    row_ok = (kr[:, None, :] >= rs[:, :, None]) & (kr[:, None, :] < rs[:, :, None] + NA_WIN_R)
    dr = np.clip(kr[:, None, :] - qr[:, :, None] + NA_WIN_R - 1, 0, 2 * NA_WIN_R - 2)
    qc = np.arange(GRID_W)
    kc = np.arange(GRID_W)
    cstart = np.clip(qc - NA_WIN_C // 2, 0, GRID_W - NA_WIN_C)
    col_ok = (kc[None, :] >= cstart[:, None]) & (kc[None, :] < cstart[:, None] + NA_WIN_C)
    dc = np.clip(kc[None, :] - qc[:, None] + NA_WIN_C - 1, 0, 2 * NA_WIN_C - 2)
    dr_i = np.broadcast_to(dr[:, :, None, :, None], (3, NA_ROWS_PER_STEP, GRID_W, NA_KEY_ROWS, GRID_W))
    dc_i = np.broadcast_to(dc[None, None, :, None, :], dr_i.shape)
    ok = row_ok[:, :, None, :, None] & col_ok[None, None, :, None, :]
    vals = rpb.astype(F32)[:, dr_i, dc_i]
    vals = jnp.where(ok[None], vals, NEG_INF)
    vals = vals.reshape(rpb.shape[0], 3, NA_ROWS_PER_STEP * GRID_W, NA_KEY_ROWS * GRID_W)
    return jnp.transpose(vals, (1, 0, 2, 3))


def _na_attention(qn, kn, vn, kn_c, vn_c, bias, g_b, *, bsz, seq_len, ctx_len):
    rows = seq_len // GRID_W
    n_groups = rows // NA_ROWS_PER_STEP
    mq = NA_ROWS_PER_STEP * GRID_W
    kern = functools.partial(_na_kernel, rows=rows)
    return pl.pallas_call(
        kern,
        grid=(bsz, n_groups),
        in_specs=[pl.BlockSpec((mq, NA_DIM), lambda b, g: (b * n_groups + g, 0)),
                  pl.BlockSpec((seq_len, NA_DIM), lambda b, g: (b, 0)),
                  pl.BlockSpec((seq_len, NA_DIM), lambda b, g: (b, 0)),
                  pl.BlockSpec((ctx_len, NA_DIM), lambda b, g: (b, 0)),
                  pl.BlockSpec((ctx_len, NA_DIM), lambda b, g: (b, 0)),
                  pl.BlockSpec((1,) + bias.shape[1:],
                               lambda b, g: (jnp.minimum(g, 1) + g // (n_groups - 1), 0, 0, 0)),
                  _const_spec((1, NA_DIM))],
        out_specs=pl.BlockSpec((mq, NA_DIM), lambda b, g: (b * n_groups + g, 0)),
        out_shape=jax.ShapeDtypeStruct(qn.shape, BF16),
        scratch_shapes=[pltpu.VMEM((mq, NA_DIM), F32)],
        compiler_params=_params(("arbitrary", "arbitrary")),
    )(qn, kn, vn, kn_c, vn_c, bias, g_b)


def _sw_kernel(sink_ref, q_ref, k_ref, v_ref, kc_ref, vc_ref, g_ref, o_ref, y_s, *, seq_len):
    i = pl.program_id(1)
    s0 = i * SW_QBLOCK
    kstart = pl.multiple_of(jnp.clip(s0 - SW_RADIUS, 0, seq_len - SW_SPAN), SW_RADIUS)
    k_loc = k_ref[pl.ds(kstart, SW_SPAN), :]
    v_loc = v_ref[pl.ds(kstart, SW_SPAN), :]
    qpos = s0 + lax.broadcasted_iota(jnp.int32, (SW_QBLOCK, SW_SPAN), 0)
    kpos = kstart + lax.broadcasted_iota(jnp.int32, (SW_QBLOCK, SW_SPAN), 1)
    band = jnp.where(jnp.abs(kpos - qpos) <= SW_RADIUS, 0.0, NEG_INF).astype(F32)
    lane_first = _lane_first(SW_QBLOCK)
    pairs = SW_HEADS // SW_KV_HEADS
    for j in range(pairs):
        cols = slice(j * LANES, (j + 1) * LANES)
        sinks = (sink_ref[j], sink_ref[j + pairs])
        y_s[:, cols] = _pair_attention(q_ref[:, cols], k_loc, v_loc, (band, band), kc_ref[...], vc_ref[...],
                                       sinks, lane_first)
    o_ref[...] = _rms(y_s[...], g_ref[...]).astype(BF16)


def _sw_attention(sink, qs, ks, vs, ks_c, vs_c, g_c, *, bsz, seq_len, ctx_len):
    nblk = seq_len // SW_QBLOCK
    kern = functools.partial(_sw_kernel, seq_len=seq_len)
    return pl.pallas_call(
        kern,
        grid=(bsz, nblk),
        in_specs=[pl.BlockSpec(memory_space=pltpu.SMEM),
                  pl.BlockSpec((SW_QBLOCK, SW_DIM), lambda b, i: (b * nblk + i, 0)),
                  pl.BlockSpec((seq_len, SW_KV_DIM), lambda b, i: (b, 0)),
                  pl.BlockSpec((seq_len, SW_KV_DIM), lambda b, i: (b, 0)),
                  pl.BlockSpec((ctx_len, SW_KV_DIM), lambda b, i: (b, 0)),
                  pl.BlockSpec((ctx_len, SW_KV_DIM), lambda b, i: (b, 0)),
                  _const_spec((1, SW_DIM))],
        out_specs=pl.BlockSpec((SW_QBLOCK, SW_DIM), lambda b, i: (b * nblk + i, 0)),
        out_shape=jax.ShapeDtypeStruct(qs.shape, BF16),
        scratch_shapes=[pltpu.VMEM((SW_QBLOCK, SW_DIM), F32)],
        compiler_params=_params(("arbitrary", "arbitrary")),
    )(sink, qs, ks, vs, ks_c, vs_c, g_c)


def _ctx_attn_kernel(sink_ref, qn_ref, kn_ref, vn_ref, qs_ref, ks_ref, vs_ref, gb_ref, gc_ref,
                     yb_ref, yc_ref, y_s, *, ctx_len):
    lane_first = _lane_first(ctx_len)
    for j in range(NA_DIM // LANES):
        cols = slice(j * LANES, (j + 1) * LANES)
        y_s[:, cols] = _pair_attention(qn_ref[:, cols], None, None, None, kn_ref[:, cols], vn_ref[:, cols],
                                       None, lane_first)
    yb_ref[...] = _rms(y_s[...], gb_ref[...]).astype(BF16)
    pairs = SW_HEADS // SW_KV_HEADS
    for j in range(pairs):
        cols = slice(j * LANES, (j + 1) * LANES)
        sinks = (sink_ref[j], sink_ref[j + pairs])
        y_s[:, cols] = _pair_attention(qs_ref[:, cols], None, None, None, ks_ref[...], vs_ref[...],
                                       sinks, lane_first)
    yc_ref[...] = _rms(y_s[...], gc_ref[...]).astype(BF16)


def _ctx_attention(sink, qn, kn, vn, qs, ks, vs, g_b, g_c, *, bsz, ctx_len):
    kern = functools.partial(_ctx_attn_kernel, ctx_len=ctx_len)
    tok = lambda width: pl.BlockSpec((ctx_len, width), lambda b: (b, 0))
    return pl.pallas_call(
        kern,
        grid=(bsz,),
        in_specs=[pl.BlockSpec(memory_space=pltpu.SMEM),
                  tok(NA_DIM), tok(NA_DIM), tok(NA_DIM), tok(SW_DIM), tok(SW_KV_DIM), tok(SW_KV_DIM),
                  _const_spec((1, NA_DIM)), _const_spec((1, SW_DIM))],
        out_specs=[tok(NA_DIM), tok(SW_DIM)],
        out_shape=[jax.ShapeDtypeStruct(qn.shape, BF16), jax.ShapeDtypeStruct(qs.shape, BF16)],
        scratch_shapes=[pltpu.VMEM((ctx_len, NA_DIM), F32)],
        compiler_params=_params(("arbitrary",)),
    )(sink, qn, kn, vn, qs, ks, vs, g_b, g_c)


def _outproj_kernel(*refs, router):
    if router:
        (ya_ref, yb_ref, yc_ref, x_ref, g1_ref, sh_ref, sc_ref, gn_ref, wa_ref, wb_ref, wc_ref,
         rhi_ref, rlo_ref, xo_ref, h_ref, lg_ref) = refs
    else:
        (ya_ref, yb_ref, yc_ref, x_ref, g1_ref, sh_ref, sc_ref, gn_ref, wa_ref, wb_ref, wc_ref,
         xo_ref, h_ref) = refs
    mix = _dot(ya_ref[...], wa_ref[...]) + _dot(yb_ref[...], wb_ref[...]) + _dot(yc_ref[...], wc_ref[...])
    xl = x_ref[...] + g1_ref[0] * mix
    xo_ref[...] = xl
    h = _rms(xl, gn_ref[...]) * (1.0 + sc_ref[0]) + sh_ref[0]
    hi = h.astype(BF16)
    h_ref[...] = hi
    if router:
        lo = (h - hi.astype(F32)).astype(BF16)
        lg_ref[...] = _dot(hi, rhi_ref[...]) + _dot(lo, rhi_ref[...]) + _dot(hi, rlo_ref[...])


def _outproj(ya, yb, yc, x2d, mods, group_of_tile, g_norm, wa, wb, wc, router_hi=None, router_lo=None, *, tm):
    t, d = x2d.shape
    router = router_hi is not None
    tok = lambda width: pl.BlockSpec((tm, width), lambda i: (i, 0))
    mod = lambda j: pl.BlockSpec((1, 1, d), lambda i: (group_of_tile(i), 0, j))
    in_specs = [tok(CONV_CH), tok(NA_DIM), tok(SW_DIM), tok(d), mod(2), mod(3), mod(4),
                _const_spec((1, d)), _resident_spec(wa.shape), _resident_spec(wb.shape), _resident_spec(wc.shape)]
    args = [ya, yb, yc, x2d, mods, mods, mods, g_norm, wa, wb, wc]
    out_specs = [tok(d), tok(d)]
    out_shape = [jax.ShapeDtypeStruct((t, d), F32), jax.ShapeDtypeStruct((t, d), BF16)]
    if router:
        in_specs += [_resident_spec(router_hi.shape), _resident_spec(router_lo.shape)]
        args += [router_hi, router_lo]
        out_specs.append(tok(LOGIT_LANES))
        out_shape.append(jax.ShapeDtypeStruct((t, LOGIT_LANES), F32))
    return pl.pallas_call(
        functools.partial(_outproj_kernel, router=router),
        grid=(t // tm,),
        in_specs=in_specs, out_specs=out_specs, out_shape=out_shape,
        compiler_params=_params(("arbitrary",)),
    )(*args)


def _ffn_kernel(h_ref, x_ref, g2_ref, wg_ref, wu_ref, wd_ref, o_ref, *, chunks):
    h = h_ref[...]
    acc = None
    for lo, hi in chunks:
        a = _silu(_dot(h, wg_ref[:, lo:hi])) * _dot(h, wu_ref[:, lo:hi])
        part = _dot(a.astype(BF16), wd_ref[lo:hi, :])
        acc = part if acc is None else acc + part
    o_ref[...] = x_ref[...] + g2_ref[0] * acc


def _ffn(h2, x2d, mods, group_of_tile, wg, wu, wd, *, tm):
    t, d = x2d.shape
    d_ff = wg.shape[1]
    step = 1024
    chunks = tuple((lo, min(lo + step, d_ff)) for lo in range(0, d_ff, step))
    tok = pl.BlockSpec((tm, d), lambda i: (i, 0))
    return pl.pallas_call(
        functools.partial(_ffn_kernel, chunks=chunks),
        grid=(t // tm,),
        in_specs=[tok, tok, pl.BlockSpec((1, 1, d), lambda i: (group_of_tile(i), 0, 5)),
                  _resident_spec(wg.shape), _resident_spec(wu.shape), _resident_spec(wd.shape)],
        out_specs=tok,
        out_shape=jax.ShapeDtypeStruct((t, d), F32),
        compiler_params=_params(("arbitrary",)),
    )(h2, x2d, mods, wg, wu, wd)


def _combine_weights(logits):
    lane = lax.broadcasted_iota(jnp.int32, logits.shape, 1)
    lg = jnp.where(lane < N_EXPERTS, logits, -jnp.inf)
    v1 = jnp.max(lg, axis=-1, keepdims=True)
    i1 = jnp.min(jnp.where(lg == v1, lane, LOGIT_LANES), axis=-1, keepdims=True)
    lg2 = jnp.where(lane == i1, -jnp.inf, lg)
    v2 = jnp.max(lg2, axis=-1, keepdims=True)
    i2 = jnp.min(jnp.where(lg2 == v2, lane, LOGIT_LANES), axis=-1, keepdims=True)
    e2 = jnp.exp(v2 - v1)
    den = 1.0 + e2
    return jnp.where(lane == i1, 1.0 / den, 0.0) + jnp.where(lane == i2, e2 / den, 0.0)


def _moe_kernel(lg_ref, h_ref, x_ref, g2_ref, gf_ref, wg_ref, wu_ref, wd_ref, o_ref, acc_s, cw_s):
    e = pl.program_id(1)
    f = pl.program_id(2)

    @pl.when((e == 0) & (f == 0))
    def _():
        cw_s[...] = _combine_weights(lg_ref[...])
        acc_s[...] = jnp.zeros_like(acc_s)

    lane = lax.broadcasted_iota(jnp.int32, cw_s.shape, 1)
    cwe = jnp.sum(jnp.where(lane == e, cw_s[...], 0.0), axis=-1, keepdims=True)
    h = h_ref[...]
    a = _silu(_dot(h, wg_ref[0])) * _dot(h, wu_ref[0]) * cwe
    acc_s[...] += _dot(a.astype(BF16), wd_ref[0])

    @pl.when((e == pl.num_programs(1) - 1) & (f == pl.num_programs(2) - 1))
    def _():
        o_ref[...] = _rms(x_ref[...] + g2_ref[0] * acc_s[...], gf_ref[...])


def _moe(logits, h2, x2d, mods, group_of_tile, g_final, wg, wu, wd, *, tm, tf):
    t, d = x2d.shape
    n_exp, _, d_ff = wg.shape
    tok = lambda width: pl.BlockSpec((tm, width), lambda i, e, f: (i, 0))
    return pl.pallas_call(
        _moe_kernel,
        grid=(t // tm, n_exp, d_ff // tf),
        in_specs=[tok(LOGIT_LANES), tok(d), tok(d),
                  pl.BlockSpec((1, 1, d), lambda i, e, f: (group_of_tile(i), 0, 5)),
                  pl.BlockSpec((1, d), lambda i, e, f: (0, 0)),
                  pl.BlockSpec((1, d, tf), lambda i, e, f: (e, 0, f)),
                  pl.BlockSpec((1, d, tf), lambda i, e, f: (e, 0, f)),
                  pl.BlockSpec((1, tf, d), lambda i, e, f: (e, f, 0))],
        out_specs=tok(d),
        out_shape=jax.ShapeDtypeStruct((t, d), F32),
        scratch_shapes=[pltpu.VMEM((tm, d), F32), pltpu.VMEM((tm, LOGIT_LANES), F32)],
        compiler_params=_params(("arbitrary", "arbitrary", "arbitrary")),
    )(logits, h2, x2d, mods, g_final, wg, wu, wd)


def _rope_tables(n):
    t = jnp.arange(n)
    row = (t // GRID_W).astype(F32)
    col = (t % GRID_W).astype(F32)
    n_freq = HEAD_DIM // 4
    inv = ROPE_THETA ** (-jnp.arange(n_freq, dtype=F32) / n_freq)
    ang = jnp.stack([row[:, None] * inv, col[:, None] * inv], axis=1)
    cos = jnp.cos(ang)[:, :, None, :]
    sin = jnp.sin(ang)[:, :, None, :]
    cos_f = jnp.broadcast_to(cos, (n, 2, 2, n_freq)).reshape(n, HEAD_DIM)
    sin_s = jnp.concatenate([-sin, sin], axis=2).reshape(n, HEAD_DIM)
    reps = LANES // HEAD_DIM
    return jnp.tile(cos_f, (1, reps)), jnp.tile(sin_s, (1, reps))


def _pair_perm():
    pairs = SW_HEADS // SW_KV_HEADS
    order = [h for j in range(pairs) for h in (j, j + pairs)]
    return np.concatenate([np.arange(h * HEAD_DIM, (h + 1) * HEAD_DIM) for h in order])


def kernel(x, c, ctx, c_ctx, w_ada, b_ada, g_norm1, g_norm2, w_in, conv_w, na_rpb, sw_sink, g_mix, w_out,
           ffn_w_gate, ffn_w_up, ffn_w_down, w_router, moe_w_gate, moe_w_up, moe_w_down, g_final):
    bsz, n, d = x.shape
    ctx_len = ctx.shape[1]
    depth = w_in.shape[0]
    assert w_in.shape[2] == D_IN and n % GRID_W == 0
    tm = 512
    tm_ctx = ctx_len
    assert n % tm == 0 and (n // GRID_W) % NA_ROWS_PER_STEP == 0 and n % SW_QBLOCK == 0

    cond_rows = 2 * SUBLANES
    cond = jnp.zeros((cond_rows, d), F32).at[:bsz].set(c).at[bsz].set(c_ctx)
    mods_all = _ada(cond, w_ada, b_ada).reshape(depth, cond_rows, 1, 6 * d)

    lat_group = lambda i: i // (n // tm)
    ctx_group = lambda i: bsz
    cos_t, sin_t = _rope_tables(n)
    perm = _pair_perm()

    xl = x.reshape(bsz * n, d)
    xc = ctx.reshape(bsz * ctx_len, d)
    out = None
    for l in range(depth):
        last = l == depth - 1
        mods = mods_all[l]
        w_l = w_in[l]
        w_l = jnp.concatenate([w_l[:, :OFF_SQ], w_l[:, OFF_SQ:OFF_NK][:, perm], w_l[:, OFF_NK:]], axis=1).astype(BF16)
        g_a = g_mix[l, :CONV_CH].reshape(1, CONV_CH)
        g_b = g_mix[l, CONV_CH:CONV_CH + NA_DIM].reshape(1, NA_DIM)
        g_c = g_mix[l, CONV_CH + NA_DIM:][perm].reshape(1, SW_DIM)
        wo = w_out[l]
        wa = wo[:CONV_CH].astype(BF16)
        wb = wo[CONV_CH:CONV_CH + NA_DIM].astype(BF16)
        wc = wo[CONV_CH + NA_DIM:][perm].astype(BF16)
        gn1 = g_norm1[l].reshape(1, d)
        gn2 = g_norm2[l].reshape(1, d)
        sink = sw_sink[l].astype(F32)
        bias = _na_bias(na_rpb[l], n // GRID_W)

        ya, qn, qs, kn, vn, ks, vs = _inproj(xl, mods, lat_group, gn1, w_l, conv_w[l], g_a, cos_t, sin_t,
                                             seq_len=n, tm=tm, rope=True)
        cya, cqn, cqs, ckn, cvn, cks, cvs = _inproj(xc, mods, ctx_group, gn1, w_l, conv_w[l], g_a, cos_t, sin_t,
                                                    seq_len=ctx_len, tm=tm_ctx, rope=False)
        yb = _na_attention(qn, kn, vn, ckn, cvn, bias, g_b, bsz=bsz, seq_len=n, ctx_len=ctx_len)
        yc = _sw_attention(sink, qs, ks, vs, cks, cvs, g_c, bsz=bsz, seq_len=n, ctx_len=ctx_len)

        if l % 2 == 0:
            i = l // 2
            xl, h2 = _outproj(ya, yb, yc, xl, mods, lat_group, gn2, wa, wb, wc, tm=tm)
            wg = ffn_w_gate[i].astype(BF16)
            wu = ffn_w_up[i].astype(BF16)
            wd = ffn_w_down[i].astype(BF16)
            xl = _ffn(h2, xl, mods, lat_group, wg, wu, wd, tm=tm)
            if not last:
                cyb, cyc = _ctx_attention(sink, cqn, ckn, cvn, cqs, cks, cvs, g_b, g_c, bsz=bsz, ctx_len=ctx_len)
                xc, ch2 = _outproj(cya, cyb, cyc, xc, mods, ctx_group, gn2, wa, wb, wc, tm=tm_ctx)
                xc = _ffn(ch2, xc, mods, ctx_group, wg, wu, wd, tm=tm_ctx)
        else:
            i = l // 2
            wr = jnp.zeros((d, LOGIT_LANES), F32).at[:, :N_EXPERTS].set(w_router[i])
            wr_hi = wr.astype(BF16)
            wr_lo = (wr - wr_hi.astype(F32)).astype(BF16)
            wg = moe_w_gate[i].astype(BF16)
            wu = moe_w_up[i].astype(BF16)
            wd = moe_w_down[i].astype(BF16)
            xl, h2, logits = _outproj(ya, yb, yc, xl, mods, lat_group, gn2, wa, wb, wc, wr_hi, wr_lo, tm=tm)
            assert last, "expert layers on the context stream are only needed before the last layer"
            out = _moe(logits, h2, xl, mods, lat_group, g_final.reshape(1, d), wg, wu, wd, tm=tm, tf=512)
    return out.reshape(bsz, n, d)
```

```python
import functools

import numpy as np
import jax
import jax.numpy as jnp
from jax import lax
from jax.experimental import pallas as pl
from jax.experimental.pallas import tpu as pltpu

F32 = jnp.float32
BF16 = jnp.bfloat16

GRID_W = 64
HEAD_DIM = 64
CONV_CH = 256
NA_HEADS = 6
NA_DIM = NA_HEADS * HEAD_DIM
SW_HEADS = 6
SW_KV_HEADS = 2
SW_DIM = SW_HEADS * HEAD_DIM
SW_KV_DIM = SW_KV_HEADS * HEAD_DIM
NA_WIN_R = 8
NA_WIN_C = 16
SW_RADIUS = 128
ROPE_THETA = 10000.0
N_EXPERTS = 8
EPS = 1e-6
NEG_INF = -1e30
OFF_H, OFF_B, OFF_C = 0, CONV_CH, 2 * CONV_CH
OFF_NQ = 3 * CONV_CH
OFF_SQ = OFF_NQ + NA_DIM
OFF_NK = OFF_SQ + SW_DIM
OFF_NV = OFF_NK + NA_DIM
OFF_SK = OFF_NV + NA_DIM
OFF_SV = OFF_SK + SW_KV_DIM
D_IN = OFF_SV + SW_KV_DIM
Q_SCALE = HEAD_DIM ** -0.5

LANES = 128
SUBLANES = 8
VMEM_BYTES_V7X = 64 * 1024 * 1024
VMEM_LIMIT = VMEM_BYTES_V7X * 7 // 8

NA_ROWS_PER_STEP = 4
NA_KEY_ROWS = NA_ROWS_PER_STEP + NA_WIN_R
NA_BIAS_MASKED = 2 * NA_WIN_R - 1
MOE_TOKENS = 2048
MOE_CHUNK = 256
MOE_ROW_TILE = 128
MOE_M_TILE = 256
MOE_FF_TILE = 512
SW_QBLOCK = 256
SW_SPAN = SW_QBLOCK + 2 * SW_RADIUS
LOGIT_LANES = LANES


def _params(semantics):
    return pltpu.CompilerParams(dimension_semantics=semantics, vmem_limit_bytes=VMEM_LIMIT)


def _const_spec(shape):
    nd = len(shape)
    return pl.BlockSpec(shape, lambda *_: (0,) * nd)


def _resident_spec(shape):
    nd = len(shape)
    return pl.BlockSpec(shape, lambda *_: (0,) * nd, pipeline_mode=pl.Buffered(1))


def _rms(x, g):
    y = x * lax.rsqrt(jnp.mean(x * x, axis=-1, keepdims=True) + EPS)
    return y * g


def _silu(x):
    return x / (1.0 + jnp.exp(-x))


def _dot(a, b):
    return jnp.dot(a, b, preferred_element_type=F32)


def _dot_nt(a, b):
    return lax.dot_general(a, b, (((1,), (1,)), ((), ())), preferred_element_type=F32)


def _ada_kernel(c_ref, w_ref, b_ref, o_ref):
    a = _silu(c_ref[...])
    o_ref[0] = jnp.dot(a, w_ref[0], preferred_element_type=F32,
                       precision=lax.Precision.HIGHEST) + b_ref[0]


def _ada(cond, w_ada, b_ada):
    depth, d, n_out = w_ada.shape
    rows = cond.shape[0]
    tn = n_out // 6
    return pl.pallas_call(
        _ada_kernel,
        grid=(depth, n_out // tn),
        in_specs=[pl.BlockSpec((rows, d), lambda l, j: (0, 0)),
                  pl.BlockSpec((1, d, tn), lambda l, j: (l, 0, j)),
                  pl.BlockSpec((1, 1, tn), lambda l, j: (l, 0, j))],
        out_specs=pl.BlockSpec((1, rows, tn), lambda l, j: (l, 0, j)),
        out_shape=jax.ShapeDtypeStruct((depth, rows, n_out), F32),
        compiler_params=_params(("arbitrary", "arbitrary")),
    )(cond, w_ada, b_ada.reshape(depth, 1, n_out))


def _rope(x, cos, sin, lane_lo):
    rot = jnp.where(lane_lo, pltpu.roll(x, LANES - 16, 1), pltpu.roll(x, 16, 1))
    return x * cos + rot * sin


def _inproj_kernel(x_ref, xp_ref, xn_ref, sh_ref, sc_ref, gn_ref, w_ref, cw_ref, ga_ref, cos_ref, sin_ref,
                   ya_ref, qn_ref, qs_ref, kn_ref, vn_ref, ks_ref, vs_ref, u_s, *, tm, tiles_per_seq, rope):
    i = pl.program_id(0)
    g = gn_ref[...]
    sh = sh_ref[0]
    sc = sc_ref[0]

    def normmod(x):
        return _rms(x, g) * (1.0 + sc) + sh

    h = normmod(x_ref[...]).astype(BF16)
    p = _dot(h, w_ref[...])

    xh = jnp.concatenate([xp_ref[...], xn_ref[...]], axis=0)
    ph = _dot(normmod(xh).astype(BF16), w_ref[:, :OFF_NQ])
    uh = ph[:, OFF_C:OFF_C + CONV_CH] * ph[:, OFF_H:OFF_H + CONV_CH]
    pos = i % tiles_per_seq
    u = p[:, OFF_C:OFF_C + CONV_CH] * p[:, OFF_H:OFF_H + CONV_CH]
    u_s[0:SUBLANES] = jnp.where(pos == 0, 0.0, uh[0:SUBLANES])
    u_s[SUBLANES:SUBLANES + tm] = u
    u_s[SUBLANES + tm:2 * SUBLANES + tm] = jnp.where(pos == tiles_per_seq - 1, 0.0, uh[SUBLANES:])
    cw = cw_ref[...]
    conv = (u_s[SUBLANES - 1:SUBLANES - 1 + tm] * cw[0:1] + u * cw[1:2]
            + u_s[SUBLANES + 1:SUBLANES + 1 + tm] * cw[2:3])
    ya = p[:, OFF_B:OFF_B + CONV_CH] * conv
    ya_ref[...] = _rms(ya, ga_ref[...]).astype(BF16)

    qn_ref[...] = (p[:, OFF_NQ:OFF_NQ + NA_DIM] * Q_SCALE).astype(BF16)
    kn_ref[...] = p[:, OFF_NK:OFF_NK + NA_DIM].astype(BF16)
    vn_ref[...] = p[:, OFF_NV:OFF_NV + NA_DIM].astype(BF16)
    vs_ref[...] = p[:, OFF_SV:OFF_SV + SW_KV_DIM].astype(BF16)
    if rope:
        cos = cos_ref[...]
        sin = sin_ref[...]
        lane_lo = (lax.broadcasted_iota(jnp.int32, (tm, LANES), 1) % 32) < 16
        for j in range(SW_DIM // LANES):
            q = p[:, OFF_SQ + j * LANES:OFF_SQ + (j + 1) * LANES]
            qs_ref[:, j * LANES:(j + 1) * LANES] = (_rope(q, cos, sin, lane_lo) * Q_SCALE).astype(BF16)
        ks_ref[...] = _rope(p[:, OFF_SK:OFF_SK + SW_KV_DIM], cos, sin, lane_lo).astype(BF16)
    else:
        qs_ref[...] = (p[:, OFF_SQ:OFF_SQ + SW_DIM] * Q_SCALE).astype(BF16)
        ks_ref[...] = p[:, OFF_SK:OFF_SK + SW_KV_DIM].astype(BF16)


def _inproj(x2d, mods, group_of_tile, g_norm, w, conv_w, g_a, cos_t, sin_t, *, seq_len, tm, rope):
    t, d = x2d.shape
    tiles_per_seq = seq_len // tm
    blk8 = tm // SUBLANES
    last8 = t // SUBLANES - 1
    kern = functools.partial(_inproj_kernel, tm=tm, tiles_per_seq=tiles_per_seq, rope=rope)
    tok = lambda width: pl.BlockSpec((tm, width), lambda i: (i, 0))
    out_widths = (CONV_CH, NA_DIM, SW_DIM, NA_DIM, NA_DIM, SW_KV_DIM, SW_KV_DIM)
    return pl.pallas_call(
        kern,
        grid=(t // tm,),
        in_specs=[tok(d),
                  pl.BlockSpec((SUBLANES, d), lambda i: (jnp.maximum(i * blk8 - 1, 0), 0)),
                  pl.BlockSpec((SUBLANES, d), lambda i: (jnp.minimum((i + 1) * blk8, last8), 0)),
                  pl.BlockSpec((1, 1, d), lambda i: (group_of_tile(i), 0, 0)),
                  pl.BlockSpec((1, 1, d), lambda i: (group_of_tile(i), 0, 1)),
                  _const_spec((1, d)),
                  _resident_spec(w.shape),
                  _const_spec(conv_w.shape),
                  _const_spec((1, CONV_CH)),
                  pl.BlockSpec((tm, LANES), lambda i: (i % tiles_per_seq, 0)),
                  pl.BlockSpec((tm, LANES), lambda i: (i % tiles_per_seq, 0))],
        out_specs=[tok(wd) for wd in out_widths],
        out_shape=[jax.ShapeDtypeStruct((t, wd), BF16) for wd in out_widths],
        scratch_shapes=[pltpu.VMEM((tm + 2 * SUBLANES, CONV_CH), F32)],
        compiler_params=_params(("arbitrary",)),
    )(x2d, x2d, x2d, mods, mods, g_norm, w, conv_w, g_a, cos_t, sin_t)


def _head_attention(qm, k_loc, v_loc, add_loc, k_ctx, v_ctx, sink):
    s_ctx = _dot_nt(qm, k_ctx)
    m = jnp.max(s_ctx, axis=-1, keepdims=True)
    if k_loc is not None:
        s_loc = _dot_nt(qm, k_loc) + add_loc
        m = jnp.maximum(m, jnp.max(s_loc, axis=-1, keepdims=True))
    if sink is not None:
        m = jnp.maximum(m, sink)
    p_ctx = jnp.exp(s_ctx - m)
    den = jnp.sum(p_ctx, axis=-1, keepdims=True)
    o = _dot(p_ctx.astype(BF16), v_ctx)
    if k_loc is not None:
        p_loc = jnp.exp(s_loc - m)
        den = den + jnp.sum(p_loc, axis=-1, keepdims=True)
        o = o + _dot(p_loc.astype(BF16), v_loc)
    if sink is not None:
        den = den + jnp.exp(sink - m)
    return o / den


def _pair_attention(q, k_loc, v_loc, add_loc, k_ctx, v_ctx, sinks, lane_first):
    outs = []
    for half in range(2):
        keep = lane_first if half == 0 else jnp.logical_not(lane_first)
        qm = jnp.where(keep, q, jnp.zeros_like(q))
        outs.append(_head_attention(qm, k_loc, v_loc, None if add_loc is None else add_loc[half],
                                    k_ctx, v_ctx, None if sinks is None else sinks[half]))
    return jnp.where(lane_first, outs[0], outs[1])


def _lane_first(m):
    return lax.broadcasted_iota(jnp.int32, (m, LANES), 1) < HEAD_DIM


def _na_kernel(q_ref, k_ref, v_ref, kc_ref, vc_ref, tab_ref, g_ref, o_ref, y_s, bias_s, *, rows):
    gidx = pl.program_id(1)
    base = jnp.clip(gidx * NA_ROWS_PER_STEP - NA_WIN_R // 2, 0, rows - NA_KEY_ROWS)
    start = pl.multiple_of(base * GRID_W, GRID_W)
    nk = NA_KEY_ROWS * GRID_W
    mq = NA_ROWS_PER_STEP * GRID_W
    lane_first = _lane_first(mq)
    tile_first = _lane_first(GRID_W)

    def fill_bias(slot, head):
        for qr in range(NA_ROWS_PER_STEP):
            r = gidx * NA_ROWS_PER_STEP + qr
            rs = jnp.clip(r - NA_WIN_R // 2, 0, rows - NA_WIN_R)
            for m in range(NA_KEY_ROWS // 2):
                idx = []
                for kr in (base + 2 * m, base + 2 * m + 1):
                    ok = (kr >= rs) & (kr < rs + NA_WIN_R)
                    idx.append(jnp.where(ok, kr - r + NA_WIN_R - 1, NA_BIAS_MASKED))
                bias_s[slot, qr * GRID_W:(qr + 1) * GRID_W, m * LANES:(m + 1) * LANES] = jnp.where(
                    tile_first, tab_ref[head, idx[0]], tab_ref[head, idx[1]])

    for j in range(NA_DIM // LANES):
        cols = slice(j * LANES, (j + 1) * LANES)
        k_loc = k_ref[pl.ds(start, nk), cols]
        v_loc = v_ref[pl.ds(start, nk), cols]
        fill_bias(0, 2 * j)
        fill_bias(1, 2 * j + 1)
        y_s[:, cols] = _pair_attention(q_ref[:, cols], k_loc, v_loc, (bias_s[0], bias_s[1]), kc_ref[:, cols],
                                       vc_ref[:, cols], None, lane_first)
    o_ref[...] = _rms(y_s[...], g_ref[...]).astype(BF16)


def _na_bias_table(rpb):
    n_heads, n_dr, n_dc = rpb.shape
    pad = GRID_W - NA_WIN_C
    rp = jnp.pad(rpb.astype(F32), ((0, 0), (0, 0), (pad, pad)))
    toep = jnp.stack([rp[:, :, GRID_W - 1 - qc:2 * GRID_W - 1 - qc] for qc in range(GRID_W)], axis=2)
    qc = np.arange(GRID_W)
    cstart = np.clip(qc - NA_WIN_C // 2, 0, GRID_W - NA_WIN_C)
    col_ok = (qc[None, :] >= cstart[:, None]) & (qc[None, :] < cstart[:, None] + NA_WIN_C)
    tab = jnp.where(col_ok[None, None], toep, NEG_INF)
    masked = jnp.full((n_heads, NA_BIAS_MASKED + 1 - n_dr, GRID_W, GRID_W), NEG_INF, F32)
    tab = jnp.concatenate([tab, masked], axis=1)
    return jnp.tile(tab, (1, 1, 1, LANES // GRID_W))


def _na_attention(qn, kn, vn, kn_c, vn_c, tab, g_b, *, bsz, seq_len, ctx_len):
    rows = seq_len // GRID_W
    n_groups = rows // NA_ROWS_PER_STEP
    mq = NA_ROWS_PER_STEP * GRID_W
    kern = functools.partial(_na_kernel, rows=rows)
    return pl.pallas_call(
        kern,
        grid=(bsz, n_groups),
        in_specs=[pl.BlockSpec((mq, NA_DIM), lambda b, g: (b * n_groups + g, 0)),
                  pl.BlockSpec((seq_len, NA_DIM), lambda b, g: (b, 0)),
                  pl.BlockSpec((seq_len, NA_DIM), lambda b, g: (b, 0)),
                  pl.BlockSpec((ctx_len, NA_DIM), lambda b, g: (b, 0)),
                  pl.BlockSpec((ctx_len, NA_DIM), lambda b, g: (b, 0)),
                  _resident_spec(tab.shape),
                  _const_spec((1, NA_DIM))],
        out_specs=pl.BlockSpec((mq, NA_DIM), lambda b, g: (b * n_groups + g, 0)),
        out_shape=jax.ShapeDtypeStruct(qn.shape, BF16),
        scratch_shapes=[pltpu.VMEM((mq, NA_DIM), F32),
                        pltpu.VMEM((2, mq, NA_KEY_ROWS * GRID_W), F32)],
        compiler_params=_params(("arbitrary", "arbitrary")),
        name="na_attention",
    )(qn, kn, vn, kn_c, vn_c, tab, g_b)


def _sw_kernel(sink_ref, q_ref, k_ref, v_ref, kc_ref, vc_ref, g_ref, o_ref, y_s, *, seq_len):
    i = pl.program_id(1)
    s0 = i * SW_QBLOCK
    kstart = pl.multiple_of(jnp.clip(s0 - SW_RADIUS, 0, seq_len - SW_SPAN), SW_RADIUS)
    k_loc = k_ref[pl.ds(kstart, SW_SPAN), :]
    v_loc = v_ref[pl.ds(kstart, SW_SPAN), :]
    qpos = s0 + lax.broadcasted_iota(jnp.int32, (SW_QBLOCK, SW_SPAN), 0)
    kpos = kstart + lax.broadcasted_iota(jnp.int32, (SW_QBLOCK, SW_SPAN), 1)
    band = jnp.where(jnp.abs(kpos - qpos) <= SW_RADIUS, 0.0, NEG_INF).astype(F32)
    lane_first = _lane_first(SW_QBLOCK)
    pairs = SW_HEADS // SW_KV_HEADS
    for j in range(pairs):
        cols = slice(j * LANES, (j + 1) * LANES)
        sinks = (sink_ref[j], sink_ref[j + pairs])
        y_s[:, cols] = _pair_attention(q_ref[:, cols], k_loc, v_loc, (band, band), kc_ref[...], vc_ref[...],
                                       sinks, lane_first)
    o_ref[...] = _rms(y_s[...], g_ref[...]).astype(BF16)


def _sw_attention(sink, qs, ks, vs, ks_c, vs_c, g_c, *, bsz, seq_len, ctx_len):
    nblk = seq_len // SW_QBLOCK
    kern = functools.partial(_sw_kernel, seq_len=seq_len)
    return pl.pallas_call(
        kern,
        grid=(bsz, nblk),
        in_specs=[pl.BlockSpec(memory_space=pltpu.SMEM),
                  pl.BlockSpec((SW_QBLOCK, SW_DIM), lambda b, i: (b * nblk + i, 0)),
                  pl.BlockSpec((seq_len, SW_KV_DIM), lambda b, i: (b, 0)),
                  pl.BlockSpec((seq_len, SW_KV_DIM), lambda b, i: (b, 0)),
                  pl.BlockSpec((ctx_len, SW_KV_DIM), lambda b, i: (b, 0)),
                  pl.BlockSpec((ctx_len, SW_KV_DIM), lambda b, i: (b, 0)),
                  _const_spec((1, SW_DIM))],
        out_specs=pl.BlockSpec((SW_QBLOCK, SW_DIM), lambda b, i: (b * nblk + i, 0)),
        out_shape=jax.ShapeDtypeStruct(qs.shape, BF16),
        scratch_shapes=[pltpu.VMEM((SW_QBLOCK, SW_DIM), F32)],
        compiler_params=_params(("arbitrary", "arbitrary")),
    )(sink, qs, ks, vs, ks_c, vs_c, g_c)


def _ctx_attn_kernel(sink_ref, qn_ref, kn_ref, vn_ref, qs_ref, ks_ref, vs_ref, gb_ref, gc_ref,
                     yb_ref, yc_ref, y_s, *, ctx_len):
    lane_first = _lane_first(ctx_len)
    for j in range(NA_DIM // LANES):
        cols = slice(j * LANES, (j + 1) * LANES)
        y_s[:, cols] = _pair_attention(qn_ref[:, cols], None, None, None, kn_ref[:, cols], vn_ref[:, cols],
                                       None, lane_first)
    yb_ref[...] = _rms(y_s[...], gb_ref[...]).astype(BF16)
    pairs = SW_HEADS // SW_KV_HEADS
    for j in range(pairs):
        cols = slice(j * LANES, (j + 1) * LANES)
        sinks = (sink_ref[j], sink_ref[j + pairs])
        y_s[:, cols] = _pair_attention(qs_ref[:, cols], None, None, None, ks_ref[...], vs_ref[...],
                                       sinks, lane_first)
    yc_ref[...] = _rms(y_s[...], gc_ref[...]).astype(BF16)


def _ctx_attention(sink, qn, kn, vn, qs, ks, vs, g_b, g_c, *, bsz, ctx_len):
    kern = functools.partial(_ctx_attn_kernel, ctx_len=ctx_len)
    tok = lambda width: pl.BlockSpec((ctx_len, width), lambda b: (b, 0))
    return pl.pallas_call(
        kern,
        grid=(bsz,),
        in_specs=[pl.BlockSpec(memory_space=pltpu.SMEM),
                  tok(NA_DIM), tok(NA_DIM), tok(NA_DIM), tok(SW_DIM), tok(SW_KV_DIM), tok(SW_KV_DIM),
                  _const_spec((1, NA_DIM)), _const_spec((1, SW_DIM))],
        out_specs=[tok(NA_DIM), tok(SW_DIM)],
        out_shape=[jax.ShapeDtypeStruct(qn.shape, BF16), jax.ShapeDtypeStruct(qs.shape, BF16)],
        scratch_shapes=[pltpu.VMEM((ctx_len, NA_DIM), F32)],
        compiler_params=_params(("arbitrary",)),
    )(sink, qn, kn, vn, qs, ks, vs, g_b, g_c)


def _outproj_kernel(*refs, router):
    if router:
        (ya_ref, yb_ref, yc_ref, x_ref, g1_ref, sh_ref, sc_ref, gn_ref, wa_ref, wb_ref, wc_ref,
         rhi_ref, rlo_ref, xo_ref, h_ref, lg_ref) = refs
    else:
        (ya_ref, yb_ref, yc_ref, x_ref, g1_ref, sh_ref, sc_ref, gn_ref, wa_ref, wb_ref, wc_ref,
         xo_ref, h_ref) = refs
    mix = _dot(ya_ref[...], wa_ref[...]) + _dot(yb_ref[...], wb_ref[...]) + _dot(yc_ref[...], wc_ref[...])
    xl = x_ref[...] + g1_ref[0] * mix
    xo_ref[...] = xl
    h = _rms(xl, gn_ref[...]) * (1.0 + sc_ref[0]) + sh_ref[0]
    hi = h.astype(BF16)
    h_ref[...] = hi
    if router:
        lo = (h - hi.astype(F32)).astype(BF16)
        lg_ref[...] = _dot(hi, rhi_ref[...]) + _dot(lo, rhi_ref[...]) + _dot(hi, rlo_ref[...])


def _outproj(ya, yb, yc, x2d, mods, group_of_tile, g_norm, wa, wb, wc, router_hi=None, router_lo=None, *, tm):
    t, d = x2d.shape
    router = router_hi is not None
    tok = lambda width: pl.BlockSpec((tm, width), lambda i: (i, 0))
    mod = lambda j: pl.BlockSpec((1, 1, d), lambda i: (group_of_tile(i), 0, j))
    in_specs = [tok(CONV_CH), tok(NA_DIM), tok(SW_DIM), tok(d), mod(2), mod(3), mod(4),
                _const_spec((1, d)), _resident_spec(wa.shape), _resident_spec(wb.shape), _resident_spec(wc.shape)]
    args = [ya, yb, yc, x2d, mods, mods, mods, g_norm, wa, wb, wc]
    out_specs = [tok(d), tok(d)]
    out_shape = [jax.ShapeDtypeStruct((t, d), F32), jax.ShapeDtypeStruct((t, d), BF16)]
    if router:
        in_specs += [_resident_spec(router_hi.shape), _resident_spec(router_lo.shape)]
        args += [router_hi, router_lo]
        out_specs.append(tok(LOGIT_LANES))
        out_shape.append(jax.ShapeDtypeStruct((t, LOGIT_LANES), F32))
    return pl.pallas_call(
        functools.partial(_outproj_kernel, router=router),
        grid=(t // tm,),
        in_specs=in_specs, out_specs=out_specs, out_shape=out_shape,
        compiler_params=_params(("arbitrary",)),
    )(*args)


def _ffn_kernel(h_ref, x_ref, g2_ref, wg_ref, wu_ref, wd_ref, o_ref, *, chunks):
    h = h_ref[...]
    acc = None
    for lo, hi in chunks:
        a = _silu(_dot(h, wg_ref[:, lo:hi])) * _dot(h, wu_ref[:, lo:hi])
        part = _dot(a.astype(BF16), wd_ref[lo:hi, :])
        acc = part if acc is None else acc + part
    o_ref[...] = x_ref[...] + g2_ref[0] * acc


def _ffn(h2, x2d, mods, group_of_tile, wg, wu, wd, *, tm):
    t, d = x2d.shape
    d_ff = wg.shape[1]
    step = 1024
    chunks = tuple((lo, min(lo + step, d_ff)) for lo in range(0, d_ff, step))
    tok = pl.BlockSpec((tm, d), lambda i: (i, 0))
    return pl.pallas_call(
        functools.partial(_ffn_kernel, chunks=chunks),
        grid=(t // tm,),
        in_specs=[tok, tok, pl.BlockSpec((1, 1, d), lambda i: (group_of_tile(i), 0, 5)),
                  _resident_spec(wg.shape), _resident_spec(wu.shape), _resident_spec(wd.shape)],
        out_specs=tok,
        out_shape=jax.ShapeDtypeStruct((t, d), F32),
        compiler_params=_params(("arbitrary",)),
    )(h2, x2d, mods, wg, wu, wd)


def _route_kernel(lg_ref, gate_ref, rank_ref, rank_t_ref, cnt_ref, *, n_chunks):
    logits = lg_ref[...]
    lane = lax.broadcasted_iota(jnp.int32, logits.shape, 1)
    lg = jnp.where(lane < N_EXPERTS, logits, -jnp.inf)
    v1 = jnp.max(lg, axis=-1, keepdims=True)
    i1 = jnp.min(jnp.where(lg == v1, lane, LOGIT_LANES), axis=-1, keepdims=True)
    lg2 = jnp.where(lane == i1, -jnp.inf, lg)
    v2 = jnp.max(lg2, axis=-1, keepdims=True)
    i2 = jnp.min(jnp.where(lg2 == v2, lane, LOGIT_LANES), axis=-1, keepdims=True)
    e2 = jnp.exp(v2 - v1)
    den = 1.0 + e2
    gate_ref[...] = jnp.where(lane == i1, 1.0 / den, 0.0) + jnp.where(lane == i2, e2 / den, 0.0)
    sel = (lane == i1) | (lane == i2)
    onehot = jnp.where(sel, 1.0, 0.0)

    r_i = lax.broadcasted_iota(jnp.int32, (MOE_CHUNK, MOE_CHUNK), 0)
    c_i = lax.broadcasted_iota(jnp.int32, (MOE_CHUNK, MOE_CHUNK), 1)
    earlier = jnp.where(c_i < r_i, 1.0, 0.0).astype(BF16)
    run = jnp.zeros((1, LOGIT_LANES), F32)
    rows8 = lax.broadcasted_iota(jnp.int32, (SUBLANES, LOGIT_LANES), 0)
    for c in range(n_chunks):
        rows = slice(c * MOE_CHUNK, (c + 1) * MOE_CHUNK)
        oh_c = onehot[rows]
        rank_c = _dot(earlier, oh_c.astype(BF16)) + run
        rank_ref[rows, :] = jnp.where(sel[rows], rank_c, -1.0)
        cnt_ref[0, c] = jnp.where(rows8 == 0, run, 0.0).astype(jnp.int32)
        run = run + jnp.sum(oh_c, axis=0, keepdims=True)
    cnt_ref[0, n_chunks] = jnp.where(rows8 == 0, run, 0.0).astype(jnp.int32)
    rank_t = rank_ref[...].T
    for e in range(N_EXPERTS):
        rank_t_ref[0, e] = rank_t[e:e + 1, :]


def _route(logits, *, tokens):
    t = logits.shape[0]
    n_sb = t // tokens
    n_chunks = tokens // MOE_CHUNK
    return pl.pallas_call(
        functools.partial(_route_kernel, n_chunks=n_chunks),
        grid=(n_sb,),
        in_specs=[pl.BlockSpec((tokens, LOGIT_LANES), lambda i: (i, 0))],
        out_specs=[pl.BlockSpec((tokens, LOGIT_LANES), lambda i: (i, 0)),
                   pl.BlockSpec((tokens, LOGIT_LANES), lambda i: (i, 0)),
                   pl.BlockSpec((1, N_EXPERTS, 1, tokens), lambda i: (i, 0, 0, 0)),
                   pl.BlockSpec((1, n_chunks + 1, SUBLANES, LOGIT_LANES), lambda i: (i, 0, 0, 0))],
        out_shape=[jax.ShapeDtypeStruct((t, LOGIT_LANES), F32),
                   jax.ShapeDtypeStruct((t, LOGIT_LANES), F32),
                   jax.ShapeDtypeStruct((n_sb, N_EXPERTS, 1, tokens), F32),
                   jax.ShapeDtypeStruct((n_sb, n_chunks + 1, SUBLANES, LOGIT_LANES), jnp.int32)],
        compiler_params=_params(("arbitrary",)),
        name="moe_route",
    )(logits)


def _moe_kernel(cnt_ref, rank_ref, rank_t_ref, gate_ref, h_ref, wg_ref, wu_ref, wd_ref, o_ref,
                he_s, ye_s, acc_s, *, n_chunks):
    sb = pl.program_id(0)
    e = pl.program_id(1)
    f = pl.program_id(2)
    n_f = pl.num_programs(2)

    def prefix(c):
        return cnt_ref[(sb * (n_chunks + 1) + c) * N_EXPERTS + e]

    count = prefix(n_chunks)
    n_m = (count + MOE_M_TILE - 1) // MOE_M_TILE
    n_r = n_m * (MOE_M_TILE // MOE_ROW_TILE)

    def overlaps(c, lo):
        return (prefix(c + 1) > lo) & (prefix(c) < lo + MOE_ROW_TILE)

    @pl.when((e == 0) & (f == 0))
    def _():
        o_ref[...] = jnp.zeros_like(o_ref)

    @pl.when(f == 0)
    def _gather_rows():
        def row_tile(r, carry):
            lo = r * MOE_ROW_TILE
            row_id = (lo + lax.broadcasted_iota(jnp.int32, (MOE_ROW_TILE, MOE_CHUNK), 0)).astype(F32)
            acc_s[...] = jnp.zeros_like(acc_s)
            for c in range(n_chunks):
                @pl.when(overlaps(c, lo))
                def _():
                    toks = slice(c * MOE_CHUNK, (c + 1) * MOE_CHUNK)
                    pick = jnp.where(row_id == rank_t_ref[0, e, :, toks], 1.0, 0.0).astype(BF16)
                    acc_s[...] += _dot(pick, h_ref[toks, :])
            he_s[pl.ds(pl.multiple_of(lo, MOE_ROW_TILE), MOE_ROW_TILE), :] = acc_s[...].astype(BF16)
            return carry
        lax.fori_loop(0, n_r, row_tile, 0)

    def m_tile(m, carry):
        rows = pl.ds(pl.multiple_of(m * MOE_M_TILE, MOE_M_TILE), MOE_M_TILE)
        hrow = he_s[rows, :]
        a = _silu(_dot(hrow, wg_ref[0])) * _dot(hrow, wu_ref[0])
        part = _dot(a.astype(BF16), wd_ref[0])

        @pl.when(f == 0)
        def _():
            ye_s[rows, :] = part

        @pl.when(f > 0)
        def _():
            ye_s[rows, :] += part
        return carry
    lax.fori_loop(0, n_m, m_tile, 0)

    @pl.when(f == n_f - 1)
    def _scatter_rows():
        lane = lax.broadcasted_iota(jnp.int32, (MOE_CHUNK, LOGIT_LANES), 1)

        def row_tile(r, carry):
            lo = r * MOE_ROW_TILE
            y_tile = ye_s[pl.ds(pl.multiple_of(lo, MOE_ROW_TILE), MOE_ROW_TILE), :].astype(BF16)
            row_id = (lo + lax.broadcasted_iota(jnp.int32, (MOE_CHUNK, MOE_ROW_TILE), 1)).astype(F32)
            for c in range(n_chunks):
                @pl.when(overlaps(c, lo))
                def _():
                    toks = slice(c * MOE_CHUNK, (c + 1) * MOE_CHUNK)
                    mine = lane == e
                    rank_col = jnp.sum(jnp.where(mine, rank_ref[toks, :], 0.0), axis=-1, keepdims=True)
                    gate_col = jnp.sum(jnp.where(mine, gate_ref[toks, :], 0.0), axis=-1, keepdims=True)
                    pick = jnp.where(rank_col == row_id, 1.0, 0.0).astype(BF16)
                    o_ref[toks, :] += gate_col * _dot(pick, y_tile)
            return carry
        lax.fori_loop(0, n_r, row_tile, 0)


def _moe(counts, rank, rank_t, gate, h2, wg, wu, wd, *, tokens):
    t, d = h2.shape
    n_exp, _, d_ff = wg.shape
    n_chunks = tokens // MOE_CHUNK
    tf = MOE_FF_TILE
    tok = lambda width: pl.BlockSpec((tokens, width), lambda i, e, f, cnt: (i, 0))
    grid_spec = pltpu.PrefetchScalarGridSpec(
        num_scalar_prefetch=1,
        grid=(t // tokens, n_exp, d_ff // tf),
        in_specs=[tok(LOGIT_LANES),
                  pl.BlockSpec((1, N_EXPERTS, 1, tokens), lambda i, e, f, cnt: (i, 0, 0, 0)),
                  tok(LOGIT_LANES), tok(d),
                  pl.BlockSpec((1, d, tf), lambda i, e, f, cnt: (e, 0, f)),
                  pl.BlockSpec((1, d, tf), lambda i, e, f, cnt: (e, 0, f)),
                  pl.BlockSpec((1, tf, d), lambda i, e, f, cnt: (e, f, 0))],
        out_specs=tok(d),
        scratch_shapes=[pltpu.VMEM((tokens, d), BF16), pltpu.VMEM((tokens, d), F32),
                        pltpu.VMEM((MOE_ROW_TILE, d), F32)])
    return pl.pallas_call(
        functools.partial(_moe_kernel, n_chunks=n_chunks),
        grid_spec=grid_spec,
        out_shape=jax.ShapeDtypeStruct((t, d), F32),
        compiler_params=_params(("arbitrary", "arbitrary", "arbitrary")),
        name="moe_experts",
    )(counts, rank, rank_t, gate, h2, wg, wu, wd)


def _final_kernel(x_ref, y_ref, g2_ref, gf_ref, o_ref):
    o_ref[...] = _rms(x_ref[...] + g2_ref[0] * y_ref[...], gf_ref[...])


def _final(x2d, y, mods, group_of_tile, g_final, *, tm):
    t, d = x2d.shape
    tok = pl.BlockSpec((tm, d), lambda i: (i, 0))
    return pl.pallas_call(
        _final_kernel,
        grid=(t // tm,),
        in_specs=[tok, tok, pl.BlockSpec((1, 1, d), lambda i: (group_of_tile(i), 0, 5)), _const_spec((1, d))],
        out_specs=tok,
        out_shape=jax.ShapeDtypeStruct((t, d), F32),
        compiler_params=_params(("arbitrary",)),
        name="final_norm",
    )(x2d, y, mods, g_final)


def _rope_tables(n):
    t = jnp.arange(n)
    row = (t // GRID_W).astype(F32)
    col = (t % GRID_W).astype(F32)
    n_freq = HEAD_DIM // 4
    inv = ROPE_THETA ** (-jnp.arange(n_freq, dtype=F32) / n_freq)
    ang = jnp.stack([row[:, None] * inv, col[:, None] * inv], axis=1)
    cos = jnp.cos(ang)[:, :, None, :]
    sin = jnp.sin(ang)[:, :, None, :]
    cos_f = jnp.broadcast_to(cos, (n, 2, 2, n_freq)).reshape(n, HEAD_DIM)
    sin_s = jnp.concatenate([-sin, sin], axis=2).reshape(n, HEAD_DIM)
    reps = LANES // HEAD_DIM
    return jnp.tile(cos_f, (1, reps)), jnp.tile(sin_s, (1, reps))


def _pair_perm():
    pairs = SW_HEADS // SW_KV_HEADS
    order = [h for j in range(pairs) for h in (j, j + pairs)]
    return np.concatenate([np.arange(h * HEAD_DIM, (h + 1) * HEAD_DIM) for h in order])


def kernel(x, c, ctx, c_ctx, w_ada, b_ada, g_norm1, g_norm2, w_in, conv_w, na_rpb, sw_sink, g_mix, w_out,
           ffn_w_gate, ffn_w_up, ffn_w_down, w_router, moe_w_gate, moe_w_up, moe_w_down, g_final):
    bsz, n, d = x.shape
    ctx_len = ctx.shape[1]
    depth = w_in.shape[0]
    assert w_in.shape[2] == D_IN and n % GRID_W == 0
    assert depth == 2, "layer 0 dense on both streams, layer 1 routed experts on the latent stream only"
    tm = 512
    tm_ctx = ctx_len
    moe_tokens = min(MOE_TOKENS, bsz * n)
    assert n % tm == 0 and (n // GRID_W) % NA_ROWS_PER_STEP == 0 and n % SW_QBLOCK == 0
    assert (bsz * n) % moe_tokens == 0 and moe_tokens % MOE_CHUNK == 0 and moe_tokens % MOE_M_TILE == 0

    cond_rows = 2 * SUBLANES
    cond = jnp.zeros((cond_rows, d), F32).at[:bsz].set(c).at[bsz].set(c_ctx)
    mods_all = _ada(cond, w_ada, b_ada).reshape(depth, cond_rows, 1, 6 * d)

    lat_group = lambda i: i // (n // tm)
    ctx_group = lambda i: bsz
    cos_t, sin_t = _rope_tables(n)
    perm = _pair_perm()

    xl = x.reshape(bsz * n, d)
    xc = ctx.reshape(bsz * ctx_len, d)
    out = None
    for l in range(depth):
        last = l == depth - 1
        mods = mods_all[l]
        w_l = w_in[l]
        w_l = jnp.concatenate([w_l[:, :OFF_SQ], w_l[:, OFF_SQ:OFF_NK][:, perm], w_l[:, OFF_NK:]], axis=1).astype(BF16)
        g_a = g_mix[l, :CONV_CH].reshape(1, CONV_CH)
        g_b = g_mix[l, CONV_CH:CONV_CH + NA_DIM].reshape(1, NA_DIM)
        g_c = g_mix[l, CONV_CH + NA_DIM:][perm].reshape(1, SW_DIM)
        wo = w_out[l]
        wa = wo[:CONV_CH].astype(BF16)
        wb = wo[CONV_CH:CONV_CH + NA_DIM].astype(BF16)
        wc = wo[CONV_CH + NA_DIM:][perm].astype(BF16)
        gn1 = g_norm1[l].reshape(1, d)
        gn2 = g_norm2[l].reshape(1, d)
        sink = sw_sink[l].astype(F32)
        tab = _na_bias_table(na_rpb[l])

        ya, qn, qs, kn, vn, ks, vs = _inproj(xl, mods, lat_group, gn1, w_l, conv_w[l], g_a, cos_t, sin_t,
                                             seq_len=n, tm=tm, rope=True)
        cya, cqn, cqs, ckn, cvn, cks, cvs = _inproj(xc, mods, ctx_group, gn1, w_l, conv_w[l], g_a, cos_t, sin_t,
                                                    seq_len=ctx_len, tm=tm_ctx, rope=False)
        yb = _na_attention(qn, kn, vn, ckn, cvn, tab, g_b, bsz=bsz, seq_len=n, ctx_len=ctx_len)
        yc = _sw_attention(sink, qs, ks, vs, cks, cvs, g_c, bsz=bsz, seq_len=n, ctx_len=ctx_len)

        if l % 2 == 0:
            i = l // 2
            xl, h2 = _outproj(ya, yb, yc, xl, mods, lat_group, gn2, wa, wb, wc, tm=tm)
            wg = ffn_w_gate[i].astype(BF16)
            wu = ffn_w_up[i].astype(BF16)
            wd = ffn_w_down[i].astype(BF16)
            xl = _ffn(h2, xl, mods, lat_group, wg, wu, wd, tm=tm)
            if not last:
                cyb, cyc = _ctx_attention(sink, cqn, ckn, cvn, cqs, cks, cvs, g_b, g_c, bsz=bsz, ctx_len=ctx_len)
                xc, ch2 = _outproj(cya, cyb, cyc, xc, mods, ctx_group, gn2, wa, wb, wc, tm=tm_ctx)
                xc = _ffn(ch2, xc, mods, ctx_group, wg, wu, wd, tm=tm_ctx)
        else:
            i = l // 2
            wr = jnp.zeros((d, LOGIT_LANES), F32).at[:, :N_EXPERTS].set(w_router[i])
            wr_hi = wr.astype(BF16)
            wr_lo = (wr - wr_hi.astype(F32)).astype(BF16)
            wg = moe_w_gate[i].astype(BF16)
            wu = moe_w_up[i].astype(BF16)
            wd = moe_w_down[i].astype(BF16)
            xl, h2, logits = _outproj(ya, yb, yc, xl, mods, lat_group, gn2, wa, wb, wc, wr_hi, wr_lo, tm=tm)
            gate, rank, rank_t, cnt = _route(logits, tokens=moe_tokens)
            counts = cnt[:, :, 0, :N_EXPERTS].reshape(-1)
            y = _moe(counts, rank, rank_t, gate, h2, wg, wu, wd, tokens=moe_tokens)
            out = _final(xl, y, mods, lat_group, g_final.reshape(1, d), tm=tm)
    return out.reshape(bsz, n, d)
```

```python
import functools

import numpy as np
import jax
import jax.numpy as jnp
from jax import lax
from jax.experimental import pallas as pl
from jax.experimental.pallas import tpu as pltpu

F32 = jnp.float32
BF16 = jnp.bfloat16

GRID_W = 64
HEAD_DIM = 64
CONV_CH = 256
NA_HEADS = 6
NA_DIM = NA_HEADS * HEAD_DIM
SW_HEADS = 6
SW_KV_HEADS = 2
SW_DIM = SW_HEADS * HEAD_DIM
SW_KV_DIM = SW_KV_HEADS * HEAD_DIM
NA_WIN_R = 8
NA_WIN_C = 16
SW_RADIUS = 128
ROPE_THETA = 10000.0
N_EXPERTS = 8
EPS = 1e-6
NEG_INF = -1e30
OFF_H, OFF_B, OFF_C = 0, CONV_CH, 2 * CONV_CH
OFF_NQ = 3 * CONV_CH
OFF_SQ = OFF_NQ + NA_DIM
OFF_NK = OFF_SQ + SW_DIM
OFF_NV = OFF_NK + NA_DIM
OFF_SK = OFF_NV + NA_DIM
OFF_SV = OFF_SK + SW_KV_DIM
D_IN = OFF_SV + SW_KV_DIM
Q_SCALE = HEAD_DIM ** -0.5

LANES = 128
SUBLANES = 8
VMEM_BYTES_V7X = 64 * 1024 * 1024
VMEM_LIMIT = VMEM_BYTES_V7X * 7 // 8

NA_ROWS_PER_STEP = 4
NA_KEY_ROWS = NA_ROWS_PER_STEP + NA_WIN_R
NA_BIAS_MASKED = 2 * NA_WIN_R - 1
MOE_CHUNK = 256
MOE_GRANULE = 2 * SUBLANES
MOE_LOCAL_ROWS = 640
MOE_SET_CHUNKS = 8
MOE_M_TILE = 512
MOE_M_MIN = 128
MOE_FF_TILE = 512
SW_QBLOCK = 256
SW_SPAN = SW_QBLOCK + 2 * SW_RADIUS
LOGIT_LANES = LANES


def _params(semantics):
    return pltpu.CompilerParams(dimension_semantics=semantics, vmem_limit_bytes=VMEM_LIMIT)


def _const_spec(shape):
    nd = len(shape)
    return pl.BlockSpec(shape, lambda *_: (0,) * nd)


def _resident_spec(shape):
    nd = len(shape)
    return pl.BlockSpec(shape, lambda *_: (0,) * nd, pipeline_mode=pl.Buffered(1))


def _rms(x, g):
    y = x * lax.rsqrt(jnp.mean(x * x, axis=-1, keepdims=True) + EPS)
    return y * g


def _silu(x):
    return x / (1.0 + jnp.exp(-x))


def _dot(a, b):
    return jnp.dot(a, b, preferred_element_type=F32)


def _dot_nt(a, b):
    return lax.dot_general(a, b, (((1,), (1,)), ((), ())), preferred_element_type=F32)


def _ada_kernel(c_ref, w_ref, b_ref, o_ref):
    a = _silu(c_ref[...])
    o_ref[0] = jnp.dot(a, w_ref[0], preferred_element_type=F32,
                       precision=lax.Precision.HIGHEST) + b_ref[0]


def _ada(cond, w_ada, b_ada):
    depth, d, n_out = w_ada.shape
    rows = cond.shape[0]
    tn = n_out // 6
    return pl.pallas_call(
        _ada_kernel,
        grid=(depth, n_out // tn),
        in_specs=[pl.BlockSpec((rows, d), lambda l, j: (0, 0)),
                  pl.BlockSpec((1, d, tn), lambda l, j: (l, 0, j)),
                  pl.BlockSpec((1, 1, tn), lambda l, j: (l, 0, j))],
        out_specs=pl.BlockSpec((1, rows, tn), lambda l, j: (l, 0, j)),
        out_shape=jax.ShapeDtypeStruct((depth, rows, n_out), F32),
        compiler_params=_params(("arbitrary", "arbitrary")),
        name="ada_modulation",
    )(cond, w_ada, b_ada.reshape(depth, 1, n_out))


def _rope(x, cos, sin, lane_lo):
    rot = jnp.where(lane_lo, pltpu.roll(x, LANES - 16, 1), pltpu.roll(x, 16, 1))
    return x * cos + rot * sin


def _inproj_kernel(x_ref, xp_ref, xn_ref, sh_ref, sc_ref, gn_ref, w_ref, cw_ref, ga_ref, cos_ref, sin_ref,
                   ya_ref, qn_ref, qs_ref, kn_ref, vn_ref, ks_ref, vs_ref, u_s, *, tm, tiles_per_seq, rope):
    i = pl.program_id(0)
    g = gn_ref[...]
    sh = sh_ref[0]
    sc = sc_ref[0]

    def normmod(x):
        return _rms(x, g) * (1.0 + sc) + sh

    h = normmod(x_ref[...]).astype(BF16)
    p = _dot(h, w_ref[...])

    xh = jnp.concatenate([xp_ref[...], xn_ref[...]], axis=0)
    ph = _dot(normmod(xh).astype(BF16), w_ref[:, :OFF_NQ])
    uh = ph[:, OFF_C:OFF_C + CONV_CH] * ph[:, OFF_H:OFF_H + CONV_CH]
    pos = i % tiles_per_seq
    u = p[:, OFF_C:OFF_C + CONV_CH] * p[:, OFF_H:OFF_H + CONV_CH]
    u_s[0:SUBLANES] = jnp.where(pos == 0, 0.0, uh[0:SUBLANES])
    u_s[SUBLANES:SUBLANES + tm] = u
    u_s[SUBLANES + tm:2 * SUBLANES + tm] = jnp.where(pos == tiles_per_seq - 1, 0.0, uh[SUBLANES:])
    cw = cw_ref[...]
    conv = (u_s[SUBLANES - 1:SUBLANES - 1 + tm] * cw[0:1] + u * cw[1:2]
            + u_s[SUBLANES + 1:SUBLANES + 1 + tm] * cw[2:3])
    ya = p[:, OFF_B:OFF_B + CONV_CH] * conv
    ya_ref[...] = _rms(ya, ga_ref[...]).astype(BF16)

    qn_ref[...] = (p[:, OFF_NQ:OFF_NQ + NA_DIM] * Q_SCALE).astype(BF16)
    kn_ref[...] = p[:, OFF_NK:OFF_NK + NA_DIM].astype(BF16)
    vn_ref[...] = p[:, OFF_NV:OFF_NV + NA_DIM].astype(BF16)
    vs_ref[...] = p[:, OFF_SV:OFF_SV + SW_KV_DIM].astype(BF16)
    if rope:
        cos = cos_ref[...]
        sin = sin_ref[...]
        lane_lo = (lax.broadcasted_iota(jnp.int32, (tm, LANES), 1) % 32) < 16
        for j in range(SW_DIM // LANES):
            q = p[:, OFF_SQ + j * LANES:OFF_SQ + (j + 1) * LANES]
            qs_ref[:, j * LANES:(j + 1) * LANES] = (_rope(q, cos, sin, lane_lo) * Q_SCALE).astype(BF16)
        ks_ref[...] = _rope(p[:, OFF_SK:OFF_SK + SW_KV_DIM], cos, sin, lane_lo).astype(BF16)
    else:
        qs_ref[...] = (p[:, OFF_SQ:OFF_SQ + SW_DIM] * Q_SCALE).astype(BF16)
        ks_ref[...] = p[:, OFF_SK:OFF_SK + SW_KV_DIM].astype(BF16)


def _inproj(x2d, mods, group_of_tile, g_norm, w, conv_w, g_a, cos_t, sin_t, *, seq_len, tm, rope):
    t, d = x2d.shape
    tiles_per_seq = seq_len // tm
    blk8 = tm // SUBLANES
    last8 = t // SUBLANES - 1
    kern = functools.partial(_inproj_kernel, tm=tm, tiles_per_seq=tiles_per_seq, rope=rope)
    tok = lambda width: pl.BlockSpec((tm, width), lambda i: (i, 0))
    out_widths = (CONV_CH, NA_DIM, SW_DIM, NA_DIM, NA_DIM, SW_KV_DIM, SW_KV_DIM)
    return pl.pallas_call(
        kern,
        grid=(t // tm,),
        in_specs=[tok(d),
                  pl.BlockSpec((SUBLANES, d), lambda i: (jnp.maximum(i * blk8 - 1, 0), 0)),
                  pl.BlockSpec((SUBLANES, d), lambda i: (jnp.minimum((i + 1) * blk8, last8), 0)),
                  pl.BlockSpec((1, 1, d), lambda i: (group_of_tile(i), 0, 0)),
                  pl.BlockSpec((1, 1, d), lambda i: (group_of_tile(i), 0, 1)),
                  _const_spec((1, d)),
                  _resident_spec(w.shape),
                  _const_spec(conv_w.shape),
                  _const_spec((1, CONV_CH)),
                  pl.BlockSpec((tm, LANES), lambda i: (i % tiles_per_seq, 0)),
                  pl.BlockSpec((tm, LANES), lambda i: (i % tiles_per_seq, 0))],
        out_specs=[tok(wd) for wd in out_widths],
        out_shape=[jax.ShapeDtypeStruct((t, wd), BF16) for wd in out_widths],
        scratch_shapes=[pltpu.VMEM((tm + 2 * SUBLANES, CONV_CH), F32)],
        compiler_params=_params(("arbitrary",)),
        name="in_projection",
    )(x2d, x2d, x2d, mods, mods, g_norm, w, conv_w, g_a, cos_t, sin_t)


def _head_attention(qm, k_loc, v_loc, add_loc, k_ctx, v_ctx, sink):
    s_ctx = _dot_nt(qm, k_ctx)
    m = jnp.max(s_ctx, axis=-1, keepdims=True)
    if k_loc is not None:
        s_loc = _dot_nt(qm, k_loc) + add_loc
        m = jnp.maximum(m, jnp.max(s_loc, axis=-1, keepdims=True))
    if sink is not None:
        m = jnp.maximum(m, sink)
    p_ctx = jnp.exp(s_ctx - m)
    den = jnp.sum(p_ctx, axis=-1, keepdims=True)
    o = _dot(p_ctx.astype(BF16), v_ctx)
    if k_loc is not None:
        p_loc = jnp.exp(s_loc - m)
        den = den + jnp.sum(p_loc, axis=-1, keepdims=True)
        o = o + _dot(p_loc.astype(BF16), v_loc)
    if sink is not None:
        den = den + jnp.exp(sink - m)
    return o / den


def _pair_attention(q, k_loc, v_loc, add_loc, k_ctx, v_ctx, sinks, lane_first):
    outs = []
    for half in range(2):
        keep = lane_first if half == 0 else jnp.logical_not(lane_first)
        qm = jnp.where(keep, q, jnp.zeros_like(q))
        outs.append(_head_attention(qm, k_loc, v_loc, None if add_loc is None else add_loc[half],
                                    k_ctx, v_ctx, None if sinks is None else sinks[half]))
    return jnp.where(lane_first, outs[0], outs[1])


def _lane_first(m):
    return lax.broadcasted_iota(jnp.int32, (m, LANES), 1) < HEAD_DIM


def _na_kernel(q_ref, k_ref, v_ref, kc_ref, vc_ref, tab_ref, g_ref, o_ref, y_s, bias_s, *, rows):
    gidx = pl.program_id(1)
    base = jnp.clip(gidx * NA_ROWS_PER_STEP - NA_WIN_R // 2, 0, rows - NA_KEY_ROWS)
    start = pl.multiple_of(base * GRID_W, GRID_W)
    nk = NA_KEY_ROWS * GRID_W
    mq = NA_ROWS_PER_STEP * GRID_W
    lane_first = _lane_first(mq)
    tile_first = _lane_first(GRID_W)

    def fill_bias(slot, head):
        for qr in range(NA_ROWS_PER_STEP):
            r = gidx * NA_ROWS_PER_STEP + qr
            rs = jnp.clip(r - NA_WIN_R // 2, 0, rows - NA_WIN_R)
            for m in range(NA_KEY_ROWS // 2):
                idx = []
                for kr in (base + 2 * m, base + 2 * m + 1):
                    ok = (kr >= rs) & (kr < rs + NA_WIN_R)
                    idx.append(jnp.where(ok, kr - r + NA_WIN_R - 1, NA_BIAS_MASKED))
                bias_s[slot, qr * GRID_W:(qr + 1) * GRID_W, m * LANES:(m + 1) * LANES] = jnp.where(
                    tile_first, tab_ref[head, idx[0]], tab_ref[head, idx[1]])

    for j in range(NA_DIM // LANES):
        cols = slice(j * LANES, (j + 1) * LANES)
        k_loc = k_ref[pl.ds(start, nk), cols]
        v_loc = v_ref[pl.ds(start, nk), cols]
        fill_bias(0, 2 * j)
        fill_bias(1, 2 * j + 1)
        y_s[:, cols] = _pair_attention(q_ref[:, cols], k_loc, v_loc, (bias_s[0], bias_s[1]), kc_ref[:, cols],
                                       vc_ref[:, cols], None, lane_first)
    o_ref[...] = _rms(y_s[...], g_ref[...]).astype(BF16)


def _na_bias_table(rpb):
    n_heads, n_dr, n_dc = rpb.shape
    pad = GRID_W - NA_WIN_C
    rp = jnp.pad(rpb.astype(F32), ((0, 0), (0, 0), (pad, pad)))
    toep = jnp.stack([rp[:, :, GRID_W - 1 - qc:2 * GRID_W - 1 - qc] for qc in range(GRID_W)], axis=2)
    qc = np.arange(GRID_W)
    cstart = np.clip(qc - NA_WIN_C // 2, 0, GRID_W - NA_WIN_C)
    col_ok = (qc[None, :] >= cstart[:, None]) & (qc[None, :] < cstart[:, None] + NA_WIN_C)
    tab = jnp.where(col_ok[None, None], toep, NEG_INF)
    masked = jnp.full((n_heads, NA_BIAS_MASKED + 1 - n_dr, GRID_W, GRID_W), NEG_INF, F32)
    tab = jnp.concatenate([tab, masked], axis=1)
    return jnp.tile(tab, (1, 1, 1, LANES // GRID_W))


def _na_attention(qn, kn, vn, kn_c, vn_c, tab, g_b, *, bsz, seq_len, ctx_len):
    rows = seq_len // GRID_W
    n_groups = rows // NA_ROWS_PER_STEP
    mq = NA_ROWS_PER_STEP * GRID_W
    kern = functools.partial(_na_kernel, rows=rows)
    return pl.pallas_call(
        kern,
        grid=(bsz, n_groups),
        in_specs=[pl.BlockSpec((mq, NA_DIM), lambda b, g: (b * n_groups + g, 0)),
                  pl.BlockSpec((seq_len, NA_DIM), lambda b, g: (b, 0)),
                  pl.BlockSpec((seq_len, NA_DIM), lambda b, g: (b, 0)),
                  pl.BlockSpec((ctx_len, NA_DIM), lambda b, g: (b, 0)),
                  pl.BlockSpec((ctx_len, NA_DIM), lambda b, g: (b, 0)),
                  _resident_spec(tab.shape),
                  _const_spec((1, NA_DIM))],
        out_specs=pl.BlockSpec((mq, NA_DIM), lambda b, g: (b * n_groups + g, 0)),
        out_shape=jax.ShapeDtypeStruct(qn.shape, BF16),
        scratch_shapes=[pltpu.VMEM((mq, NA_DIM), F32),
                        pltpu.VMEM((2, mq, NA_KEY_ROWS * GRID_W), F32)],
        compiler_params=_params(("arbitrary", "arbitrary")),
        name="na_attention",
    )(qn, kn, vn, kn_c, vn_c, tab, g_b)


def _sw_kernel(sink_ref, q_ref, k_ref, v_ref, kc_ref, vc_ref, g_ref, o_ref, y_s, *, seq_len):
    i = pl.program_id(1)
    s0 = i * SW_QBLOCK
    kstart = pl.multiple_of(jnp.clip(s0 - SW_RADIUS, 0, seq_len - SW_SPAN), SW_RADIUS)
    k_loc = k_ref[pl.ds(kstart, SW_SPAN), :]
    v_loc = v_ref[pl.ds(kstart, SW_SPAN), :]
    qpos = s0 + lax.broadcasted_iota(jnp.int32, (SW_QBLOCK, SW_SPAN), 0)
    kpos = kstart + lax.broadcasted_iota(jnp.int32, (SW_QBLOCK, SW_SPAN), 1)
    band = jnp.where(jnp.abs(kpos - qpos) <= SW_RADIUS, 0.0, NEG_INF).astype(F32)
    lane_first = _lane_first(SW_QBLOCK)
    pairs = SW_HEADS // SW_KV_HEADS
    for j in range(pairs):
        cols = slice(j * LANES, (j + 1) * LANES)
        sinks = (sink_ref[j], sink_ref[j + pairs])
        y_s[:, cols] = _pair_attention(q_ref[:, cols], k_loc, v_loc, (band, band), kc_ref[...], vc_ref[...],
                                       sinks, lane_first)
    o_ref[...] = _rms(y_s[...], g_ref[...]).astype(BF16)


def _sw_attention(sink, qs, ks, vs, ks_c, vs_c, g_c, *, bsz, seq_len, ctx_len):
    nblk = seq_len // SW_QBLOCK
    kern = functools.partial(_sw_kernel, seq_len=seq_len)
    return pl.pallas_call(
        kern,
        grid=(bsz, nblk),
        in_specs=[pl.BlockSpec(memory_space=pltpu.SMEM),
                  pl.BlockSpec((SW_QBLOCK, SW_DIM), lambda b, i: (b * nblk + i, 0)),
                  pl.BlockSpec((seq_len, SW_KV_DIM), lambda b, i: (b, 0)),
                  pl.BlockSpec((seq_len, SW_KV_DIM), lambda b, i: (b, 0)),
                  pl.BlockSpec((ctx_len, SW_KV_DIM), lambda b, i: (b, 0)),
                  pl.BlockSpec((ctx_len, SW_KV_DIM), lambda b, i: (b, 0)),
                  _const_spec((1, SW_DIM))],
        out_specs=pl.BlockSpec((SW_QBLOCK, SW_DIM), lambda b, i: (b * nblk + i, 0)),
        out_shape=jax.ShapeDtypeStruct(qs.shape, BF16),
        scratch_shapes=[pltpu.VMEM((SW_QBLOCK, SW_DIM), F32)],
        compiler_params=_params(("arbitrary", "arbitrary")),
        name="sw_attention",
    )(sink, qs, ks, vs, ks_c, vs_c, g_c)


def _ctx_attn_kernel(sink_ref, qn_ref, kn_ref, vn_ref, qs_ref, ks_ref, vs_ref, gb_ref, gc_ref,
                     yb_ref, yc_ref, y_s, *, ctx_len):
    lane_first = _lane_first(ctx_len)
    for j in range(NA_DIM // LANES):
        cols = slice(j * LANES, (j + 1) * LANES)
        y_s[:, cols] = _pair_attention(qn_ref[:, cols], None, None, None, kn_ref[:, cols], vn_ref[:, cols],
                                       None, lane_first)
    yb_ref[...] = _rms(y_s[...], gb_ref[...]).astype(BF16)
    pairs = SW_HEADS // SW_KV_HEADS
    for j in range(pairs):
        cols = slice(j * LANES, (j + 1) * LANES)
        sinks = (sink_ref[j], sink_ref[j + pairs])
        y_s[:, cols] = _pair_attention(qs_ref[:, cols], None, None, None, ks_ref[...], vs_ref[...],
                                       sinks, lane_first)
    yc_ref[...] = _rms(y_s[...], gc_ref[...]).astype(BF16)


def _ctx_attention(sink, qn, kn, vn, qs, ks, vs, g_b, g_c, *, bsz, ctx_len):
    kern = functools.partial(_ctx_attn_kernel, ctx_len=ctx_len)
    tok = lambda width: pl.BlockSpec((ctx_len, width), lambda b: (b, 0))
    return pl.pallas_call(
        kern,
        grid=(bsz,),
        in_specs=[pl.BlockSpec(memory_space=pltpu.SMEM),
                  tok(NA_DIM), tok(NA_DIM), tok(NA_DIM), tok(SW_DIM), tok(SW_KV_DIM), tok(SW_KV_DIM),
                  _const_spec((1, NA_DIM)), _const_spec((1, SW_DIM))],
        out_specs=[tok(NA_DIM), tok(SW_DIM)],
        out_shape=[jax.ShapeDtypeStruct(qn.shape, BF16), jax.ShapeDtypeStruct(qs.shape, BF16)],
        scratch_shapes=[pltpu.VMEM((ctx_len, NA_DIM), F32)],
        compiler_params=_params(("arbitrary",)),
        name="ctx_attention",
    )(sink, qn, kn, vn, qs, ks, vs, g_b, g_c)


def _mix_residual(ya_ref, yb_ref, yc_ref, x_ref, g1_ref, sh_ref, sc_ref, gn_ref, wa_ref, wb_ref, wc_ref):
    mix = _dot(ya_ref[...], wa_ref[...]) + _dot(yb_ref[...], wb_ref[...]) + _dot(yc_ref[...], wc_ref[...])
    xl = x_ref[...] + g1_ref[0] * mix
    h = _rms(xl, gn_ref[...]) * (1.0 + sc_ref[0]) + sh_ref[0]
    return xl, h


def _mix_specs(x2d, mods, group_of_tile, g_norm, ys, ws, tm):
    d = x2d.shape[1]
    tok = lambda width: pl.BlockSpec((tm, width), lambda i: (i, 0))
    mod = lambda j: pl.BlockSpec((1, 1, d), lambda i: (group_of_tile(i), 0, j))
    in_specs = ([tok(y.shape[1]) for y in ys] + [tok(d), mod(2), mod(3), mod(4), _const_spec((1, d))]
                + [_resident_spec(w.shape) for w in ws])
    return in_specs, [*ys, x2d, mods, mods, mods, g_norm, *ws]


def _outproj_router_kernel(*refs):
    *mix_refs, r_ref, xo_ref, h_ref, lg_ref = refs
    xl, h = _mix_residual(*mix_refs)
    xo_ref[...] = xl
    hi = h.astype(BF16)
    h_ref[...] = hi
    tm = h.shape[0]
    lo = (h - hi.astype(F32)).astype(BF16)
    r = _dot(jnp.concatenate([hi, lo], axis=0), r_ref[...])
    lg_ref[...] = (r[:tm, :LOGIT_LANES] + r[:tm, LOGIT_LANES:]) + (r[tm:, :LOGIT_LANES] + r[tm:, LOGIT_LANES:])


def _outproj_router(ys, x2d, mods, group_of_tile, g_norm, ws, router_w, *, tm):
    t, d = x2d.shape
    in_specs, args = _mix_specs(x2d, mods, group_of_tile, g_norm, ys, ws, tm)
    tok = lambda width: pl.BlockSpec((tm, width), lambda i: (i, 0))
    return pl.pallas_call(
        _outproj_router_kernel,
        grid=(t // tm,),
        in_specs=in_specs + [_resident_spec(router_w.shape)],
        out_specs=[tok(d), tok(d), tok(LOGIT_LANES)],
        out_shape=[jax.ShapeDtypeStruct((t, d), F32), jax.ShapeDtypeStruct((t, d), BF16),
                   jax.ShapeDtypeStruct((t, LOGIT_LANES), F32)],
        compiler_params=_params(("arbitrary",)),
        name="out_projection_router",
    )(*args, router_w)


def _outproj_ffn_kernel(*refs, chunks):
    *mix_refs, g2_ref, wg_ref, wu_ref, wd_ref, o_ref = refs
    xl, h = _mix_residual(*mix_refs)
    h = h.astype(BF16)
    acc = None
    for lo, hi in chunks:
        a = _silu(_dot(h, wg_ref[:, lo:hi])) * _dot(h, wu_ref[:, lo:hi])
        part = _dot(a.astype(BF16), wd_ref[lo:hi, :])
        acc = part if acc is None else acc + part
    o_ref[...] = xl + g2_ref[0] * acc


def _outproj_ffn(ys, x2d, mods, group_of_tile, g_norm, ws, wg, wu, wd, *, tm):
    t, d = x2d.shape
    d_ff = wg.shape[1]
    step = 1024
    chunks = tuple((lo, min(lo + step, d_ff)) for lo in range(0, d_ff, step))
    in_specs, args = _mix_specs(x2d, mods, group_of_tile, g_norm, ys, ws, tm)
    return pl.pallas_call(
        functools.partial(_outproj_ffn_kernel, chunks=chunks),
        grid=(t // tm,),
        in_specs=in_specs + [pl.BlockSpec((1, 1, d), lambda i: (group_of_tile(i), 0, 5)),
                             _resident_spec(wg.shape), _resident_spec(wu.shape), _resident_spec(wd.shape)],
        out_specs=pl.BlockSpec((tm, d), lambda i: (i, 0)),
        out_shape=jax.ShapeDtypeStruct((t, d), F32),
        compiler_params=_params(("arbitrary",)),
        name="out_projection_ffn",
    )(*args, mods, wg, wu, wd)


def _route_kernel(lg_ref, h_ref, hl_ref, pos_ref, gsel_ref, meta_ref):
    logits = lg_ref[...]
    lane = lax.broadcasted_iota(jnp.int32, logits.shape, 1)
    lg = jnp.where(lane < N_EXPERTS, logits, -jnp.inf)
    v1 = jnp.max(lg, axis=-1, keepdims=True)
    i1 = jnp.min(jnp.where(lg == v1, lane, LOGIT_LANES), axis=-1, keepdims=True)
    lg2 = jnp.where(lane == i1, -jnp.inf, lg)
    v2 = jnp.max(lg2, axis=-1, keepdims=True)
    i2 = jnp.min(jnp.where(lg2 == v2, lane, LOGIT_LANES), axis=-1, keepdims=True)
    e2 = jnp.exp(v2 - v1)
    den = 1.0 + e2
    g_first = 1.0 / den
    g_second = e2 / den
    sel1 = lane == i1
    sel2 = lane == i2
    onehot = jnp.where(sel1, 1.0, jnp.where(sel2, 1.0, 0.0))

    r_i = lax.broadcasted_iota(jnp.int32, (MOE_CHUNK, MOE_CHUNK), 0)
    c_i = lax.broadcasted_iota(jnp.int32, (MOE_CHUNK, MOE_CHUNK), 1)
    earlier = jnp.where(c_i < r_i, 1.0, 0.0).astype(BF16)
    within = _dot(earlier, onehot.astype(BF16))
    count = jnp.sum(onehot, axis=0, keepdims=True)
    n_gran = jnp.floor((count + (MOE_GRANULE - 1)) * (1.0 / MOE_GRANULE))
    u_r = lax.broadcasted_iota(jnp.int32, (LOGIT_LANES, LOGIT_LANES), 0)
    u_c = lax.broadcasted_iota(jnp.int32, (LOGIT_LANES, LOGIT_LANES), 1)
    before = jnp.where(u_r < u_c, 1.0, 0.0).astype(BF16)
    n_gran8 = jnp.broadcast_to(n_gran, (SUBLANES, LOGIT_LANES))
    off_gran = _dot(n_gran8.astype(BF16), before)[0:1]
    lpos = off_gran * MOE_GRANULE + within
    p_first = jnp.sum(jnp.where(sel1, lpos, 0.0), axis=-1, keepdims=True)
    p_second = jnp.sum(jnp.where(sel2, lpos, 0.0), axis=-1, keepdims=True)
    pos = jnp.where(lane == 0, p_first, jnp.where(lane == 1, p_second, -1.0))
    pos_ref[...] = pos
    gsel_ref[...] = jnp.where(lane == 0, g_first, jnp.where(lane == 1, g_second, 0.0))
    rows8 = lax.broadcasted_iota(jnp.int32, (SUBLANES, LOGIT_LANES), 0)
    meta_ref[0] = jnp.where(rows8 == 0, off_gran, jnp.where(rows8 == 1, n_gran, 0.0)).astype(jnp.int32)

    pos_t = pos.T
    row_id = lax.broadcasted_iota(jnp.int32, (MOE_LOCAL_ROWS, MOE_CHUNK), 0).astype(F32)
    pick = jnp.where(row_id == pos_t[0:1], 1.0, jnp.where(row_id == pos_t[1:2], 1.0, 0.0)).astype(BF16)
    hl_ref[0] = _dot(pick, h_ref[...]).astype(BF16)


def _route(logits, h2):
    t, d = h2.shape
    n_chunks = t // MOE_CHUNK
    tok = lambda width: pl.BlockSpec((MOE_CHUNK, width), lambda i: (i, 0))
    return pl.pallas_call(
        _route_kernel,
        grid=(n_chunks,),
        in_specs=[tok(LOGIT_LANES), tok(d)],
        out_specs=[pl.BlockSpec((1, MOE_LOCAL_ROWS, d), lambda i: (i, 0, 0)),
                   tok(LOGIT_LANES), tok(LOGIT_LANES),
                   pl.BlockSpec((1, SUBLANES, LOGIT_LANES), lambda i: (i, 0, 0))],
        out_shape=[jax.ShapeDtypeStruct((n_chunks, MOE_LOCAL_ROWS, d), BF16),
                   jax.ShapeDtypeStruct((t, LOGIT_LANES), F32),
                   jax.ShapeDtypeStruct((t, LOGIT_LANES), F32),
                   jax.ShapeDtypeStruct((n_chunks, SUBLANES, LOGIT_LANES), jnp.int32)],
        compiler_params=_params(("arbitrary",)),
        name="moe_route",
    )(logits, h2)


def _experts_kernel(meta_ref, hl_ref, wg_ref, wu_ref, wd_ref, yl_ref, he_s, ye_s, *, n_chunks):
    ws = pl.program_id(0)
    e = pl.program_id(1)
    f = pl.program_id(2)
    n_f = pl.num_programs(2)

    def meta(c, k):
        return meta_ref[((ws * n_chunks + c) * 2 + k) * N_EXPERTS + e]

    @pl.when((ws == 0) & (e == 0) & (f == 0))
    def _():
        he_s[...] = jnp.zeros_like(he_s)

    @pl.when((e == 0) & (f == 0))
    def _():
        yl_ref[...] = jnp.zeros_like(yl_ref)

    def copy_runs(copy_granule):
        dst = 0
        for c in range(n_chunks):
            src, n = meta(c, 0), meta(c, 1)

            def body(i, carry, c=c, src=src, dst=dst):
                copy_granule(c, pl.multiple_of((src + i) * MOE_GRANULE, MOE_GRANULE),
                             pl.multiple_of((dst + i) * MOE_GRANULE, MOE_GRANULE))
                return carry
            lax.fori_loop(0, n, body, 0)
            dst = dst + n
        return dst

    @pl.when(f == 0)
    def _():
        def load_granule(c, src_row, dst_row):
            he_s[pl.ds(dst_row, MOE_GRANULE), :] = hl_ref[c, pl.ds(src_row, MOE_GRANULE), :]
        copy_runs(load_granule)

    def m_tile(start, size):
        rows = pl.ds(pl.multiple_of(start, MOE_M_MIN), size)
        hrow = he_s[rows, :]
        a = _silu(_dot(hrow, wg_ref[0])) * _dot(hrow, wu_ref[0])
        part = _dot(a.astype(BF16), wd_ref[0])

        @pl.when(f == 0)
        def _():
            ye_s[rows, :] = part

        @pl.when(f > 0)
        def _():
            ye_s[rows, :] += part

    n_rows = sum(meta(c, 1) for c in range(n_chunks)) * MOE_GRANULE
    n_full = n_rows // MOE_M_TILE
    rem = n_rows - n_full * MOE_M_TILE

    def full_tile(m, carry):
        m_tile(m * MOE_M_TILE, MOE_M_TILE)
        return carry
    lax.fori_loop(0, n_full, full_tile, 0)
    size = MOE_M_TILE
    while size >= MOE_M_MIN:
        lo = 0 if size == MOE_M_MIN else size // 2

        @pl.when((rem > lo) & (rem <= size))
        def _(size=size):
            m_tile(n_full * MOE_M_TILE, size)
        size //= 2

    @pl.when(f == n_f - 1)
    def _():
        def store_granule(c, src_row, dst_row):
            yl_ref[c, pl.ds(src_row, MOE_GRANULE), :] = ye_s[pl.ds(dst_row, MOE_GRANULE), :].astype(BF16)
        copy_runs(store_granule)


def _experts(meta, hl, wg, wu, wd, *, n_chunks):
    total_chunks, local_rows, d = hl.shape
    n_exp, _, d_ff = wg.shape
    tf = MOE_FF_TILE
    max_rows = n_chunks * MOE_CHUNK
    work = lambda: pl.BlockSpec((n_chunks, local_rows, d), lambda i, e, f, m: (i, 0, 0),
                                pipeline_mode=pl.Buffered(1))
    grid_spec = pltpu.PrefetchScalarGridSpec(
        num_scalar_prefetch=1,
        grid=(total_chunks // n_chunks, n_exp, d_ff // tf),
        in_specs=[work(),
                  pl.BlockSpec((1, d, tf), lambda i, e, f, m: (e, 0, f)),
                  pl.BlockSpec((1, d, tf), lambda i, e, f, m: (e, 0, f)),
                  pl.BlockSpec((1, tf, d), lambda i, e, f, m: (e, f, 0))],
        out_specs=work(),
        scratch_shapes=[pltpu.VMEM((max_rows, d), BF16), pltpu.VMEM((max_rows, d), F32)])
    return pl.pallas_call(
        functools.partial(_experts_kernel, n_chunks=n_chunks),
        grid_spec=grid_spec,
        out_shape=jax.ShapeDtypeStruct(hl.shape, BF16),
        compiler_params=_params(("arbitrary", "arbitrary", "arbitrary")),
        name="moe_experts",
    )(meta, hl, wg, wu, wd)


def _scatter_kernel(yl_ref, pos_ref, gsel_ref, x_ref, g2_ref, gf_ref, o_ref):
    pos = pos_ref[...]
    gsel = gsel_ref[...]
    yl = yl_ref[0]
    row_id = lax.broadcasted_iota(jnp.int32, (MOE_CHUNK, MOE_LOCAL_ROWS), 1).astype(F32)
    y = None
    for k in range(2):
        pick = jnp.where(pos[:, k:k + 1] == row_id, 1.0, 0.0).astype(BF16)
        term = gsel[:, k:k + 1] * _dot(pick, yl)
        y = term if y is None else y + term
    o_ref[...] = _rms(x_ref[...] + g2_ref[0] * y, gf_ref[...])


def _scatter_final(yl, pos, gsel, x2d, mods, group_of_chunk, g_final):
    t, d = x2d.shape
    tok = lambda width: pl.BlockSpec((MOE_CHUNK, width), lambda i: (i, 0))
    return pl.pallas_call(
        _scatter_kernel,
        grid=(t // MOE_CHUNK,),
        in_specs=[pl.BlockSpec((1, MOE_LOCAL_ROWS, d), lambda i: (i, 0, 0)),
                  tok(LOGIT_LANES), tok(LOGIT_LANES), tok(d),
                  pl.BlockSpec((1, 1, d), lambda i: (group_of_chunk(i), 0, 5)), _const_spec((1, d))],
        out_specs=tok(d),
        out_shape=jax.ShapeDtypeStruct((t, d), F32),
        compiler_params=_params(("arbitrary",)),
        name="moe_scatter_final",
    )(yl, pos, gsel, x2d, mods, g_final)


def _rope_tables(n):
    t = jnp.arange(n)
    row = (t // GRID_W).astype(F32)
    col = (t % GRID_W).astype(F32)
    n_freq = HEAD_DIM // 4
    inv = ROPE_THETA ** (-jnp.arange(n_freq, dtype=F32) / n_freq)
    ang = jnp.stack([row[:, None] * inv, col[:, None] * inv], axis=1)
    cos = jnp.cos(ang)[:, :, None, :]
    sin = jnp.sin(ang)[:, :, None, :]
    cos_f = jnp.broadcast_to(cos, (n, 2, 2, n_freq)).reshape(n, HEAD_DIM)
    sin_s = jnp.concatenate([-sin, sin], axis=2).reshape(n, HEAD_DIM)
    reps = LANES // HEAD_DIM
    return jnp.tile(cos_f, (1, reps)), jnp.tile(sin_s, (1, reps))


def _pair_perm():
    pairs = SW_HEADS // SW_KV_HEADS
    order = [h for j in range(pairs) for h in (j, j + pairs)]
    return np.concatenate([np.arange(h * HEAD_DIM, (h + 1) * HEAD_DIM) for h in order])


def kernel(x, c, ctx, c_ctx, w_ada, b_ada, g_norm1, g_norm2, w_in, conv_w, na_rpb, sw_sink, g_mix, w_out,
           ffn_w_gate, ffn_w_up, ffn_w_down, w_router, moe_w_gate, moe_w_up, moe_w_down, g_final):
    bsz, n, d = x.shape
    ctx_len = ctx.shape[1]
    depth = w_in.shape[0]
    assert w_in.shape[2] == D_IN and n % GRID_W == 0
    assert depth == 2, "layer 0 dense on both streams, layer 1 routed experts on the latent stream only"
    tm = 512
    tm_ctx = ctx_len
    set_chunks = MOE_SET_CHUNKS
    assert n % tm == 0 and (n // GRID_W) % NA_ROWS_PER_STEP == 0 and n % SW_QBLOCK == 0
    assert n % MOE_CHUNK == 0 and (bsz * n) % (MOE_CHUNK * set_chunks) == 0
    assert (MOE_CHUNK * set_chunks) % MOE_M_TILE == 0

    cond_rows = 2 * SUBLANES
    cond = jnp.zeros((cond_rows, d), F32).at[:bsz].set(c).at[bsz].set(c_ctx)
    mods_all = _ada(cond, w_ada, b_ada).reshape(depth, cond_rows, 1, 6 * d)

    lat_group = lambda i: i // (n // tm)
    ctx_group = lambda i: bsz
    cos_t, sin_t = _rope_tables(n)
    perm = _pair_perm()

    xl = x.reshape(bsz * n, d)
    xc = ctx.reshape(bsz * ctx_len, d)
    out = None
    for l in range(depth):
        last = l == depth - 1
        mods = mods_all[l]
        w_l = w_in[l]
        w_l = jnp.concatenate([w_l[:, :OFF_SQ], w_l[:, OFF_SQ:OFF_NK][:, perm], w_l[:, OFF_NK:]], axis=1).astype(BF16)
        g_a = g_mix[l, :CONV_CH].reshape(1, CONV_CH)
        g_b = g_mix[l, CONV_CH:CONV_CH + NA_DIM].reshape(1, NA_DIM)
        g_c = g_mix[l, CONV_CH + NA_DIM:][perm].reshape(1, SW_DIM)
        wo = w_out[l]
        wa = wo[:CONV_CH].astype(BF16)
        wb = wo[CONV_CH:CONV_CH + NA_DIM].astype(BF16)
        wc = wo[CONV_CH + NA_DIM:][perm].astype(BF16)
        gn1 = g_norm1[l].reshape(1, d)
        gn2 = g_norm2[l].reshape(1, d)
        sink = sw_sink[l].astype(F32)
        tab = _na_bias_table(na_rpb[l])

        ya, qn, qs, kn, vn, ks, vs = _inproj(xl, mods, lat_group, gn1, w_l, conv_w[l], g_a, cos_t, sin_t,
                                             seq_len=n, tm=tm, rope=True)
        cya, cqn, cqs, ckn, cvn, cks, cvs = _inproj(xc, mods, ctx_group, gn1, w_l, conv_w[l], g_a, cos_t, sin_t,
                                                    seq_len=ctx_len, tm=tm_ctx, rope=False)
        yb = _na_attention(qn, kn, vn, ckn, cvn, tab, g_b, bsz=bsz, seq_len=n, ctx_len=ctx_len)
        yc = _sw_attention(sink, qs, ks, vs, cks, cvs, g_c, bsz=bsz, seq_len=n, ctx_len=ctx_len)

        if l % 2 == 0:
            i = l // 2
            wg = ffn_w_gate[i].astype(BF16)
            wu = ffn_w_up[i].astype(BF16)
            wd = ffn_w_down[i].astype(BF16)
            xl = _outproj_ffn((ya, yb, yc), xl, mods, lat_group, gn2, (wa, wb, wc), wg, wu, wd, tm=tm)
            if not last:
                cyb, cyc = _ctx_attention(sink, cqn, ckn, cvn, cqs, cks, cvs, g_b, g_c, bsz=bsz, ctx_len=ctx_len)
                xc = _outproj_ffn((cya, cyb, cyc), xc, mods, ctx_group, gn2, (wa, wb, wc), wg, wu, wd,
                                  tm=tm_ctx)
        else:
            i = l // 2
            wr = jnp.zeros((d, LOGIT_LANES), F32).at[:, :N_EXPERTS].set(w_router[i])
            wr_hi = wr.astype(BF16)
            wr_lo = (wr - wr_hi.astype(F32)).astype(BF16)
            wr2 = jnp.concatenate([wr_hi, wr_lo], axis=1)
            wg = moe_w_gate[i].astype(BF16)
            wu = moe_w_up[i].astype(BF16)
            wd = moe_w_down[i].astype(BF16)
            xl, h2, logits = _outproj_router((ya, yb, yc), xl, mods, lat_group, gn2, (wa, wb, wc), wr2, tm=tm)
            hl, pos, gsel, meta = _route(logits, h2)
            meta = meta[:, :2, :N_EXPERTS].reshape(-1)
            yl = _experts(meta, hl, wg, wu, wd, n_chunks=set_chunks)
            out = _scatter_final(yl, pos, gsel, xl, mods, lambda i: i // (n // MOE_CHUNK), g_final.reshape(1, d))
    return out.reshape(bsz, n, d)
```

```python
import functools

import numpy as np
import jax
import jax.numpy as jnp
from jax import lax
from jax.experimental import pallas as pl
from jax.experimental.pallas import tpu as pltpu

F32 = jnp.float32
BF16 = jnp.bfloat16

GRID_W = 64
HEAD_DIM = 64
CONV_CH = 256
NA_HEADS = 6
NA_DIM = NA_HEADS * HEAD_DIM
SW_HEADS = 6
SW_KV_HEADS = 2
SW_DIM = SW_HEADS * HEAD_DIM
SW_KV_DIM = SW_KV_HEADS * HEAD_DIM
NA_WIN_R = 8
NA_WIN_C = 16
SW_RADIUS = 128
ROPE_THETA = 10000.0
N_EXPERTS = 8
EPS = 1e-6
NEG_INF = -1e30
OFF_H, OFF_B, OFF_C = 0, CONV_CH, 2 * CONV_CH
OFF_NQ = 3 * CONV_CH
OFF_SQ = OFF_NQ + NA_DIM
OFF_NK = OFF_SQ + SW_DIM
OFF_NV = OFF_NK + NA_DIM
OFF_SK = OFF_NV + NA_DIM
OFF_SV = OFF_SK + SW_KV_DIM
D_IN = OFF_SV + SW_KV_DIM
Q_SCALE = HEAD_DIM ** -0.5

LANES = 128
SUBLANES = 8
VMEM_BYTES_V7X = 64 * 1024 * 1024
VMEM_LIMIT = VMEM_BYTES_V7X * 7 // 8

NA_ROWS_PER_STEP = 4
NA_KEY_ROWS = NA_ROWS_PER_STEP + NA_WIN_R
NA_BIAS_MASKED = 2 * NA_WIN_R - 1
MOE_CHUNK = 256
MOE_GRANULE = 2 * SUBLANES
MOE_LOCAL_ROWS = 640
MOE_SET_CHUNKS = 16
MOE_STEP_CHUNKS = 2
MOE_M_TILE = 512
MOE_M_MIN = 128
MOE_FF_TILE = 512
SW_QBLOCK = 256
SW_SPAN = SW_QBLOCK + 2 * SW_RADIUS
LOGIT_LANES = LANES


def _params(semantics):
    return pltpu.CompilerParams(dimension_semantics=semantics, vmem_limit_bytes=VMEM_LIMIT)


def _const_spec(shape):
    nd = len(shape)
    return pl.BlockSpec(shape, lambda *_: (0,) * nd)


def _resident_spec(shape):
    nd = len(shape)
    return pl.BlockSpec(shape, lambda *_: (0,) * nd, pipeline_mode=pl.Buffered(1))


def _rms(x, g):
    y = x * lax.rsqrt(jnp.mean(x * x, axis=-1, keepdims=True) + EPS)
    return y * g


def _silu(x):
    return x / (1.0 + jnp.exp(-x))


def _dot(a, b):
    return jnp.dot(a, b, preferred_element_type=F32)


def _dot_nt(a, b):
    return lax.dot_general(a, b, (((1,), (1,)), ((), ())), preferred_element_type=F32)


def _ada_kernel(c_ref, w_ref, b_ref, o_ref):
    a = _silu(c_ref[...])
    o_ref[0] = jnp.dot(a, w_ref[0], preferred_element_type=F32,
                       precision=lax.Precision.HIGHEST) + b_ref[0]


def _ada(cond, w_ada, b_ada):
    depth, d, n_out = w_ada.shape
    rows = cond.shape[0]
    tn = n_out // 6
    return pl.pallas_call(
        _ada_kernel,
        grid=(depth, n_out // tn),
        in_specs=[pl.BlockSpec((rows, d), lambda l, j: (0, 0)),
                  pl.BlockSpec((1, d, tn), lambda l, j: (l, 0, j)),
                  pl.BlockSpec((1, 1, tn), lambda l, j: (l, 0, j))],
        out_specs=pl.BlockSpec((1, rows, tn), lambda l, j: (l, 0, j)),
        out_shape=jax.ShapeDtypeStruct((depth, rows, n_out), F32),
        compiler_params=_params(("arbitrary", "arbitrary")),
        name="ada_modulation",
    )(cond, w_ada, b_ada.reshape(depth, 1, n_out))


def _rope(x, cos, sin, lane_lo):
    rot = jnp.where(lane_lo, pltpu.roll(x, LANES - 16, 1), pltpu.roll(x, 16, 1))
    return x * cos + rot * sin


def _inproj_kernel(x_ref, xp_ref, xn_ref, sh_ref, sc_ref, gn_ref, w_ref, cw_ref, ga_ref, cos_ref, sin_ref,
                   ya_ref, qn_ref, qs_ref, kn_ref, vn_ref, ks_ref, vs_ref, u_s, *, tm, tiles_per_seq, rope):
    i = pl.program_id(0)
    g = gn_ref[...]
    sh = sh_ref[0]
    sc = sc_ref[0]

    def normmod(x):
        return _rms(x, g) * (1.0 + sc) + sh

    h = normmod(x_ref[...]).astype(BF16)
    p = _dot(h, w_ref[...])

    xh = jnp.concatenate([xp_ref[...], xn_ref[...]], axis=0)
    ph = _dot(normmod(xh).astype(BF16), w_ref[:, :OFF_NQ])
    uh = ph[:, OFF_C:OFF_C + CONV_CH] * ph[:, OFF_H:OFF_H + CONV_CH]
    pos = i % tiles_per_seq
    u = p[:, OFF_C:OFF_C + CONV_CH] * p[:, OFF_H:OFF_H + CONV_CH]
    u_s[0:SUBLANES] = jnp.where(pos == 0, 0.0, uh[0:SUBLANES])
    u_s[SUBLANES:SUBLANES + tm] = u
    u_s[SUBLANES + tm:2 * SUBLANES + tm] = jnp.where(pos == tiles_per_seq - 1, 0.0, uh[SUBLANES:])
    cw = cw_ref[...]
    conv = (u_s[SUBLANES - 1:SUBLANES - 1 + tm] * cw[0:1] + u * cw[1:2]
            + u_s[SUBLANES + 1:SUBLANES + 1 + tm] * cw[2:3])
    ya = p[:, OFF_B:OFF_B + CONV_CH] * conv
    ya_ref[...] = _rms(ya, ga_ref[...]).astype(BF16)

    qn_ref[...] = (p[:, OFF_NQ:OFF_NQ + NA_DIM] * Q_SCALE).astype(BF16)
    kn_ref[...] = p[:, OFF_NK:OFF_NK + NA_DIM].astype(BF16)
    vn_ref[...] = p[:, OFF_NV:OFF_NV + NA_DIM].astype(BF16)
    vs_ref[...] = p[:, OFF_SV:OFF_SV + SW_KV_DIM].astype(BF16)
    if rope:
        cos = cos_ref[...]
        sin = sin_ref[...]
        lane_lo = (lax.broadcasted_iota(jnp.int32, (tm, LANES), 1) % 32) < 16
        for j in range(SW_DIM // LANES):
            q = p[:, OFF_SQ + j * LANES:OFF_SQ + (j + 1) * LANES]
            qs_ref[:, j * LANES:(j + 1) * LANES] = (_rope(q, cos, sin, lane_lo) * Q_SCALE).astype(BF16)
        ks_ref[...] = _rope(p[:, OFF_SK:OFF_SK + SW_KV_DIM], cos, sin, lane_lo).astype(BF16)
    else:
        qs_ref[...] = (p[:, OFF_SQ:OFF_SQ + SW_DIM] * Q_SCALE).astype(BF16)
        ks_ref[...] = p[:, OFF_SK:OFF_SK + SW_KV_DIM].astype(BF16)


def _inproj(x2d, mods, group_of_tile, g_norm, w, conv_w, g_a, cos_t, sin_t, *, seq_len, tm, rope):
    t, d = x2d.shape
    tiles_per_seq = seq_len // tm
    blk8 = tm // SUBLANES
    last8 = t // SUBLANES - 1
    kern = functools.partial(_inproj_kernel, tm=tm, tiles_per_seq=tiles_per_seq, rope=rope)
    tok = lambda width: pl.BlockSpec((tm, width), lambda i: (i, 0))
    out_widths = (CONV_CH, NA_DIM, SW_DIM, NA_DIM, NA_DIM, SW_KV_DIM, SW_KV_DIM)
    return pl.pallas_call(
        kern,
        grid=(t // tm,),
        in_specs=[tok(d),
                  pl.BlockSpec((SUBLANES, d), lambda i: (jnp.maximum(i * blk8 - 1, 0), 0)),
                  pl.BlockSpec((SUBLANES, d), lambda i: (jnp.minimum((i + 1) * blk8, last8), 0)),
                  pl.BlockSpec((1, 1, d), lambda i: (group_of_tile(i), 0, 0)),
                  pl.BlockSpec((1, 1, d), lambda i: (group_of_tile(i), 0, 1)),
                  _const_spec((1, d)),
                  _resident_spec(w.shape),
                  _const_spec(conv_w.shape),
                  _const_spec((1, CONV_CH)),
                  pl.BlockSpec((tm, LANES), lambda i: (i % tiles_per_seq, 0)),
                  pl.BlockSpec((tm, LANES), lambda i: (i % tiles_per_seq, 0))],
        out_specs=[tok(wd) for wd in out_widths],
        out_shape=[jax.ShapeDtypeStruct((t, wd), BF16) for wd in out_widths],
        scratch_shapes=[pltpu.VMEM((tm + 2 * SUBLANES, CONV_CH), F32)],
        compiler_params=_params(("arbitrary",)),
        name="in_projection",
    )(x2d, x2d, x2d, mods, mods, g_norm, w, conv_w, g_a, cos_t, sin_t)


def _head_attention(qm, k_loc, v_loc, add_loc, k_ctx, v_ctx, sink):
    s_ctx = _dot_nt(qm, k_ctx)
    m = jnp.max(s_ctx, axis=-1, keepdims=True)
    if k_loc is not None:
        s_loc = _dot_nt(qm, k_loc) + add_loc
        m = jnp.maximum(m, jnp.max(s_loc, axis=-1, keepdims=True))
    if sink is not None:
        m = jnp.maximum(m, sink)
    p_ctx = jnp.exp(s_ctx - m)
    den = jnp.sum(p_ctx, axis=-1, keepdims=True)
    o = _dot(p_ctx.astype(BF16), v_ctx)
    if k_loc is not None:
        p_loc = jnp.exp(s_loc - m)
        den = den + jnp.sum(p_loc, axis=-1, keepdims=True)
        o = o + _dot(p_loc.astype(BF16), v_loc)
    if sink is not None:
        den = den + jnp.exp(sink - m)
    return o / den


def _pair_attention(q, k_loc, v_loc, add_loc, k_ctx, v_ctx, sinks, lane_first):
    outs = []
    for half in range(2):
        keep = lane_first if half == 0 else jnp.logical_not(lane_first)
        qm = jnp.where(keep, q, jnp.zeros_like(q))
        outs.append(_head_attention(qm, k_loc, v_loc, None if add_loc is None else add_loc[half],
                                    k_ctx, v_ctx, None if sinks is None else sinks[half]))
    return jnp.where(lane_first, outs[0], outs[1])


def _lane_first(m):
    return lax.broadcasted_iota(jnp.int32, (m, LANES), 1) < HEAD_DIM


def _na_kernel(q_ref, k_ref, v_ref, kc_ref, vc_ref, tab_ref, g_ref, o_ref, y_s, bias_s, *, rows):
    gidx = pl.program_id(1)
    base = jnp.clip(gidx * NA_ROWS_PER_STEP - NA_WIN_R // 2, 0, rows - NA_KEY_ROWS)
    start = pl.multiple_of(base * GRID_W, GRID_W)
    nk = NA_KEY_ROWS * GRID_W
    mq = NA_ROWS_PER_STEP * GRID_W
    lane_first = _lane_first(mq)
    tile_first = _lane_first(GRID_W)

    def fill_bias(slot, head):
        for qr in range(NA_ROWS_PER_STEP):
            r = gidx * NA_ROWS_PER_STEP + qr
            rs = jnp.clip(r - NA_WIN_R // 2, 0, rows - NA_WIN_R)
            for m in range(NA_KEY_ROWS // 2):
                idx = []
                for kr in (base + 2 * m, base + 2 * m + 1):
                    ok = (kr >= rs) & (kr < rs + NA_WIN_R)
                    idx.append(jnp.where(ok, kr - r + NA_WIN_R - 1, NA_BIAS_MASKED))
                bias_s[slot, qr * GRID_W:(qr + 1) * GRID_W, m * LANES:(m + 1) * LANES] = jnp.where(
                    tile_first, tab_ref[head, idx[0]], tab_ref[head, idx[1]])

    for j in range(NA_DIM // LANES):
        cols = slice(j * LANES, (j + 1) * LANES)
        k_loc = k_ref[pl.ds(start, nk), cols]
        v_loc = v_ref[pl.ds(start, nk), cols]
        fill_bias(0, 2 * j)
        fill_bias(1, 2 * j + 1)
        y_s[:, cols] = _pair_attention(q_ref[:, cols], k_loc, v_loc, (bias_s[0], bias_s[1]), kc_ref[:, cols],
                                       vc_ref[:, cols], None, lane_first)
    o_ref[...] = _rms(y_s[...], g_ref[...]).astype(BF16)


def _na_bias_table(rpb):
    n_heads, n_dr, n_dc = rpb.shape
    pad = GRID_W - NA_WIN_C
    rp = jnp.pad(rpb.astype(F32), ((0, 0), (0, 0), (pad, pad)))
    toep = jnp.stack([rp[:, :, GRID_W - 1 - qc:2 * GRID_W - 1 - qc] for qc in range(GRID_W)], axis=2)
    qc = np.arange(GRID_W)
    cstart = np.clip(qc - NA_WIN_C // 2, 0, GRID_W - NA_WIN_C)
    col_ok = (qc[None, :] >= cstart[:, None]) & (qc[None, :] < cstart[:, None] + NA_WIN_C)
    tab = jnp.where(col_ok[None, None], toep, NEG_INF)
    masked = jnp.full((n_heads, NA_BIAS_MASKED + 1 - n_dr, GRID_W, GRID_W), NEG_INF, F32)
    tab = jnp.concatenate([tab, masked], axis=1)
    return jnp.tile(tab, (1, 1, 1, LANES // GRID_W))


def _na_attention(qn, kn, vn, kn_c, vn_c, tab, g_b, *, bsz, seq_len, ctx_len):
    rows = seq_len // GRID_W
    n_groups = rows // NA_ROWS_PER_STEP
    mq = NA_ROWS_PER_STEP * GRID_W
    kern = functools.partial(_na_kernel, rows=rows)
    return pl.pallas_call(
        kern,
        grid=(bsz, n_groups),
        in_specs=[pl.BlockSpec((mq, NA_DIM), lambda b, g: (b * n_groups + g, 0)),
                  pl.BlockSpec((seq_len, NA_DIM), lambda b, g: (b, 0)),
                  pl.BlockSpec((seq_len, NA_DIM), lambda b, g: (b, 0)),
                  pl.BlockSpec((ctx_len, NA_DIM), lambda b, g: (b, 0)),
                  pl.BlockSpec((ctx_len, NA_DIM), lambda b, g: (b, 0)),
                  _resident_spec(tab.shape),
                  _const_spec((1, NA_DIM))],
        out_specs=pl.BlockSpec((mq, NA_DIM), lambda b, g: (b * n_groups + g, 0)),
        out_shape=jax.ShapeDtypeStruct(qn.shape, BF16),
        scratch_shapes=[pltpu.VMEM((mq, NA_DIM), F32),
                        pltpu.VMEM((2, mq, NA_KEY_ROWS * GRID_W), F32)],
        compiler_params=_params(("arbitrary", "arbitrary")),
        name="na_attention",
    )(qn, kn, vn, kn_c, vn_c, tab, g_b)


def _sw_kernel(sink_ref, q_ref, k_ref, v_ref, kc_ref, vc_ref, g_ref, o_ref, y_s, *, seq_len):
    i = pl.program_id(1)
    s0 = i * SW_QBLOCK
    kstart = pl.multiple_of(jnp.clip(s0 - SW_RADIUS, 0, seq_len - SW_SPAN), SW_RADIUS)
    k_loc = k_ref[pl.ds(kstart, SW_SPAN), :]
    v_loc = v_ref[pl.ds(kstart, SW_SPAN), :]
    qpos = s0 + lax.broadcasted_iota(jnp.int32, (SW_QBLOCK, SW_SPAN), 0)
    kpos = kstart + lax.broadcasted_iota(jnp.int32, (SW_QBLOCK, SW_SPAN), 1)
    band = jnp.where(jnp.abs(kpos - qpos) <= SW_RADIUS, 0.0, NEG_INF).astype(F32)
    lane_first = _lane_first(SW_QBLOCK)
    pairs = SW_HEADS // SW_KV_HEADS
    for j in range(pairs):
        cols = slice(j * LANES, (j + 1) * LANES)
        sinks = (sink_ref[j], sink_ref[j + pairs])
        y_s[:, cols] = _pair_attention(q_ref[:, cols], k_loc, v_loc, (band, band), kc_ref[...], vc_ref[...],
                                       sinks, lane_first)
    o_ref[...] = _rms(y_s[...], g_ref[...]).astype(BF16)


def _sw_attention(sink, qs, ks, vs, ks_c, vs_c, g_c, *, bsz, seq_len, ctx_len):
    nblk = seq_len // SW_QBLOCK
    kern = functools.partial(_sw_kernel, seq_len=seq_len)
    return pl.pallas_call(
        kern,
        grid=(bsz, nblk),
        in_specs=[pl.BlockSpec(memory_space=pltpu.SMEM),
                  pl.BlockSpec((SW_QBLOCK, SW_DIM), lambda b, i: (b * nblk + i, 0)),
                  pl.BlockSpec((seq_len, SW_KV_DIM), lambda b, i: (b, 0)),
                  pl.BlockSpec((seq_len, SW_KV_DIM), lambda b, i: (b, 0)),
                  pl.BlockSpec((ctx_len, SW_KV_DIM), lambda b, i: (b, 0)),
                  pl.BlockSpec((ctx_len, SW_KV_DIM), lambda b, i: (b, 0)),
                  _const_spec((1, SW_DIM))],
        out_specs=pl.BlockSpec((SW_QBLOCK, SW_DIM), lambda b, i: (b * nblk + i, 0)),
        out_shape=jax.ShapeDtypeStruct(qs.shape, BF16),
        scratch_shapes=[pltpu.VMEM((SW_QBLOCK, SW_DIM), F32)],
        compiler_params=_params(("arbitrary", "arbitrary")),
        name="sw_attention",
    )(sink, qs, ks, vs, ks_c, vs_c, g_c)


def _ctx_attn_kernel(sink_ref, qn_ref, kn_ref, vn_ref, qs_ref, ks_ref, vs_ref, gb_ref, gc_ref,
                     yb_ref, yc_ref, y_s, *, ctx_len):
    lane_first = _lane_first(ctx_len)
    for j in range(NA_DIM // LANES):
        cols = slice(j * LANES, (j + 1) * LANES)
        y_s[:, cols] = _pair_attention(qn_ref[:, cols], None, None, None, kn_ref[:, cols], vn_ref[:, cols],
                                       None, lane_first)
    yb_ref[...] = _rms(y_s[...], gb_ref[...]).astype(BF16)
    pairs = SW_HEADS // SW_KV_HEADS
    for j in range(pairs):
        cols = slice(j * LANES, (j + 1) * LANES)
        sinks = (sink_ref[j], sink_ref[j + pairs])
        y_s[:, cols] = _pair_attention(qs_ref[:, cols], None, None, None, ks_ref[...], vs_ref[...],
                                       sinks, lane_first)
    yc_ref[...] = _rms(y_s[...], gc_ref[...]).astype(BF16)


def _ctx_attention(sink, qn, kn, vn, qs, ks, vs, g_b, g_c, *, bsz, ctx_len):
    kern = functools.partial(_ctx_attn_kernel, ctx_len=ctx_len)
    tok = lambda width: pl.BlockSpec((ctx_len, width), lambda b: (b, 0))
    return pl.pallas_call(
        kern,
        grid=(bsz,),
        in_specs=[pl.BlockSpec(memory_space=pltpu.SMEM),
                  tok(NA_DIM), tok(NA_DIM), tok(NA_DIM), tok(SW_DIM), tok(SW_KV_DIM), tok(SW_KV_DIM),
                  _const_spec((1, NA_DIM)), _const_spec((1, SW_DIM))],
        out_specs=[tok(NA_DIM), tok(SW_DIM)],
        out_shape=[jax.ShapeDtypeStruct(qn.shape, BF16), jax.ShapeDtypeStruct(qs.shape, BF16)],
        scratch_shapes=[pltpu.VMEM((ctx_len, NA_DIM), F32)],
        compiler_params=_params(("arbitrary",)),
        name="ctx_attention",
    )(sink, qn, kn, vn, qs, ks, vs, g_b, g_c)


def _mix_residual(ya_ref, yb_ref, yc_ref, x_ref, g1_ref, sh_ref, sc_ref, gn_ref, wa_ref, wb_ref, wc_ref):
    mix = _dot(ya_ref[...], wa_ref[...]) + _dot(yb_ref[...], wb_ref[...]) + _dot(yc_ref[...], wc_ref[...])
    xl = x_ref[...] + g1_ref[0] * mix
    h = _rms(xl, gn_ref[...]) * (1.0 + sc_ref[0]) + sh_ref[0]
    return xl, h


def _mix_specs(x2d, mods, group_of_tile, g_norm, ys, ws, tm):
    d = x2d.shape[1]
    tok = lambda width: pl.BlockSpec((tm, width), lambda i: (i, 0))
    mod = lambda j: pl.BlockSpec((1, 1, d), lambda i: (group_of_tile(i), 0, j))
    in_specs = ([tok(y.shape[1]) for y in ys] + [tok(d), mod(2), mod(3), mod(4), _const_spec((1, d))]
                + [_resident_spec(w.shape) for w in ws])
    return in_specs, [*ys, x2d, mods, mods, mods, g_norm, *ws]


def _outproj_router_kernel(*refs):
    *mix_refs, r_ref, xo_ref, h_ref, lg_ref = refs
    xl, h = _mix_residual(*mix_refs)
    xo_ref[...] = xl
    hi = h.astype(BF16)
    h_ref[...] = hi
    tm = h.shape[0]
    lo = (h - hi.astype(F32)).astype(BF16)
    r = _dot(jnp.concatenate([hi, lo], axis=0), r_ref[...])
    lg_ref[...] = (r[:tm, :LOGIT_LANES] + r[:tm, LOGIT_LANES:]) + (r[tm:, :LOGIT_LANES] + r[tm:, LOGIT_LANES:])


def _outproj_router(ys, x2d, mods, group_of_tile, g_norm, ws, router_w, *, tm):
    t, d = x2d.shape
    in_specs, args = _mix_specs(x2d, mods, group_of_tile, g_norm, ys, ws, tm)
    tok = lambda width: pl.BlockSpec((tm, width), lambda i: (i, 0))
    return pl.pallas_call(
        _outproj_router_kernel,
        grid=(t // tm,),
        in_specs=in_specs + [_resident_spec(router_w.shape)],
        out_specs=[tok(d), tok(d), tok(LOGIT_LANES)],
        out_shape=[jax.ShapeDtypeStruct((t, d), F32), jax.ShapeDtypeStruct((t, d), BF16),
                   jax.ShapeDtypeStruct((t, LOGIT_LANES), F32)],
        compiler_params=_params(("arbitrary",)),
        name="out_projection_router",
    )(*args, router_w)


def _outproj_ffn_kernel(*refs, chunks):
    *mix_refs, g2_ref, wg_ref, wu_ref, wd_ref, o_ref = refs
    xl, h = _mix_residual(*mix_refs)
    h = h.astype(BF16)
    acc = None
    for lo, hi in chunks:
        a = _silu(_dot(h, wg_ref[:, lo:hi])) * _dot(h, wu_ref[:, lo:hi])
        part = _dot(a.astype(BF16), wd_ref[lo:hi, :])
        acc = part if acc is None else acc + part
    o_ref[...] = xl + g2_ref[0] * acc


def _outproj_ffn(ys, x2d, mods, group_of_tile, g_norm, ws, wg, wu, wd, *, tm):
    t, d = x2d.shape
    d_ff = wg.shape[1]
    step = 1024
    chunks = tuple((lo, min(lo + step, d_ff)) for lo in range(0, d_ff, step))
    in_specs, args = _mix_specs(x2d, mods, group_of_tile, g_norm, ys, ws, tm)
    return pl.pallas_call(
        functools.partial(_outproj_ffn_kernel, chunks=chunks),
        grid=(t // tm,),
        in_specs=in_specs + [pl.BlockSpec((1, 1, d), lambda i: (group_of_tile(i), 0, 5)),
                             _resident_spec(wg.shape), _resident_spec(wu.shape), _resident_spec(wd.shape)],
        out_specs=pl.BlockSpec((tm, d), lambda i: (i, 0)),
        out_shape=jax.ShapeDtypeStruct((t, d), F32),
        compiler_params=_params(("arbitrary",)),
        name="out_projection_ffn",
    )(*args, mods, wg, wu, wd)


def _route_kernel(*refs):
    for j in range(MOE_STEP_CHUNKS):
        _route_chunk(j, *refs)


def _route_chunk(j, lg_ref, h_ref, hl_ref, pos_ref, gsel_ref, meta_ref):
    toks = slice(j * MOE_CHUNK, (j + 1) * MOE_CHUNK)
    logits = lg_ref[toks, :]
    lane = lax.broadcasted_iota(jnp.int32, logits.shape, 1)
    lg = jnp.where(lane < N_EXPERTS, logits, -jnp.inf)
    v1 = jnp.max(lg, axis=-1, keepdims=True)
    i1 = jnp.min(jnp.where(lg == v1, lane, LOGIT_LANES), axis=-1, keepdims=True)
    lg2 = jnp.where(lane == i1, -jnp.inf, lg)
    v2 = jnp.max(lg2, axis=-1, keepdims=True)
    i2 = jnp.min(jnp.where(lg2 == v2, lane, LOGIT_LANES), axis=-1, keepdims=True)
    e2 = jnp.exp(v2 - v1)
    den = 1.0 + e2
    g_first = 1.0 / den
    g_second = e2 / den
    sel1 = lane == i1
    sel2 = lane == i2
    onehot = jnp.where(sel1, 1.0, jnp.where(sel2, 1.0, 0.0))

    r_i = lax.broadcasted_iota(jnp.int32, (MOE_CHUNK, MOE_CHUNK), 0)
    c_i = lax.broadcasted_iota(jnp.int32, (MOE_CHUNK, MOE_CHUNK), 1)
    earlier = jnp.where(c_i < r_i, 1.0, 0.0).astype(BF16)
    within = _dot(earlier, onehot.astype(BF16))
    count = jnp.sum(onehot, axis=0, keepdims=True)
    n_gran = jnp.floor((count + (MOE_GRANULE - 1)) * (1.0 / MOE_GRANULE))
    u_r = lax.broadcasted_iota(jnp.int32, (LOGIT_LANES, LOGIT_LANES), 0)
    u_c = lax.broadcasted_iota(jnp.int32, (LOGIT_LANES, LOGIT_LANES), 1)
    before = jnp.where(u_r < u_c, 1.0, 0.0).astype(BF16)
    n_gran8 = jnp.broadcast_to(n_gran, (SUBLANES, LOGIT_LANES))
    off_gran = _dot(n_gran8.astype(BF16), before)[0:1]
    lpos = off_gran * MOE_GRANULE + within
    p_first = jnp.sum(jnp.where(sel1, lpos, 0.0), axis=-1, keepdims=True)
    p_second = jnp.sum(jnp.where(sel2, lpos, 0.0), axis=-1, keepdims=True)
    pos = jnp.where(lane == 0, p_first, jnp.where(lane == 1, p_second, -1.0))
    pos_ref[toks, :] = pos
    gsel_ref[toks, :] = jnp.where(lane == 0, g_first, jnp.where(lane == 1, g_second, 0.0))
    rows8 = lax.broadcasted_iota(jnp.int32, (SUBLANES, LOGIT_LANES), 0)
    meta_ref[j] = jnp.where(rows8 == 0, off_gran, jnp.where(rows8 == 1, n_gran, 0.0)).astype(jnp.int32)

    pos_t = pos.T
    row_id = lax.broadcasted_iota(jnp.int32, (MOE_LOCAL_ROWS, MOE_CHUNK), 0).astype(F32)
    pick = jnp.where(row_id == pos_t[0:1], 1.0, jnp.where(row_id == pos_t[1:2], 1.0, 0.0)).astype(BF16)
    hl_ref[j] = _dot(pick, h_ref[toks, :]).astype(BF16)


def _route(logits, h2):
    t, d = h2.shape
    n_chunks = t // MOE_CHUNK
    per_step = MOE_STEP_CHUNKS
    tok = lambda width: pl.BlockSpec((per_step * MOE_CHUNK, width), lambda i: (i, 0))
    return pl.pallas_call(
        _route_kernel,
        grid=(n_chunks // per_step,),
        in_specs=[tok(LOGIT_LANES), tok(d)],
        out_specs=[pl.BlockSpec((per_step, MOE_LOCAL_ROWS, d), lambda i: (i, 0, 0)),
                   tok(LOGIT_LANES), tok(LOGIT_LANES),
                   pl.BlockSpec((per_step, SUBLANES, LOGIT_LANES), lambda i: (i, 0, 0))],
        out_shape=[jax.ShapeDtypeStruct((n_chunks, MOE_LOCAL_ROWS, d), BF16),
                   jax.ShapeDtypeStruct((t, LOGIT_LANES), F32),
                   jax.ShapeDtypeStruct((t, LOGIT_LANES), F32),
                   jax.ShapeDtypeStruct((n_chunks, SUBLANES, LOGIT_LANES), jnp.int32)],
        compiler_params=_params(("arbitrary",)),
        name="moe_route",
    )(logits, h2)


def _experts_kernel(meta_ref, hl_hbm, wg_ref, wu_ref, wd_ref, yl_hbm, he_s, ye_s, yb_s, zero_s,
                    gather_sem, store_sem, fill_sem, *, n_chunks):
    ws = pl.program_id(0)
    e = pl.program_id(1)
    f = pl.program_id(2)
    n_e = pl.num_programs(1)
    n_f = pl.num_programs(2)
    step = ws * n_e + e
    last_step = pl.num_programs(0) * n_e - 1

    def meta(ws_, e_, c, k):
        return meta_ref[((ws_ * n_chunks + c) * 2 + k) * N_EXPERTS + e_]

    def rows_of(granules):
        return pl.multiple_of(granules * MOE_GRANULE, MOE_GRANULE)

    def for_each_run(ws_, e_, fn):
        dst = 0
        for c in range(n_chunks):
            src, n = meta(ws_, e_, c, 0), meta(ws_, e_, c, 1)

            @pl.when(n > 0)
            def _(c=c, src=src, n=n, dst=dst):
                fn(ws_ * n_chunks + c, rows_of(src), rows_of(dst), rows_of(n))
            dst = dst + n
        return dst * MOE_GRANULE

    def gather_copy(slot):
        return lambda chunk, src, dst, rows: pltpu.make_async_copy(
            hl_hbm.at[chunk, pl.ds(src, rows)], he_s.at[slot, pl.ds(dst, rows)], gather_sem.at[slot])

    def store_copy(chunk, src, dst, rows):
        return pltpu.make_async_copy(yb_s.at[pl.ds(dst, rows)], yl_hbm.at[chunk, pl.ds(src, rows)], store_sem)

    def fill_tails(start):
        for c in range(n_chunks):
            used = meta(ws, N_EXPERTS - 1, c, 0) + meta(ws, N_EXPERTS - 1, c, 1)
            tail = MOE_LOCAL_ROWS // MOE_GRANULE - used

            @pl.when(tail > 0)
            def _(c=c, used=used, tail=tail):
                cp = pltpu.make_async_copy(zero_s.at[pl.ds(0, rows_of(tail))],
                                           yl_hbm.at[ws * n_chunks + c, pl.ds(rows_of(used), rows_of(tail))],
                                           fill_sem)
                cp.start() if start else cp.wait()

    slot = step % 2

    @pl.when(f == 0)
    def _():
        @pl.when(step == 0)
        def _():
            he_s[...] = jnp.zeros_like(he_s)
            zero_s[...] = jnp.zeros_like(zero_s)
            for_each_run(ws, e, lambda *run: gather_copy(slot)(*run).start())

        @pl.when(e == 0)
        def _():
            fill_tails(True)
            fill_tails(False)

        for_each_run(ws, e, lambda *run: gather_copy(slot)(*run).wait())

        @pl.when(step < last_step)
        def _():
            nxt = step + 1
            for_each_run(nxt // n_e, nxt % n_e, lambda *run: gather_copy(1 - slot)(*run).start())

    def m_tile(start, size):
        rows = pl.ds(pl.multiple_of(start, MOE_M_MIN), size)
        hrow = he_s[slot, rows, :]
        a = _silu(_dot(hrow, wg_ref[0])) * _dot(hrow, wu_ref[0])
        part = _dot(a.astype(BF16), wd_ref[0])

        @pl.when(f == 0)
        def _():
            ye_s[rows, :] = part

        @pl.when(f > 0)
        def _():
            ye_s[rows, :] += part

    n_rows = sum(meta(ws, e, c, 1) for c in range(n_chunks)) * MOE_GRANULE
    n_full = n_rows // MOE_M_TILE
    rem = n_rows - n_full * MOE_M_TILE
    n_min = (n_rows + MOE_M_MIN - 1) // MOE_M_MIN

    def full_tile(m, carry):
        m_tile(m * MOE_M_TILE, MOE_M_TILE)
        return carry
    lax.fori_loop(0, n_full, full_tile, 0)
    size = MOE_M_TILE
    while size >= MOE_M_MIN:
        lo = 0 if size == MOE_M_MIN else size // 2

        @pl.when((rem > lo) & (rem <= size))
        def _(size=size):
            m_tile(n_full * MOE_M_TILE, size)
        size //= 2

    @pl.when(f == n_f - 1)
    def _():
        @pl.when(step > 0)
        def _():
            prev = step - 1
            for_each_run(prev // n_e, prev % n_e, lambda *run: store_copy(*run).wait())

        def cast_tile(m, carry):
            rows = pl.ds(pl.multiple_of(m * MOE_M_MIN, MOE_M_MIN), MOE_M_MIN)
            yb_s[rows, :] = ye_s[rows, :].astype(BF16)
            return carry
        lax.fori_loop(0, n_min, cast_tile, 0)
        for_each_run(ws, e, lambda *run: store_copy(*run).start())

        @pl.when(step == last_step)
        def _():
            for_each_run(ws, e, lambda *run: store_copy(*run).wait())


def _experts(meta, hl, wg, wu, wd, *, n_chunks):
    total_chunks, local_rows, d = hl.shape
    n_exp, _, d_ff = wg.shape
    tf = MOE_FF_TILE
    max_rows = n_chunks * MOE_CHUNK
    grid_spec = pltpu.PrefetchScalarGridSpec(
        num_scalar_prefetch=1,
        grid=(total_chunks // n_chunks, n_exp, d_ff // tf),
        in_specs=[pl.BlockSpec(memory_space=pl.ANY),
                  pl.BlockSpec((1, d, tf), lambda i, e, f, m: (e, 0, f)),
                  pl.BlockSpec((1, d, tf), lambda i, e, f, m: (e, 0, f)),
                  pl.BlockSpec((1, tf, d), lambda i, e, f, m: (e, f, 0))],
        out_specs=pl.BlockSpec(memory_space=pl.ANY),
        scratch_shapes=[pltpu.VMEM((2, max_rows, d), BF16), pltpu.VMEM((max_rows, d), F32),
                        pltpu.VMEM((max_rows, d), BF16), pltpu.VMEM((local_rows, d), BF16),
                        pltpu.SemaphoreType.DMA((2,)), pltpu.SemaphoreType.DMA(()), pltpu.SemaphoreType.DMA(())])
    return pl.pallas_call(
        functools.partial(_experts_kernel, n_chunks=n_chunks),
        grid_spec=grid_spec,
        out_shape=jax.ShapeDtypeStruct(hl.shape, BF16),
        compiler_params=_params(("arbitrary", "arbitrary", "arbitrary")),
        name="moe_experts",
    )(meta, hl, wg, wu, wd)


def _scatter_kernel(yl_ref, pos_ref, gsel_ref, x_ref, g2_ref, gf_ref, o_ref):
    row_id = lax.broadcasted_iota(jnp.int32, (MOE_CHUNK, MOE_LOCAL_ROWS), 1).astype(F32)
    for j in range(MOE_STEP_CHUNKS):
        toks = slice(j * MOE_CHUNK, (j + 1) * MOE_CHUNK)
        pos = pos_ref[toks, :]
        gsel = gsel_ref[toks, :]
        yl = yl_ref[j]
        y = None
        for k in range(2):
            pick = jnp.where(pos[:, k:k + 1] == row_id, 1.0, 0.0).astype(BF16)
            term = gsel[:, k:k + 1] * _dot(pick, yl)
            y = term if y is None else y + term
        o_ref[toks, :] = _rms(x_ref[toks, :] + g2_ref[0] * y, gf_ref[...])


def _scatter_final(yl, pos, gsel, x2d, mods, group_of_step, g_final):
    t, d = x2d.shape
    per_step = MOE_STEP_CHUNKS
    tok = lambda width: pl.BlockSpec((per_step * MOE_CHUNK, width), lambda i: (i, 0))
    return pl.pallas_call(
        _scatter_kernel,
        grid=(t // (per_step * MOE_CHUNK),),
        in_specs=[pl.BlockSpec((per_step, MOE_LOCAL_ROWS, d), lambda i: (i, 0, 0)),
                  tok(LOGIT_LANES), tok(LOGIT_LANES), tok(d),
                  pl.BlockSpec((1, 1, d), lambda i: (group_of_step(i), 0, 5)), _const_spec((1, d))],
        out_specs=tok(d),
        out_shape=jax.ShapeDtypeStruct((t, d), F32),
        compiler_params=_params(("arbitrary",)),
        name="moe_scatter_final",
    )(yl, pos, gsel, x2d, mods, g_final)


def _rope_tables(n):
    t = jnp.arange(n)
    row = (t // GRID_W).astype(F32)
    col = (t % GRID_W).astype(F32)
    n_freq = HEAD_DIM // 4
    inv = ROPE_THETA ** (-jnp.arange(n_freq, dtype=F32) / n_freq)
    ang = jnp.stack([row[:, None] * inv, col[:, None] * inv], axis=1)
    cos = jnp.cos(ang)[:, :, None, :]
    sin = jnp.sin(ang)[:, :, None, :]
    cos_f = jnp.broadcast_to(cos, (n, 2, 2, n_freq)).reshape(n, HEAD_DIM)
    sin_s = jnp.concatenate([-sin, sin], axis=2).reshape(n, HEAD_DIM)
    reps = LANES // HEAD_DIM
    return jnp.tile(cos_f, (1, reps)), jnp.tile(sin_s, (1, reps))


def _pair_perm():
    pairs = SW_HEADS // SW_KV_HEADS
    order = [h for j in range(pairs) for h in (j, j + pairs)]
    return np.concatenate([np.arange(h * HEAD_DIM, (h + 1) * HEAD_DIM) for h in order])


def kernel(x, c, ctx, c_ctx, w_ada, b_ada, g_norm1, g_norm2, w_in, conv_w, na_rpb, sw_sink, g_mix, w_out,
           ffn_w_gate, ffn_w_up, ffn_w_down, w_router, moe_w_gate, moe_w_up, moe_w_down, g_final):
    bsz, n, d = x.shape
    ctx_len = ctx.shape[1]
    depth = w_in.shape[0]
    assert w_in.shape[2] == D_IN and n % GRID_W == 0
    assert depth == 2, "layer 0 dense on both streams, layer 1 routed experts on the latent stream only"
    tm = 512
    tm_ctx = ctx_len
    set_chunks = MOE_SET_CHUNKS
    assert n % tm == 0 and (n // GRID_W) % NA_ROWS_PER_STEP == 0 and n % SW_QBLOCK == 0
    assert n % (MOE_CHUNK * MOE_STEP_CHUNKS) == 0 and (bsz * n) % (MOE_CHUNK * set_chunks) == 0
    assert (MOE_CHUNK * set_chunks) % MOE_M_TILE == 0

    cond_rows = 2 * SUBLANES
    cond = jnp.zeros((cond_rows, d), F32).at[:bsz].set(c).at[bsz].set(c_ctx)
    mods_all = _ada(cond, w_ada, b_ada).reshape(depth, cond_rows, 1, 6 * d)

    lat_group = lambda i: i // (n // tm)
    ctx_group = lambda i: bsz
    cos_t, sin_t = _rope_tables(n)
    perm = _pair_perm()

    xl = x.reshape(bsz * n, d)
    xc = ctx.reshape(bsz * ctx_len, d)
    out = None
    for l in range(depth):
        last = l == depth - 1
        mods = mods_all[l]
        w_l = w_in[l]
        w_l = jnp.concatenate([w_l[:, :OFF_SQ], w_l[:, OFF_SQ:OFF_NK][:, perm], w_l[:, OFF_NK:]], axis=1).astype(BF16)
        g_a = g_mix[l, :CONV_CH].reshape(1, CONV_CH)
        g_b = g_mix[l, CONV_CH:CONV_CH + NA_DIM].reshape(1, NA_DIM)
        g_c = g_mix[l, CONV_CH + NA_DIM:][perm].reshape(1, SW_DIM)
        wo = w_out[l]
        wa = wo[:CONV_CH].astype(BF16)
        wb = wo[CONV_CH:CONV_CH + NA_DIM].astype(BF16)
        wc = wo[CONV_CH + NA_DIM:][perm].astype(BF16)
        gn1 = g_norm1[l].reshape(1, d)
        gn2 = g_norm2[l].reshape(1, d)
        sink = sw_sink[l].astype(F32)
        tab = _na_bias_table(na_rpb[l])

        ya, qn, qs, kn, vn, ks, vs = _inproj(xl, mods, lat_group, gn1, w_l, conv_w[l], g_a, cos_t, sin_t,
                                             seq_len=n, tm=tm, rope=True)
        cya, cqn, cqs, ckn, cvn, cks, cvs = _inproj(xc, mods, ctx_group, gn1, w_l, conv_w[l], g_a, cos_t, sin_t,
                                                    seq_len=ctx_len, tm=tm_ctx, rope=False)
        yb = _na_attention(qn, kn, vn, ckn, cvn, tab, g_b, bsz=bsz, seq_len=n, ctx_len=ctx_len)
        yc = _sw_attention(sink, qs, ks, vs, cks, cvs, g_c, bsz=bsz, seq_len=n, ctx_len=ctx_len)

        if l % 2 == 0:
            i = l // 2
            wg = ffn_w_gate[i].astype(BF16)
            wu = ffn_w_up[i].astype(BF16)
            wd = ffn_w_down[i].astype(BF16)
            xl = _outproj_ffn((ya, yb, yc), xl, mods, lat_group, gn2, (wa, wb, wc), wg, wu, wd, tm=tm)
            if not last:
                cyb, cyc = _ctx_attention(sink, cqn, ckn, cvn, cqs, cks, cvs, g_b, g_c, bsz=bsz, ctx_len=ctx_len)
                xc = _outproj_ffn((cya, cyb, cyc), xc, mods, ctx_group, gn2, (wa, wb, wc), wg, wu, wd,
                                  tm=tm_ctx)
        else:
            i = l // 2
            wr = jnp.zeros((d, LOGIT_LANES), F32).at[:, :N_EXPERTS].set(w_router[i])
            wr_hi = wr.astype(BF16)
            wr_lo = (wr - wr_hi.astype(F32)).astype(BF16)
            wr2 = jnp.concatenate([wr_hi, wr_lo], axis=1)
            wg = moe_w_gate[i].astype(BF16)
            wu = moe_w_up[i].astype(BF16)
            wd = moe_w_down[i].astype(BF16)
            xl, h2, logits = _outproj_router((ya, yb, yc), xl, mods, lat_group, gn2, (wa, wb, wc), wr2, tm=tm)
            hl, pos, gsel, meta = _route(logits, h2)
            meta = meta[:, :2, :N_EXPERTS].reshape(-1)
            yl = _experts(meta, hl, wg, wu, wd, n_chunks=set_chunks)
            step_tokens = MOE_CHUNK * MOE_STEP_CHUNKS
            out = _scatter_final(yl, pos, gsel, xl, mods, lambda i: i // (n // step_tokens), g_final.reshape(1, d))
    return out.reshape(bsz, n, d)
```

```python
import functools

import numpy as np
import jax
import jax.numpy as jnp
from jax import lax
from jax.experimental import pallas as pl
from jax.experimental.pallas import tpu as pltpu

F32 = jnp.float32
BF16 = jnp.bfloat16

GRID_W = 64
HEAD_DIM = 64
CONV_CH = 256
NA_HEADS = 6
NA_DIM = NA_HEADS * HEAD_DIM
SW_HEADS = 6
SW_KV_HEADS = 2
SW_DIM = SW_HEADS * HEAD_DIM
SW_KV_DIM = SW_KV_HEADS * HEAD_DIM
NA_WIN_R = 8
NA_WIN_C = 16
SW_RADIUS = 128
ROPE_THETA = 10000.0
N_EXPERTS = 8
EPS = 1e-6
NEG_INF = -1e30
OFF_H, OFF_B, OFF_C = 0, CONV_CH, 2 * CONV_CH
OFF_NQ = 3 * CONV_CH
OFF_SQ = OFF_NQ + NA_DIM
OFF_NK = OFF_SQ + SW_DIM
OFF_NV = OFF_NK + NA_DIM
OFF_SK = OFF_NV + NA_DIM
OFF_SV = OFF_SK + SW_KV_DIM
D_IN = OFF_SV + SW_KV_DIM
LOG2E = 1.4426950408889634
Q_SCALE = HEAD_DIM ** -0.5 * LOG2E

LANES = 128
SUBLANES = 8
VMEM_BYTES_V7X = 64 * 1024 * 1024
VMEM_LIMIT = VMEM_BYTES_V7X * 7 // 8

NA_ROWS_PER_STEP = 4
NA_KEY_ROWS = NA_ROWS_PER_STEP + NA_WIN_R
NA_BIAS_MASKED = 2 * NA_WIN_R - 1
MOE_CHUNK = 256
MOE_GRANULE = 2 * SUBLANES
MOE_LOCAL_ROWS = 640
MOE_SET_CHUNKS = 16
MOE_STEP_CHUNKS = 4
MOE_M_TILE = 512
MOE_M_MIN = 128
MOE_FF_TILE = 512
SW_QBLOCK = 256
SW_SPAN = SW_QBLOCK + 2 * SW_RADIUS
LOGIT_LANES = LANES


def _params(semantics):
    return pltpu.CompilerParams(dimension_semantics=semantics, vmem_limit_bytes=VMEM_LIMIT)


def _const_spec(shape):
    nd = len(shape)
    return pl.BlockSpec(shape, lambda *_: (0,) * nd)


def _resident_spec(shape):
    nd = len(shape)
    return pl.BlockSpec(shape, lambda *_: (0,) * nd, pipeline_mode=pl.Buffered(1))


def _rms(x, g):
    y = x * lax.rsqrt(jnp.mean(x * x, axis=-1, keepdims=True) + EPS)
    return y * g


def _silu(x):
    return x / (1.0 + jnp.exp(-x))


def _dot(a, b):
    return jnp.dot(a, b, preferred_element_type=F32)


def _dot_nt(a, b):
    return lax.dot_general(a, b, (((1,), (1,)), ((), ())), preferred_element_type=F32)


def _ada_kernel(c_ref, w_ref, b_ref, o_ref):
    a = _silu(c_ref[...])
    o_ref[0] = jnp.dot(a, w_ref[0], preferred_element_type=F32,
                       precision=lax.Precision.HIGHEST) + b_ref[0]


def _ada(cond, w_ada, b_ada):
    depth, d, n_out = w_ada.shape
    rows = cond.shape[0]
    tn = n_out // 6
    return pl.pallas_call(
        _ada_kernel,
        grid=(depth, n_out // tn),
        in_specs=[pl.BlockSpec((rows, d), lambda l, j: (0, 0)),
                  pl.BlockSpec((1, d, tn), lambda l, j: (l, 0, j)),
                  pl.BlockSpec((1, 1, tn), lambda l, j: (l, 0, j))],
        out_specs=pl.BlockSpec((1, rows, tn), lambda l, j: (l, 0, j)),
        out_shape=jax.ShapeDtypeStruct((depth, rows, n_out), F32),
        compiler_params=_params(("arbitrary", "arbitrary")),
        name="ada_modulation",
    )(cond, w_ada, b_ada.reshape(depth, 1, n_out))


def _rope(x, cos, sin, lane_lo):
    rot = jnp.where(lane_lo, pltpu.roll(x, LANES - 16, 1), pltpu.roll(x, 16, 1))
    return x * cos + rot * sin


def _inproj_kernel(x_ref, xp_ref, xn_ref, sh_ref, sc_ref, gn_ref, w_ref, cw_ref, ga_ref, cos_ref, sin_ref,
                   ya_ref, qn_ref, qs_ref, kn_ref, vn_ref, ks_ref, vs_ref, u_s, *, tm, tiles_per_seq, rope):
    i = pl.program_id(0)
    g = gn_ref[...]
    sh = sh_ref[0]
    sc = sc_ref[0]

    def normmod(x):
        return _rms(x, g) * (1.0 + sc) + sh

    h = normmod(x_ref[...]).astype(BF16)
    p = _dot(h, w_ref[...])

    xh = jnp.concatenate([xp_ref[...], xn_ref[...]], axis=0)
    ph = _dot(normmod(xh).astype(BF16), w_ref[:, :OFF_NQ])
    uh = ph[:, OFF_C:OFF_C + CONV_CH] * ph[:, OFF_H:OFF_H + CONV_CH]
    pos = i % tiles_per_seq
    u = p[:, OFF_C:OFF_C + CONV_CH] * p[:, OFF_H:OFF_H + CONV_CH]
    u_s[0:SUBLANES] = jnp.where(pos == 0, 0.0, uh[0:SUBLANES])
    u_s[SUBLANES:SUBLANES + tm] = u
    u_s[SUBLANES + tm:2 * SUBLANES + tm] = jnp.where(pos == tiles_per_seq - 1, 0.0, uh[SUBLANES:])
    cw = cw_ref[...]
    conv = (u_s[SUBLANES - 1:SUBLANES - 1 + tm] * cw[0:1] + u * cw[1:2]
            + u_s[SUBLANES + 1:SUBLANES + 1 + tm] * cw[2:3])
    ya = p[:, OFF_B:OFF_B + CONV_CH] * conv
    ya_ref[...] = _rms(ya, ga_ref[...]).astype(BF16)

    qn_ref[...] = (p[:, OFF_NQ:OFF_NQ + NA_DIM] * Q_SCALE).astype(BF16)
    kn_ref[...] = p[:, OFF_NK:OFF_NK + NA_DIM].astype(BF16)
    vn_ref[...] = p[:, OFF_NV:OFF_NV + NA_DIM].astype(BF16)
    vs_ref[...] = p[:, OFF_SV:OFF_SV + SW_KV_DIM].astype(BF16)
    if rope:
        cos = cos_ref[...]
        sin = sin_ref[...]
        lane_lo = (lax.broadcasted_iota(jnp.int32, (tm, LANES), 1) % 32) < 16
        for j in range(SW_DIM // LANES):
            q = p[:, OFF_SQ + j * LANES:OFF_SQ + (j + 1) * LANES]
            qs_ref[:, j * LANES:(j + 1) * LANES] = (_rope(q, cos, sin, lane_lo) * Q_SCALE).astype(BF16)
        ks_ref[...] = _rope(p[:, OFF_SK:OFF_SK + SW_KV_DIM], cos, sin, lane_lo).astype(BF16)
    else:
        qs_ref[...] = (p[:, OFF_SQ:OFF_SQ + SW_DIM] * Q_SCALE).astype(BF16)
        ks_ref[...] = p[:, OFF_SK:OFF_SK + SW_KV_DIM].astype(BF16)


def _inproj(x2d, mods, group_of_tile, g_norm, w, conv_w, g_a, cos_t, sin_t, *, seq_len, tm, rope):
    t, d = x2d.shape
    tiles_per_seq = seq_len // tm
    blk8 = tm // SUBLANES
    last8 = t // SUBLANES - 1
    kern = functools.partial(_inproj_kernel, tm=tm, tiles_per_seq=tiles_per_seq, rope=rope)
    tok = lambda width: pl.BlockSpec((tm, width), lambda i: (i, 0))
    out_widths = (CONV_CH, NA_DIM, SW_DIM, NA_DIM, NA_DIM, SW_KV_DIM, SW_KV_DIM)
    return pl.pallas_call(
        kern,
        grid=(t // tm,),
        in_specs=[tok(d),
                  pl.BlockSpec((SUBLANES, d), lambda i: (jnp.maximum(i * blk8 - 1, 0), 0)),
                  pl.BlockSpec((SUBLANES, d), lambda i: (jnp.minimum((i + 1) * blk8, last8), 0)),
                  pl.BlockSpec((1, 1, d), lambda i: (group_of_tile(i), 0, 0)),
                  pl.BlockSpec((1, 1, d), lambda i: (group_of_tile(i), 0, 1)),
                  _const_spec((1, d)),
                  _resident_spec(w.shape),
                  _const_spec(conv_w.shape),
                  _const_spec((1, CONV_CH)),
                  pl.BlockSpec((tm, LANES), lambda i: (i % tiles_per_seq, 0)),
                  pl.BlockSpec((tm, LANES), lambda i: (i % tiles_per_seq, 0))],
        out_specs=[tok(wd) for wd in out_widths],
        out_shape=[jax.ShapeDtypeStruct((t, wd), BF16) for wd in out_widths],
        scratch_shapes=[pltpu.VMEM((tm + 2 * SUBLANES, CONV_CH), F32)],
        compiler_params=_params(("arbitrary",)),
        name="in_projection",
    )(x2d, x2d, x2d, mods, mods, g_norm, w, conv_w, g_a, cos_t, sin_t)


def _masked_attention(qm, k_loc, v_loc, add_loc, k_ctx, v_ctx, sink):
    s_ctx = _dot_nt(qm, k_ctx)
    m = jnp.max(s_ctx, axis=-1, keepdims=True)
    if k_loc is not None:
        s_loc = _dot_nt(qm, k_loc) + add_loc
        m = jnp.maximum(m, jnp.max(s_loc, axis=-1, keepdims=True))
    if sink is not None:
        m = jnp.maximum(m, sink)
    p_ctx = jnp.exp2(s_ctx - m)
    den = jnp.sum(p_ctx, axis=-1, keepdims=True)
    o = _dot(p_ctx.astype(BF16), v_ctx)
    if k_loc is not None:
        p_loc = jnp.exp2(s_loc - m)
        den = den + jnp.sum(p_loc, axis=-1, keepdims=True)
        o = o + _dot(p_loc.astype(BF16), v_loc)
    if sink is not None:
        den = den + jnp.exp2(sink - m)
    return o / den


def _pair_attention(q, k_loc, v_loc, add_loc, k_ctx, v_ctx, sinks, lane_first):
    m = q.shape[0]
    zero = jnp.zeros_like(q)
    q2 = jnp.concatenate([jnp.where(lane_first, q, zero), jnp.where(lane_first, zero, q)], axis=0)
    sink2 = None
    if sinks is not None:
        second = lax.broadcasted_iota(jnp.int32, (2 * m, 1), 0) >= m
        sink2 = jnp.where(second, sinks[1], sinks[0])
    o = _masked_attention(q2, k_loc, v_loc, add_loc, k_ctx, v_ctx, sink2)
    return jnp.where(lane_first, o[:m], o[m:])


def _lane_first(m):
    return lax.broadcasted_iota(jnp.int32, (m, LANES), 1) < HEAD_DIM


def _na_kernel(q_ref, k_ref, v_ref, kc_ref, vc_ref, tab_ref, g_ref, o_ref, y_s, bias_s, *, rows):
    gidx = pl.program_id(1)
    base = jnp.clip(gidx * NA_ROWS_PER_STEP - NA_WIN_R // 2, 0, rows - NA_KEY_ROWS)
    start = pl.multiple_of(base * GRID_W, GRID_W)
    nk = NA_KEY_ROWS * GRID_W
    mq = NA_ROWS_PER_STEP * GRID_W
    lane_first = _lane_first(mq)
    tile_first = _lane_first(GRID_W)

    def fill_bias(head):
        for qr in range(NA_ROWS_PER_STEP):
            r = gidx * NA_ROWS_PER_STEP + qr
            rs = jnp.clip(r - NA_WIN_R // 2, 0, rows - NA_WIN_R)
            row0 = (head % 2) * mq + qr * GRID_W
            for m in range(NA_KEY_ROWS // 2):
                idx = []
                for kr in (base + 2 * m, base + 2 * m + 1):
                    ok = (kr >= rs) & (kr < rs + NA_WIN_R)
                    idx.append(jnp.where(ok, kr - r + NA_WIN_R - 1, NA_BIAS_MASKED))
                bias_s[head // 2, row0:row0 + GRID_W, m * LANES:(m + 1) * LANES] = jnp.where(
                    tile_first, tab_ref[head, idx[0]], tab_ref[head, idx[1]])

    @pl.when((gidx <= 1) | (gidx == pl.num_programs(1) - 1))
    def _():
        for head in range(NA_HEADS):
            fill_bias(head)

    for j in range(NA_DIM // LANES):
        cols = slice(j * LANES, (j + 1) * LANES)
        k_loc = k_ref[pl.ds(start, nk), cols]
        v_loc = v_ref[pl.ds(start, nk), cols]
        y_s[:, cols] = _pair_attention(q_ref[:, cols], k_loc, v_loc, bias_s[j], kc_ref[:, cols],
                                       vc_ref[:, cols], None, lane_first)
    o_ref[...] = _rms(y_s[...], g_ref[...]).astype(BF16)


def _na_bias_table(rpb):
    n_heads, n_dr, n_dc = rpb.shape
    pad = GRID_W - NA_WIN_C
    rp = jnp.pad(rpb.astype(F32), ((0, 0), (0, 0), (pad, pad)))
    toep = jnp.stack([rp[:, :, GRID_W - 1 - qc:2 * GRID_W - 1 - qc] for qc in range(GRID_W)], axis=2)
    qc = np.arange(GRID_W)
    cstart = np.clip(qc - NA_WIN_C // 2, 0, GRID_W - NA_WIN_C)
    col_ok = (qc[None, :] >= cstart[:, None]) & (qc[None, :] < cstart[:, None] + NA_WIN_C)
    tab = jnp.where(col_ok[None, None], toep * LOG2E, NEG_INF)
    masked = jnp.full((n_heads, NA_BIAS_MASKED + 1 - n_dr, GRID_W, GRID_W), NEG_INF, F32)
    tab = jnp.concatenate([tab, masked], axis=1)
    return jnp.tile(tab, (1, 1, 1, LANES // GRID_W))


def _na_attention(qn, kn, vn, kn_c, vn_c, tab, g_b, *, bsz, seq_len, ctx_len):
    rows = seq_len // GRID_W
    n_groups = rows // NA_ROWS_PER_STEP
    mq = NA_ROWS_PER_STEP * GRID_W
    kern = functools.partial(_na_kernel, rows=rows)
    return pl.pallas_call(
        kern,
        grid=(bsz, n_groups),
        in_specs=[pl.BlockSpec((mq, NA_DIM), lambda b, g: (b * n_groups + g, 0)),
                  pl.BlockSpec((seq_len, NA_DIM), lambda b, g: (b, 0)),
                  pl.BlockSpec((seq_len, NA_DIM), lambda b, g: (b, 0)),
                  pl.BlockSpec((ctx_len, NA_DIM), lambda b, g: (b, 0)),
                  pl.BlockSpec((ctx_len, NA_DIM), lambda b, g: (b, 0)),
                  _resident_spec(tab.shape),
                  _const_spec((1, NA_DIM))],
        out_specs=pl.BlockSpec((mq, NA_DIM), lambda b, g: (b * n_groups + g, 0)),
        out_shape=jax.ShapeDtypeStruct(qn.shape, BF16),
        scratch_shapes=[pltpu.VMEM((mq, NA_DIM), F32),
                        pltpu.VMEM((NA_HEADS // 2, 2 * mq, NA_KEY_ROWS * GRID_W), F32)],
        compiler_params=_params(("arbitrary", "arbitrary")),
        name="na_attention",
    )(qn, kn, vn, kn_c, vn_c, tab, g_b)


def _sw_kernel(sink_ref, q_ref, k_ref, v_ref, kc_ref, vc_ref, g_ref, o_ref, y_s, *, seq_len):
    i = pl.program_id(1)
    s0 = i * SW_QBLOCK
    kstart = pl.multiple_of(jnp.clip(s0 - SW_RADIUS, 0, seq_len - SW_SPAN), SW_RADIUS)
    k_loc = k_ref[pl.ds(kstart, SW_SPAN), :]
    v_loc = v_ref[pl.ds(kstart, SW_SPAN), :]
    qpos = s0 + lax.broadcasted_iota(jnp.int32, (SW_QBLOCK, SW_SPAN), 0)
    kpos = kstart + lax.broadcasted_iota(jnp.int32, (SW_QBLOCK, SW_SPAN), 1)
    band = jnp.where(jnp.abs(kpos - qpos) <= SW_RADIUS, 0.0, NEG_INF).astype(F32)
    band2 = jnp.concatenate([band, band], axis=0)
    lane_first = _lane_first(SW_QBLOCK)
    pairs = SW_HEADS // SW_KV_HEADS
    for j in range(pairs):
        cols = slice(j * LANES, (j + 1) * LANES)
        sinks = (sink_ref[j], sink_ref[j + pairs])
        y_s[:, cols] = _pair_attention(q_ref[:, cols], k_loc, v_loc, band2, kc_ref[...], vc_ref[...],
                                       sinks, lane_first)
    o_ref[...] = _rms(y_s[...], g_ref[...]).astype(BF16)


def _sw_attention(sink, qs, ks, vs, ks_c, vs_c, g_c, *, bsz, seq_len, ctx_len):
    nblk = seq_len // SW_QBLOCK
    kern = functools.partial(_sw_kernel, seq_len=seq_len)
    return pl.pallas_call(
        kern,
        grid=(bsz, nblk),
        in_specs=[pl.BlockSpec(memory_space=pltpu.SMEM),
                  pl.BlockSpec((SW_QBLOCK, SW_DIM), lambda b, i: (b * nblk + i, 0)),
                  pl.BlockSpec((seq_len, SW_KV_DIM), lambda b, i: (b, 0)),
                  pl.BlockSpec((seq_len, SW_KV_DIM), lambda b, i: (b, 0)),
                  pl.BlockSpec((ctx_len, SW_KV_DIM), lambda b, i: (b, 0)),
                  pl.BlockSpec((ctx_len, SW_KV_DIM), lambda b, i: (b, 0)),
                  _const_spec((1, SW_DIM))],
        out_specs=pl.BlockSpec((SW_QBLOCK, SW_DIM), lambda b, i: (b * nblk + i, 0)),
        out_shape=jax.ShapeDtypeStruct(qs.shape, BF16),
        scratch_shapes=[pltpu.VMEM((SW_QBLOCK, SW_DIM), F32)],
        compiler_params=_params(("arbitrary", "arbitrary")),
        name="sw_attention",
    )(sink, qs, ks, vs, ks_c, vs_c, g_c)


def _ctx_attn_kernel(sink_ref, qn_ref, kn_ref, vn_ref, qs_ref, ks_ref, vs_ref, gb_ref, gc_ref,
                     yb_ref, yc_ref, y_s, *, ctx_len):
    lane_first = _lane_first(ctx_len)
    for j in range(NA_DIM // LANES):
        cols = slice(j * LANES, (j + 1) * LANES)
        y_s[:, cols] = _pair_attention(qn_ref[:, cols], None, None, None, kn_ref[:, cols], vn_ref[:, cols],
                                       None, lane_first)
    yb_ref[...] = _rms(y_s[...], gb_ref[...]).astype(BF16)
    pairs = SW_HEADS // SW_KV_HEADS
    for j in range(pairs):
        cols = slice(j * LANES, (j + 1) * LANES)
        sinks = (sink_ref[j], sink_ref[j + pairs])
        y_s[:, cols] = _pair_attention(qs_ref[:, cols], None, None, None, ks_ref[...], vs_ref[...],
                                       sinks, lane_first)
    yc_ref[...] = _rms(y_s[...], gc_ref[...]).astype(BF16)


def _ctx_attention(sink, qn, kn, vn, qs, ks, vs, g_b, g_c, *, bsz, ctx_len):
    kern = functools.partial(_ctx_attn_kernel, ctx_len=ctx_len)
    tok = lambda width: pl.BlockSpec((ctx_len, width), lambda b: (b, 0))
    return pl.pallas_call(
        kern,
        grid=(bsz,),
        in_specs=[pl.BlockSpec(memory_space=pltpu.SMEM),
                  tok(NA_DIM), tok(NA_DIM), tok(NA_DIM), tok(SW_DIM), tok(SW_KV_DIM), tok(SW_KV_DIM),
                  _const_spec((1, NA_DIM)), _const_spec((1, SW_DIM))],
        out_specs=[tok(NA_DIM), tok(SW_DIM)],
        out_shape=[jax.ShapeDtypeStruct(qn.shape, BF16), jax.ShapeDtypeStruct(qs.shape, BF16)],
        scratch_shapes=[pltpu.VMEM((ctx_len, NA_DIM), F32)],
        compiler_params=_params(("arbitrary",)),
        name="ctx_attention",
    )(sink, qn, kn, vn, qs, ks, vs, g_b, g_c)


def _mix_residual(ya_ref, yb_ref, yc_ref, x_ref, g1_ref, sh_ref, sc_ref, gn_ref, wa_ref, wb_ref, wc_ref):
    mix = _dot(ya_ref[...], wa_ref[...]) + _dot(yb_ref[...], wb_ref[...]) + _dot(yc_ref[...], wc_ref[...])
    xl = x_ref[...] + g1_ref[0] * mix
    h = _rms(xl, gn_ref[...]) * (1.0 + sc_ref[0]) + sh_ref[0]
    return xl, h


def _mix_specs(x2d, mods, group_of_tile, g_norm, ys, ws, tm):
    d = x2d.shape[1]
    tok = lambda width: pl.BlockSpec((tm, width), lambda i: (i, 0))
    mod = lambda j: pl.BlockSpec((1, 1, d), lambda i: (group_of_tile(i), 0, j))
    in_specs = ([tok(y.shape[1]) for y in ys] + [tok(d), mod(2), mod(3), mod(4), _const_spec((1, d))]
                + [_resident_spec(w.shape) for w in ws])
    return in_specs, [*ys, x2d, mods, mods, mods, g_norm, *ws]


def _outproj_router_kernel(*refs):
    *mix_refs, r_ref, xo_ref, h_ref, lg_ref = refs
    xl, h = _mix_residual(*mix_refs)
    xo_ref[...] = xl
    hi = h.astype(BF16)
    h_ref[...] = hi
    tm = h.shape[0]
    lo = (h - hi.astype(F32)).astype(BF16)
    r = _dot(jnp.concatenate([hi, lo], axis=0), r_ref[...])
    lg_ref[...] = (r[:tm, :LOGIT_LANES] + r[:tm, LOGIT_LANES:]) + (r[tm:, :LOGIT_LANES] + r[tm:, LOGIT_LANES:])


def _outproj_router(ys, x2d, mods, group_of_tile, g_norm, ws, router_w, *, tm):
    t, d = x2d.shape
    in_specs, args = _mix_specs(x2d, mods, group_of_tile, g_norm, ys, ws, tm)
    tok = lambda width: pl.BlockSpec((tm, width), lambda i: (i, 0))
    return pl.pallas_call(
        _outproj_router_kernel,
        grid=(t // tm,),
        in_specs=in_specs + [_resident_spec(router_w.shape)],
        out_specs=[tok(d), tok(d), tok(LOGIT_LANES)],
        out_shape=[jax.ShapeDtypeStruct((t, d), F32), jax.ShapeDtypeStruct((t, d), BF16),
                   jax.ShapeDtypeStruct((t, LOGIT_LANES), F32)],
        compiler_params=_params(("arbitrary",)),
        name="out_projection_router",
    )(*args, router_w)


def _outproj_ffn_kernel(*refs, chunks):
    *mix_refs, g2_ref, wg_ref, wu_ref, wd_ref, o_ref = refs
    xl, h = _mix_residual(*mix_refs)
    h = h.astype(BF16)
    acc = None
    for lo, hi in chunks:
        a = _silu(_dot(h, wg_ref[:, lo:hi].astype(BF16))) * _dot(h, wu_ref[:, lo:hi].astype(BF16))
        part = _dot(a.astype(BF16), wd_ref[lo:hi, :].astype(BF16))
        acc = part if acc is None else acc + part
    o_ref[...] = xl + g2_ref[0] * acc


def _outproj_ffn(ys, x2d, mods, group_of_tile, g_norm, ws, wg, wu, wd, *, tm):
    t, d = x2d.shape
    d_ff = wg.shape[1]
    step = 1024
    chunks = tuple((lo, min(lo + step, d_ff)) for lo in range(0, d_ff, step))
    in_specs, args = _mix_specs(x2d, mods, group_of_tile, g_norm, ys, ws, tm)
    return pl.pallas_call(
        functools.partial(_outproj_ffn_kernel, chunks=chunks),
        grid=(t // tm,),
        in_specs=in_specs + [pl.BlockSpec((1, 1, d), lambda i: (group_of_tile(i), 0, 5)),
                             _resident_spec(wg.shape), _resident_spec(wu.shape), _resident_spec(wd.shape)],
        out_specs=pl.BlockSpec((tm, d), lambda i: (i, 0)),
        out_shape=jax.ShapeDtypeStruct((t, d), F32),
        compiler_params=_params(("arbitrary",)),
        name="out_projection_ffn",
    )(*args, mods, wg, wu, wd)


def _route_kernel(*refs):
    for j in range(MOE_STEP_CHUNKS):
        _route_chunk(j, *refs)


def _route_chunk(j, lg_ref, h_ref, hl_ref, pos_ref, gsel_ref, meta_ref):
    toks = slice(j * MOE_CHUNK, (j + 1) * MOE_CHUNK)
    logits = lg_ref[toks, :]
    lane = lax.broadcasted_iota(jnp.int32, logits.shape, 1)
    lg = jnp.where(lane < N_EXPERTS, logits, -jnp.inf)
    v1 = jnp.max(lg, axis=-1, keepdims=True)
    i1 = jnp.min(jnp.where(lg == v1, lane, LOGIT_LANES), axis=-1, keepdims=True)
    lg2 = jnp.where(lane == i1, -jnp.inf, lg)
    v2 = jnp.max(lg2, axis=-1, keepdims=True)
    i2 = jnp.min(jnp.where(lg2 == v2, lane, LOGIT_LANES), axis=-1, keepdims=True)
    e2 = jnp.exp(v2 - v1)
    den = 1.0 + e2
    g_first = 1.0 / den
    g_second = e2 / den
    sel1 = lane == i1
    sel2 = lane == i2
    onehot = jnp.where(sel1, 1.0, jnp.where(sel2, 1.0, 0.0))

    r_i = lax.broadcasted_iota(jnp.int32, (MOE_CHUNK, MOE_CHUNK), 0)
    c_i = lax.broadcasted_iota(jnp.int32, (MOE_CHUNK, MOE_CHUNK), 1)
    earlier = jnp.where(c_i < r_i, 1.0, 0.0).astype(BF16)
    within = _dot(earlier, onehot.astype(BF16))
    count = jnp.sum(onehot, axis=0, keepdims=True)
    n_gran = jnp.floor((count + (MOE_GRANULE - 1)) * (1.0 / MOE_GRANULE))
    u_r = lax.broadcasted_iota(jnp.int32, (LOGIT_LANES, LOGIT_LANES), 0)
    u_c = lax.broadcasted_iota(jnp.int32, (LOGIT_LANES, LOGIT_LANES), 1)
    before = jnp.where(u_r < u_c, 1.0, 0.0).astype(BF16)
    n_gran8 = jnp.broadcast_to(n_gran, (SUBLANES, LOGIT_LANES))
    off_gran = _dot(n_gran8.astype(BF16), before)[0:1]
    lpos = off_gran * MOE_GRANULE + within
    p_first = jnp.sum(jnp.where(sel1, lpos, 0.0), axis=-1, keepdims=True)
    p_second = jnp.sum(jnp.where(sel2, lpos, 0.0), axis=-1, keepdims=True)
    pos = jnp.where(lane == 0, p_first, jnp.where(lane == 1, p_second, -1.0))
    pos_ref[toks, :] = pos
    gsel_ref[toks, :] = jnp.where(lane == 0, g_first, jnp.where(lane == 1, g_second, 0.0))
    rows8 = lax.broadcasted_iota(jnp.int32, (SUBLANES, LOGIT_LANES), 0)
    meta_ref[j] = jnp.where(rows8 == 0, off_gran, jnp.where(rows8 == 1, n_gran, 0.0)).astype(jnp.int32)

    pos_t = pos.T
    row_id = lax.broadcasted_iota(jnp.int32, (MOE_LOCAL_ROWS, MOE_CHUNK), 0).astype(F32)
    pick = jnp.where(row_id == pos_t[0:1], 1.0, jnp.where(row_id == pos_t[1:2], 1.0, 0.0)).astype(BF16)
    hl_ref[j] = _dot(pick, h_ref[toks, :]).astype(BF16)


def _route(logits, h2):
    t, d = h2.shape
    n_chunks = t // MOE_CHUNK
    per_step = MOE_STEP_CHUNKS
    tok = lambda width: pl.BlockSpec((per_step * MOE_CHUNK, width), lambda i: (i, 0))
    return pl.pallas_call(
        _route_kernel,
        grid=(n_chunks // per_step,),
        in_specs=[tok(LOGIT_LANES), tok(d)],
        out_specs=[pl.BlockSpec((per_step, MOE_LOCAL_ROWS, d), lambda i: (i, 0, 0)),
                   tok(LOGIT_LANES), tok(LOGIT_LANES),
                   pl.BlockSpec((per_step, SUBLANES, LOGIT_LANES), lambda i: (i, 0, 0))],
        out_shape=[jax.ShapeDtypeStruct((n_chunks, MOE_LOCAL_ROWS, d), BF16),
                   jax.ShapeDtypeStruct((t, LOGIT_LANES), F32),
                   jax.ShapeDtypeStruct((t, LOGIT_LANES), F32),
                   jax.ShapeDtypeStruct((n_chunks, SUBLANES, LOGIT_LANES), jnp.int32)],
        compiler_params=_params(("arbitrary",)),
        name="moe_route",
    )(logits, h2)


def _experts_kernel(meta_ref, hl_hbm, wg_ref, wu_ref, wd_ref, yl_hbm, he_s, ye_s, yb_s, zero_s,
                    gather_sem, store_sem, fill_sem, *, n_chunks):
    ws = pl.program_id(0)
    e = pl.program_id(1)
    f = pl.program_id(2)
    n_e = pl.num_programs(1)
    n_f = pl.num_programs(2)
    step = ws * n_e + e
    last_step = pl.num_programs(0) * n_e - 1

    def meta(ws_, e_, c, k):
        return meta_ref[((ws_ * n_chunks + c) * 2 + k) * N_EXPERTS + e_]

    def rows_of(granules):
        return pl.multiple_of(granules * MOE_GRANULE, MOE_GRANULE)

    def for_each_run(ws_, e_, fn):
        dst = 0
        for c in range(n_chunks):
            src, n = meta(ws_, e_, c, 0), meta(ws_, e_, c, 1)

            @pl.when(n > 0)
            def _(c=c, src=src, n=n, dst=dst):
                fn(ws_ * n_chunks + c, rows_of(src), rows_of(dst), rows_of(n))
            dst = dst + n
        return dst * MOE_GRANULE

    def gather_copy(slot):
        return lambda chunk, src, dst, rows: pltpu.make_async_copy(
            hl_hbm.at[chunk, pl.ds(src, rows)], he_s.at[slot, pl.ds(dst, rows)], gather_sem.at[slot])

    def store_copy(chunk, src, dst, rows):
        return pltpu.make_async_copy(yb_s.at[pl.ds(dst, rows)], yl_hbm.at[chunk, pl.ds(src, rows)], store_sem)

    def fill_tails(start):
        for c in range(n_chunks):
            used = meta(ws, N_EXPERTS - 1, c, 0) + meta(ws, N_EXPERTS - 1, c, 1)
            tail = MOE_LOCAL_ROWS // MOE_GRANULE - used

            @pl.when(tail > 0)
            def _(c=c, used=used, tail=tail):
                cp = pltpu.make_async_copy(zero_s.at[pl.ds(0, rows_of(tail))],
                                           yl_hbm.at[ws * n_chunks + c, pl.ds(rows_of(used), rows_of(tail))],
                                           fill_sem)
                cp.start() if start else cp.wait()

    slot = step % 2

    @pl.when(f == 0)
    def _():
        @pl.when(step == 0)
        def _():
            he_s[...] = jnp.zeros_like(he_s)
            zero_s[...] = jnp.zeros_like(zero_s)
            for_each_run(ws, e, lambda *run: gather_copy(slot)(*run).start())

        @pl.when(e == 0)
        def _():
            fill_tails(True)
            fill_tails(False)

        for_each_run(ws, e, lambda *run: gather_copy(slot)(*run).wait())

        @pl.when(step < last_step)
        def _():
            nxt = step + 1
            for_each_run(nxt // n_e, nxt % n_e, lambda *run: gather_copy(1 - slot)(*run).start())

    def m_tile(start, size):
        rows = pl.ds(pl.multiple_of(start, MOE_M_MIN), size)
        hrow = he_s[slot, rows, :]
        a = _silu(_dot(hrow, wg_ref[0].astype(BF16))) * _dot(hrow, wu_ref[0].astype(BF16))
        part = _dot(a.astype(BF16), wd_ref[0].astype(BF16))

        @pl.when(f == 0)
        def _():
            ye_s[rows, :] = part

        @pl.when(f > 0)
        def _():
            ye_s[rows, :] += part

    n_rows = sum(meta(ws, e, c, 1) for c in range(n_chunks)) * MOE_GRANULE
    n_full = n_rows // MOE_M_TILE
    rem = n_rows - n_full * MOE_M_TILE
    n_min = (n_rows + MOE_M_MIN - 1) // MOE_M_MIN

    def full_tile(m, carry):
        m_tile(m * MOE_M_TILE, MOE_M_TILE)
        return carry
    lax.fori_loop(0, n_full, full_tile, 0)
    size = MOE_M_TILE
    while size >= MOE_M_MIN:
        lo = 0 if size == MOE_M_MIN else size // 2

        @pl.when((rem > lo) & (rem <= size))
        def _(size=size):
            m_tile(n_full * MOE_M_TILE, size)
        size //= 2

    @pl.when(f == n_f - 1)
    def _():
        @pl.when(step > 0)
        def _():
            prev = step - 1
            for_each_run(prev // n_e, prev % n_e, lambda *run: store_copy(*run).wait())

        def cast_tile(m, carry):
            rows = pl.ds(pl.multiple_of(m * MOE_M_MIN, MOE_M_MIN), MOE_M_MIN)
            yb_s[rows, :] = ye_s[rows, :].astype(BF16)
            return carry
        lax.fori_loop(0, n_min, cast_tile, 0)
        for_each_run(ws, e, lambda *run: store_copy(*run).start())

        @pl.when(step == last_step)
        def _():
            for_each_run(ws, e, lambda *run: store_copy(*run).wait())


def _experts(meta, hl, wg, wu, wd, *, n_chunks):
    total_chunks, local_rows, d = hl.shape
    n_exp, _, d_ff = wg.shape
    tf = MOE_FF_TILE
    max_rows = n_chunks * MOE_CHUNK
    grid_spec = pltpu.PrefetchScalarGridSpec(
        num_scalar_prefetch=1,
        grid=(total_chunks // n_chunks, n_exp, d_ff // tf),
        in_specs=[pl.BlockSpec(memory_space=pl.ANY),
                  pl.BlockSpec((1, d, tf), lambda i, e, f, m: (e, 0, f)),
                  pl.BlockSpec((1, d, tf), lambda i, e, f, m: (e, 0, f)),
                  pl.BlockSpec((1, tf, d), lambda i, e, f, m: (e, f, 0))],
        out_specs=pl.BlockSpec(memory_space=pl.ANY),
        scratch_shapes=[pltpu.VMEM((2, max_rows, d), BF16), pltpu.VMEM((max_rows, d), F32),
                        pltpu.VMEM((max_rows, d), BF16), pltpu.VMEM((local_rows, d), BF16),
                        pltpu.SemaphoreType.DMA((2,)), pltpu.SemaphoreType.DMA(()), pltpu.SemaphoreType.DMA(())])
    return pl.pallas_call(
        functools.partial(_experts_kernel, n_chunks=n_chunks),
        grid_spec=grid_spec,
        out_shape=jax.ShapeDtypeStruct(hl.shape, BF16),
        compiler_params=_params(("arbitrary", "arbitrary", "arbitrary")),
        name="moe_experts",
    )(meta, hl, wg, wu, wd)


def _scatter_kernel(yl_ref, pos_ref, gsel_ref, x_ref, g2_ref, gf_ref, o_ref):
    row_id = lax.broadcasted_iota(jnp.int32, (MOE_CHUNK, MOE_LOCAL_ROWS), 1).astype(F32)
    for j in range(MOE_STEP_CHUNKS):
        toks = slice(j * MOE_CHUNK, (j + 1) * MOE_CHUNK)
        pos = pos_ref[toks, :]
        gsel = gsel_ref[toks, :]
        yl = yl_ref[j]
        y = None
        for k in range(2):
            pick = jnp.where(pos[:, k:k + 1] == row_id, 1.0, 0.0).astype(BF16)
            term = gsel[:, k:k + 1] * _dot(pick, yl)
            y = term if y is None else y + term
        o_ref[toks, :] = _rms(x_ref[toks, :] + g2_ref[0] * y, gf_ref[...])


def _scatter_final(yl, pos, gsel, x2d, mods, group_of_step, g_final):
    t, d = x2d.shape
    per_step = MOE_STEP_CHUNKS
    tok = lambda width: pl.BlockSpec((per_step * MOE_CHUNK, width), lambda i: (i, 0))
    return pl.pallas_call(
        _scatter_kernel,
        grid=(t // (per_step * MOE_CHUNK),),
        in_specs=[pl.BlockSpec((per_step, MOE_LOCAL_ROWS, d), lambda i: (i, 0, 0)),
                  tok(LOGIT_LANES), tok(LOGIT_LANES), tok(d),
                  pl.BlockSpec((1, 1, d), lambda i: (group_of_step(i), 0, 5)), _const_spec((1, d))],
        out_specs=tok(d),
        out_shape=jax.ShapeDtypeStruct((t, d), F32),
        compiler_params=_params(("arbitrary",)),
        name="moe_scatter_final",
    )(yl, pos, gsel, x2d, mods, g_final)


def _rope_tables(n):
    t = jnp.arange(n)
    row = (t // GRID_W).astype(F32)
    col = (t % GRID_W).astype(F32)
    n_freq = HEAD_DIM // 4
    inv = ROPE_THETA ** (-jnp.arange(n_freq, dtype=F32) / n_freq)
    ang = jnp.stack([row[:, None] * inv, col[:, None] * inv], axis=1)
    cos = jnp.cos(ang)[:, :, None, :]
    sin = jnp.sin(ang)[:, :, None, :]
    cos_f = jnp.broadcast_to(cos, (n, 2, 2, n_freq)).reshape(n, HEAD_DIM)
    sin_s = jnp.concatenate([-sin, sin], axis=2).reshape(n, HEAD_DIM)
    reps = LANES // HEAD_DIM
    return jnp.tile(cos_f, (1, reps)), jnp.tile(sin_s, (1, reps))


def _pair_perm():
    pairs = SW_HEADS // SW_KV_HEADS
    order = [h for j in range(pairs) for h in (j, j + pairs)]
    return np.concatenate([np.arange(h * HEAD_DIM, (h + 1) * HEAD_DIM) for h in order])


def kernel(x, c, ctx, c_ctx, w_ada, b_ada, g_norm1, g_norm2, w_in, conv_w, na_rpb, sw_sink, g_mix, w_out,
           ffn_w_gate, ffn_w_up, ffn_w_down, w_router, moe_w_gate, moe_w_up, moe_w_down, g_final):
    bsz, n, d = x.shape
    ctx_len = ctx.shape[1]
    depth = w_in.shape[0]
    assert w_in.shape[2] == D_IN and n % GRID_W == 0
    assert depth == 2, "layer 0 dense on both streams, layer 1 routed experts on the latent stream only"
    tm = 512
    tm_ctx = ctx_len
    set_chunks = MOE_SET_CHUNKS
    assert n % tm == 0 and (n // GRID_W) % NA_ROWS_PER_STEP == 0 and n % SW_QBLOCK == 0
    assert n % (MOE_CHUNK * MOE_STEP_CHUNKS) == 0 and (bsz * n) % (MOE_CHUNK * set_chunks) == 0
    assert (MOE_CHUNK * set_chunks) % MOE_M_TILE == 0

    cond_rows = 2 * SUBLANES
    cond = jnp.zeros((cond_rows, d), F32).at[:bsz].set(c).at[bsz].set(c_ctx)
    mods_all = _ada(cond, w_ada, b_ada).reshape(depth, cond_rows, 1, 6 * d)

    lat_group = lambda i: i // (n // tm)
    ctx_group = lambda i: bsz
    cos_t, sin_t = _rope_tables(n)
    perm = _pair_perm()

    xl = x.reshape(bsz * n, d)
    xc = ctx.reshape(bsz * ctx_len, d)
    out = None
    for l in range(depth):
        last = l == depth - 1
        mods = mods_all[l]
        w_l = w_in[l]
        w_l = jnp.concatenate([w_l[:, :OFF_SQ], w_l[:, OFF_SQ:OFF_NK][:, perm], w_l[:, OFF_NK:]], axis=1).astype(BF16)
        g_a = g_mix[l, :CONV_CH].reshape(1, CONV_CH)
        g_b = g_mix[l, CONV_CH:CONV_CH + NA_DIM].reshape(1, NA_DIM)
        g_c = g_mix[l, CONV_CH + NA_DIM:][perm].reshape(1, SW_DIM)
        wo = w_out[l]
        wa = wo[:CONV_CH].astype(BF16)
        wb = wo[CONV_CH:CONV_CH + NA_DIM].astype(BF16)
        wc = wo[CONV_CH + NA_DIM:][perm].astype(BF16)
        gn1 = g_norm1[l].reshape(1, d)
        gn2 = g_norm2[l].reshape(1, d)
        sink = sw_sink[l].astype(F32) * LOG2E
        tab = _na_bias_table(na_rpb[l])

        ya, qn, qs, kn, vn, ks, vs = _inproj(xl, mods, lat_group, gn1, w_l, conv_w[l], g_a, cos_t, sin_t,
                                             seq_len=n, tm=tm, rope=True)
        cya, cqn, cqs, ckn, cvn, cks, cvs = _inproj(xc, mods, ctx_group, gn1, w_l, conv_w[l], g_a, cos_t, sin_t,
                                                    seq_len=ctx_len, tm=tm_ctx, rope=False)
        yb = _na_attention(qn, kn, vn, ckn, cvn, tab, g_b, bsz=bsz, seq_len=n, ctx_len=ctx_len)
        yc = _sw_attention(sink, qs, ks, vs, cks, cvs, g_c, bsz=bsz, seq_len=n, ctx_len=ctx_len)

        if l % 2 == 0:
            i = l // 2
            wg, wu, wd = ffn_w_gate[i], ffn_w_up[i], ffn_w_down[i]
            xl = _outproj_ffn((ya, yb, yc), xl, mods, lat_group, gn2, (wa, wb, wc), wg, wu, wd, tm=tm)
            if not last:
                cyb, cyc = _ctx_attention(sink, cqn, ckn, cvn, cqs, cks, cvs, g_b, g_c, bsz=bsz, ctx_len=ctx_len)
                xc = _outproj_ffn((cya, cyb, cyc), xc, mods, ctx_group, gn2, (wa, wb, wc), wg, wu, wd,
                                  tm=tm_ctx)
        else:
            i = l // 2
            wr = jnp.zeros((d, LOGIT_LANES), F32).at[:, :N_EXPERTS].set(w_router[i])
            wr_hi = wr.astype(BF16)
            wr_lo = (wr - wr_hi.astype(F32)).astype(BF16)
            wr2 = jnp.concatenate([wr_hi, wr_lo], axis=1)
            wg, wu, wd = moe_w_gate[i], moe_w_up[i], moe_w_down[i]
            xl, h2, logits = _outproj_router((ya, yb, yc), xl, mods, lat_group, gn2, (wa, wb, wc), wr2, tm=tm)
            hl, pos, gsel, meta = _route(logits, h2)
            meta = meta[:, :2, :N_EXPERTS].reshape(-1)
            yl = _experts(meta, hl, wg, wu, wd, n_chunks=set_chunks)
            step_tokens = MOE_CHUNK * MOE_STEP_CHUNKS
            out = _scatter_final(yl, pos, gsel, xl, mods, lambda i: i // (n // step_tokens), g_final.reshape(1, d))
    return out.reshape(bsz, n, d)
```

```python
import functools

import numpy as np
import jax
import jax.numpy as jnp
from jax import lax
from jax.experimental import pallas as pl
from jax.experimental.pallas import tpu as pltpu

F32 = jnp.float32
BF16 = jnp.bfloat16

GRID_W = 64
HEAD_DIM = 64
CONV_CH = 256
NA_HEADS = 6
NA_DIM = NA_HEADS * HEAD_DIM
SW_HEADS = 6
SW_KV_HEADS = 2
SW_DIM = SW_HEADS * HEAD_DIM
SW_KV_DIM = SW_KV_HEADS * HEAD_DIM
NA_WIN_R = 8
NA_WIN_C = 16
SW_RADIUS = 128
ROPE_THETA = 10000.0
N_EXPERTS = 8
EPS = 1e-6
NEG_INF = -1e30
OFF_H, OFF_B, OFF_C = 0, CONV_CH, 2 * CONV_CH
OFF_NQ = 3 * CONV_CH
OFF_SQ = OFF_NQ + NA_DIM
OFF_NK = OFF_SQ + SW_DIM
OFF_NV = OFF_NK + NA_DIM
OFF_SK = OFF_NV + NA_DIM
OFF_SV = OFF_SK + SW_KV_DIM
D_IN = OFF_SV + SW_KV_DIM
LOG2E = 1.4426950408889634
Q_SCALE = HEAD_DIM ** -0.5 * LOG2E

LANES = 128
SUBLANES = 8
VMEM_BYTES_V7X = 64 * 1024 * 1024
VMEM_LIMIT = VMEM_BYTES_V7X * 7 // 8

NA_ROWS_PER_STEP = 4
NA_KEY_ROWS = NA_ROWS_PER_STEP + NA_WIN_R
NA_BIAS_MASKED = 2 * NA_WIN_R - 1
MOE_CHUNK = 256
MOE_GRANULE = 2 * SUBLANES
MOE_LOCAL_ROWS = 640
MOE_SET_CHUNKS = 16
MOE_STEP_CHUNKS = 4
MOE_M_TILE = 512
MOE_M_MIN = 128
MOE_FF_TILE = 512
NA_STEP_GROUPS = 2
SW_QBLOCK = 256
SW_STEP_BLOCKS = 4
SW_SPAN = SW_QBLOCK + 2 * SW_RADIUS
LOGIT_LANES = LANES


def _params(semantics):
    return pltpu.CompilerParams(dimension_semantics=semantics, vmem_limit_bytes=VMEM_LIMIT)


def _const_spec(shape):
    nd = len(shape)
    return pl.BlockSpec(shape, lambda *_: (0,) * nd)


def _resident_spec(shape):
    nd = len(shape)
    return pl.BlockSpec(shape, lambda *_: (0,) * nd, pipeline_mode=pl.Buffered(1))


def _rms(x, g):
    y = x * lax.rsqrt(jnp.mean(x * x, axis=-1, keepdims=True) + EPS)
    return y * g


def _silu(x):
    return x / (1.0 + jnp.exp(-x))


def _dot(a, b):
    return jnp.dot(a, b, preferred_element_type=F32)


def _dot_nt(a, b):
    return lax.dot_general(a, b, (((1,), (1,)), ((), ())), preferred_element_type=F32)


def _ada_kernel(c_ref, w_ref, b_ref, o_ref):
    a = _silu(c_ref[...])
    o_ref[0] = jnp.dot(a, w_ref[0], preferred_element_type=F32,
                       precision=lax.Precision.HIGHEST) + b_ref[0]


def _ada(cond, w_ada, b_ada):
    depth, d, n_out = w_ada.shape
    rows = cond.shape[0]
    tn = n_out // 6
    return pl.pallas_call(
        _ada_kernel,
        grid=(depth, n_out // tn),
        in_specs=[pl.BlockSpec((rows, d), lambda l, j: (0, 0)),
                  pl.BlockSpec((1, d, tn), lambda l, j: (l, 0, j)),
                  pl.BlockSpec((1, 1, tn), lambda l, j: (l, 0, j))],
        out_specs=pl.BlockSpec((1, rows, tn), lambda l, j: (l, 0, j)),
        out_shape=jax.ShapeDtypeStruct((depth, rows, n_out), F32),
        compiler_params=_params(("arbitrary", "arbitrary")),
        name="ada_modulation",
    )(cond, w_ada, b_ada.reshape(depth, 1, n_out))


def _rope(x, cos, sin, lane_lo):
    rot = jnp.where(lane_lo, pltpu.roll(x, LANES - 16, 1), pltpu.roll(x, 16, 1))
    return x * cos + rot * sin


def _inproj_kernel(x_ref, xp_ref, xn_ref, sh_ref, sc_ref, gn_ref, w_ref, cw_ref, ga_ref, cos_ref, sin_ref,
                   ya_ref, qn_ref, qs_ref, kn_ref, vn_ref, ks_ref, vs_ref, u_s, *, tm, tiles_per_seq, rope):
    i = pl.program_id(0)
    g = gn_ref[...]
    sh = sh_ref[0]
    sc = sc_ref[0]

    def normmod(x):
        return _rms(x, g) * (1.0 + sc) + sh

    h = normmod(x_ref[...]).astype(BF16)
    p = _dot(h, w_ref[...])

    xh = jnp.concatenate([xp_ref[...], xn_ref[...]], axis=0)
    ph = _dot(normmod(xh).astype(BF16), w_ref[:, :OFF_NQ])
    uh = ph[:, OFF_C:OFF_C + CONV_CH] * ph[:, OFF_H:OFF_H + CONV_CH]
    pos = i % tiles_per_seq
    u = p[:, OFF_C:OFF_C + CONV_CH] * p[:, OFF_H:OFF_H + CONV_CH]
    u_s[0:SUBLANES] = jnp.where(pos == 0, 0.0, uh[0:SUBLANES])
    u_s[SUBLANES:SUBLANES + tm] = u
    u_s[SUBLANES + tm:2 * SUBLANES + tm] = jnp.where(pos == tiles_per_seq - 1, 0.0, uh[SUBLANES:])
    cw = cw_ref[...]
    conv = (u_s[SUBLANES - 1:SUBLANES - 1 + tm] * cw[0:1] + u * cw[1:2]
            + u_s[SUBLANES + 1:SUBLANES + 1 + tm] * cw[2:3])
    ya = p[:, OFF_B:OFF_B + CONV_CH] * conv
    ya_ref[...] = _rms(ya, ga_ref[...]).astype(BF16)

    qn_ref[...] = (p[:, OFF_NQ:OFF_NQ + NA_DIM] * Q_SCALE).astype(BF16)
    kn_ref[...] = p[:, OFF_NK:OFF_NK + NA_DIM].astype(BF16)
    vn_ref[...] = p[:, OFF_NV:OFF_NV + NA_DIM].astype(BF16)
    vs_ref[...] = p[:, OFF_SV:OFF_SV + SW_KV_DIM].astype(BF16)
    if rope:
        cos = cos_ref[...]
        sin = sin_ref[...]
        lane_lo = (lax.broadcasted_iota(jnp.int32, (tm, LANES), 1) % 32) < 16
        for j in range(SW_DIM // LANES):
            q = p[:, OFF_SQ + j * LANES:OFF_SQ + (j + 1) * LANES]
            qs_ref[:, j * LANES:(j + 1) * LANES] = (_rope(q, cos, sin, lane_lo) * Q_SCALE).astype(BF16)
        ks_ref[...] = _rope(p[:, OFF_SK:OFF_SK + SW_KV_DIM], cos, sin, lane_lo).astype(BF16)
    else:
        qs_ref[...] = (p[:, OFF_SQ:OFF_SQ + SW_DIM] * Q_SCALE).astype(BF16)
        ks_ref[...] = p[:, OFF_SK:OFF_SK + SW_KV_DIM].astype(BF16)


def _inproj(x2d, mods, group_of_tile, g_norm, w, conv_w, g_a, cos_t, sin_t, *, seq_len, tm, rope):
    t, d = x2d.shape
    tiles_per_seq = seq_len // tm
    blk8 = tm // SUBLANES
    last8 = t // SUBLANES - 1
    kern = functools.partial(_inproj_kernel, tm=tm, tiles_per_seq=tiles_per_seq, rope=rope)
    tok = lambda width: pl.BlockSpec((tm, width), lambda i: (i, 0))
    out_widths = (CONV_CH, NA_DIM, SW_DIM, NA_DIM, NA_DIM, SW_KV_DIM, SW_KV_DIM)
    return pl.pallas_call(
        kern,
        grid=(t // tm,),
        in_specs=[tok(d),
                  pl.BlockSpec((SUBLANES, d), lambda i: (jnp.maximum(i * blk8 - 1, 0), 0)),
                  pl.BlockSpec((SUBLANES, d), lambda i: (jnp.minimum((i + 1) * blk8, last8), 0)),
                  pl.BlockSpec((1, 1, d), lambda i: (group_of_tile(i), 0, 0)),
                  pl.BlockSpec((1, 1, d), lambda i: (group_of_tile(i), 0, 1)),
                  _const_spec((1, d)),
                  _resident_spec(w.shape),
                  _const_spec(conv_w.shape),
                  _const_spec((1, CONV_CH)),
                  pl.BlockSpec((tm, LANES), lambda i: (i % tiles_per_seq, 0)),
                  pl.BlockSpec((tm, LANES), lambda i: (i % tiles_per_seq, 0))],
        out_specs=[tok(wd) for wd in out_widths],
        out_shape=[jax.ShapeDtypeStruct((t, wd), BF16) for wd in out_widths],
        scratch_shapes=[pltpu.VMEM((tm + 2 * SUBLANES, CONV_CH), F32)],
        compiler_params=_params(("arbitrary",)),
        name="in_projection",
    )(x2d, x2d, x2d, mods, mods, g_norm, w, conv_w, g_a, cos_t, sin_t)


def _masked_attention(qm, k_loc, v_loc, add_loc, k_ctx, v_ctx, sink):
    s_ctx = _dot_nt(qm, k_ctx)
    m = jnp.max(s_ctx, axis=-1, keepdims=True)
    if k_loc is not None:
        s_loc = _dot_nt(qm, k_loc) + add_loc
        m = jnp.maximum(m, jnp.max(s_loc, axis=-1, keepdims=True))
    if sink is not None:
        m = jnp.maximum(m, sink)
    p_ctx = jnp.exp2(s_ctx - m)
    den = jnp.sum(p_ctx, axis=-1, keepdims=True)
    o = _dot(p_ctx.astype(BF16), v_ctx)
    if k_loc is not None:
        p_loc = jnp.exp2(s_loc - m)
        den = den + jnp.sum(p_loc, axis=-1, keepdims=True)
        o = o + _dot(p_loc.astype(BF16), v_loc)
    if sink is not None:
        den = den + jnp.exp2(sink - m)
    return o / den


def _pair_attention(q, k_loc, v_loc, add_loc, k_ctx, v_ctx, sinks, lane_first):
    m = q.shape[0]
    zero = jnp.zeros_like(q)
    q2 = jnp.concatenate([jnp.where(lane_first, q, zero), jnp.where(lane_first, zero, q)], axis=0)
    sink2 = None
    if sinks is not None:
        second = lax.broadcasted_iota(jnp.int32, (2 * m, 1), 0) >= m
        sink2 = jnp.where(second, sinks[1], sinks[0])
    o = _masked_attention(q2, k_loc, v_loc, add_loc, k_ctx, v_ctx, sink2)
    return jnp.where(lane_first, o[:m], o[m:])


def _lane_first(m):
    return lax.broadcasted_iota(jnp.int32, (m, LANES), 1) < HEAD_DIM


def _na_kernel(*refs, rows):
    for t in range(NA_STEP_GROUPS):
        _na_group(t, pl.program_id(1) * NA_STEP_GROUPS + t, *refs, rows=rows)


def _na_group(t, gidx, q_ref, k_ref, v_ref, kc_ref, vc_ref, tab_ref, g_ref, o_ref, y_s, bias_s, *, rows):
    base = jnp.clip(gidx * NA_ROWS_PER_STEP - NA_WIN_R // 2, 0, rows - NA_KEY_ROWS)
    start = pl.multiple_of(base * GRID_W, GRID_W)
    nk = NA_KEY_ROWS * GRID_W
    mq = NA_ROWS_PER_STEP * GRID_W
    lane_first = _lane_first(mq)
    tile_first = _lane_first(GRID_W)

    def fill_bias(head):
        for qr in range(NA_ROWS_PER_STEP):
            r = gidx * NA_ROWS_PER_STEP + qr
            rs = jnp.clip(r - NA_WIN_R // 2, 0, rows - NA_WIN_R)
            row0 = (head % 2) * mq + qr * GRID_W
            for m in range(NA_KEY_ROWS // 2):
                idx = []
                for kr in (base + 2 * m, base + 2 * m + 1):
                    ok = (kr >= rs) & (kr < rs + NA_WIN_R)
                    idx.append(jnp.where(ok, kr - r + NA_WIN_R - 1, NA_BIAS_MASKED))
                bias_s[head // 2, row0:row0 + GRID_W, m * LANES:(m + 1) * LANES] = jnp.where(
                    tile_first, tab_ref[head, idx[0]], tab_ref[head, idx[1]])

    @pl.when((gidx <= 1) | (gidx == rows // NA_ROWS_PER_STEP - 1))
    def _():
        for head in range(NA_HEADS):
            fill_bias(head)

    qrows = slice(t * mq, (t + 1) * mq)
    for j in range(NA_DIM // LANES):
        cols = slice(j * LANES, (j + 1) * LANES)
        k_loc = k_ref[pl.ds(start, nk), cols]
        v_loc = v_ref[pl.ds(start, nk), cols]
        y_s[qrows, cols] = _pair_attention(q_ref[qrows, cols], k_loc, v_loc, bias_s[j], kc_ref[:, cols],
                                           vc_ref[:, cols], None, lane_first)
    o_ref[qrows, :] = _rms(y_s[qrows, :], g_ref[...]).astype(BF16)


def _na_bias_table(rpb):
    n_heads, n_dr, n_dc = rpb.shape
    pad = GRID_W - NA_WIN_C
    rp = jnp.pad(rpb.astype(F32), ((0, 0), (0, 0), (pad, pad)))
    toep = jnp.stack([rp[:, :, GRID_W - 1 - qc:2 * GRID_W - 1 - qc] for qc in range(GRID_W)], axis=2)
    qc = np.arange(GRID_W)
    cstart = np.clip(qc - NA_WIN_C // 2, 0, GRID_W - NA_WIN_C)
    col_ok = (qc[None, :] >= cstart[:, None]) & (qc[None, :] < cstart[:, None] + NA_WIN_C)
    tab = jnp.where(col_ok[None, None], toep * LOG2E, NEG_INF)
    masked = jnp.full((n_heads, NA_BIAS_MASKED + 1 - n_dr, GRID_W, GRID_W), NEG_INF, F32)
    tab = jnp.concatenate([tab, masked], axis=1)
    return jnp.tile(tab, (1, 1, 1, LANES // GRID_W))


def _na_attention(qn, kn, vn, kn_c, vn_c, tab, g_b, *, bsz, seq_len, ctx_len):
    rows = seq_len // GRID_W
    n_groups = rows // (NA_ROWS_PER_STEP * NA_STEP_GROUPS)
    mq = NA_ROWS_PER_STEP * GRID_W * NA_STEP_GROUPS
    kern = functools.partial(_na_kernel, rows=rows)
    return pl.pallas_call(
        kern,
        grid=(bsz, n_groups),
        in_specs=[pl.BlockSpec((mq, NA_DIM), lambda b, g: (b * n_groups + g, 0)),
                  pl.BlockSpec((seq_len, NA_DIM), lambda b, g: (b, 0)),
                  pl.BlockSpec((seq_len, NA_DIM), lambda b, g: (b, 0)),
                  pl.BlockSpec((ctx_len, NA_DIM), lambda b, g: (b, 0)),
                  pl.BlockSpec((ctx_len, NA_DIM), lambda b, g: (b, 0)),
                  _resident_spec(tab.shape),
                  _const_spec((1, NA_DIM))],
        out_specs=pl.BlockSpec((mq, NA_DIM), lambda b, g: (b * n_groups + g, 0)),
        out_shape=jax.ShapeDtypeStruct(qn.shape, BF16),
        scratch_shapes=[pltpu.VMEM((mq, NA_DIM), F32),
                        pltpu.VMEM((NA_HEADS // 2, 2 * NA_ROWS_PER_STEP * GRID_W, NA_KEY_ROWS * GRID_W), F32)],
        compiler_params=_params(("arbitrary", "arbitrary")),
        name="na_attention",
    )(qn, kn, vn, kn_c, vn_c, tab, g_b)


def _sw_kernel(*refs, seq_len):
    for t in range(SW_STEP_BLOCKS):
        _sw_block(t, pl.program_id(1) * SW_STEP_BLOCKS + t, *refs, seq_len=seq_len)


def _sw_block(t, i, sink_ref, q_ref, k_ref, v_ref, kc_ref, vc_ref, g_ref, o_ref, y_s, *, seq_len):
    s0 = i * SW_QBLOCK
    kstart = pl.multiple_of(jnp.clip(s0 - SW_RADIUS, 0, seq_len - SW_SPAN), SW_RADIUS)
    k_loc = k_ref[pl.ds(kstart, SW_SPAN), :]
    v_loc = v_ref[pl.ds(kstart, SW_SPAN), :]
    qpos = s0 + lax.broadcasted_iota(jnp.int32, (SW_QBLOCK, SW_SPAN), 0)
    kpos = kstart + lax.broadcasted_iota(jnp.int32, (SW_QBLOCK, SW_SPAN), 1)
    band = jnp.where(jnp.abs(kpos - qpos) <= SW_RADIUS, 0.0, NEG_INF).astype(F32)
    band2 = jnp.concatenate([band, band], axis=0)
    lane_first = _lane_first(SW_QBLOCK)
    pairs = SW_HEADS // SW_KV_HEADS
    qrows = slice(t * SW_QBLOCK, (t + 1) * SW_QBLOCK)
    for j in range(pairs):
        cols = slice(j * LANES, (j + 1) * LANES)
        sinks = (sink_ref[j], sink_ref[j + pairs])
        y_s[qrows, cols] = _pair_attention(q_ref[qrows, cols], k_loc, v_loc, band2, kc_ref[...], vc_ref[...],
                                           sinks, lane_first)
    o_ref[qrows, :] = _rms(y_s[qrows, :], g_ref[...]).astype(BF16)


def _sw_attention(sink, qs, ks, vs, ks_c, vs_c, g_c, *, bsz, seq_len, ctx_len):
    step_tokens = SW_QBLOCK * SW_STEP_BLOCKS
    nblk = seq_len // step_tokens
    kern = functools.partial(_sw_kernel, seq_len=seq_len)
    return pl.pallas_call(
        kern,
        grid=(bsz, nblk),
        in_specs=[pl.BlockSpec(memory_space=pltpu.SMEM),
                  pl.BlockSpec((step_tokens, SW_DIM), lambda b, i: (b * nblk + i, 0)),
                  pl.BlockSpec((seq_len, SW_KV_DIM), lambda b, i: (b, 0)),
                  pl.BlockSpec((seq_len, SW_KV_DIM), lambda b, i: (b, 0)),
                  pl.BlockSpec((ctx_len, SW_KV_DIM), lambda b, i: (b, 0)),
                  pl.BlockSpec((ctx_len, SW_KV_DIM), lambda b, i: (b, 0)),
                  _const_spec((1, SW_DIM))],
        out_specs=pl.BlockSpec((step_tokens, SW_DIM), lambda b, i: (b * nblk + i, 0)),
        out_shape=jax.ShapeDtypeStruct(qs.shape, BF16),
        scratch_shapes=[pltpu.VMEM((step_tokens, SW_DIM), F32)],
        compiler_params=_params(("arbitrary", "arbitrary")),
        name="sw_attention",
    )(sink, qs, ks, vs, ks_c, vs_c, g_c)


def _ctx_attn_kernel(sink_ref, qn_ref, kn_ref, vn_ref, qs_ref, ks_ref, vs_ref, gb_ref, gc_ref,
                     yb_ref, yc_ref, y_s, *, ctx_len):
    lane_first = _lane_first(ctx_len)
    for j in range(NA_DIM // LANES):
        cols = slice(j * LANES, (j + 1) * LANES)
        y_s[:, cols] = _pair_attention(qn_ref[:, cols], None, None, None, kn_ref[:, cols], vn_ref[:, cols],
                                       None, lane_first)
    yb_ref[...] = _rms(y_s[...], gb_ref[...]).astype(BF16)
    pairs = SW_HEADS // SW_KV_HEADS
    for j in range(pairs):
        cols = slice(j * LANES, (j + 1) * LANES)
        sinks = (sink_ref[j], sink_ref[j + pairs])
        y_s[:, cols] = _pair_attention(qs_ref[:, cols], None, None, None, ks_ref[...], vs_ref[...],
                                       sinks, lane_first)
    yc_ref[...] = _rms(y_s[...], gc_ref[...]).astype(BF16)


def _ctx_attention(sink, qn, kn, vn, qs, ks, vs, g_b, g_c, *, bsz, ctx_len):
    kern = functools.partial(_ctx_attn_kernel, ctx_len=ctx_len)
    tok = lambda width: pl.BlockSpec((ctx_len, width), lambda b: (b, 0))
    return pl.pallas_call(
        kern,
        grid=(bsz,),
        in_specs=[pl.BlockSpec(memory_space=pltpu.SMEM),
                  tok(NA_DIM), tok(NA_DIM), tok(NA_DIM), tok(SW_DIM), tok(SW_KV_DIM), tok(SW_KV_DIM),
                  _const_spec((1, NA_DIM)), _const_spec((1, SW_DIM))],
        out_specs=[tok(NA_DIM), tok(SW_DIM)],
        out_shape=[jax.ShapeDtypeStruct(qn.shape, BF16), jax.ShapeDtypeStruct(qs.shape, BF16)],
        scratch_shapes=[pltpu.VMEM((ctx_len, NA_DIM), F32)],
        compiler_params=_params(("arbitrary",)),
        name="ctx_attention",
    )(sink, qn, kn, vn, qs, ks, vs, g_b, g_c)


def _mix_residual(ya_ref, yb_ref, yc_ref, x_ref, g1_ref, sh_ref, sc_ref, gn_ref, wa_ref, wb_ref, wc_ref):
    mix = _dot(ya_ref[...], wa_ref[...]) + _dot(yb_ref[...], wb_ref[...]) + _dot(yc_ref[...], wc_ref[...])
    xl = x_ref[...] + g1_ref[0] * mix
    h = _rms(xl, gn_ref[...]) * (1.0 + sc_ref[0]) + sh_ref[0]
    return xl, h


def _mix_specs(x2d, mods, group_of_tile, g_norm, ys, ws, tm):
    d = x2d.shape[1]
    tok = lambda width: pl.BlockSpec((tm, width), lambda i: (i, 0))
    mod = lambda j: pl.BlockSpec((1, 1, d), lambda i: (group_of_tile(i), 0, j))
    in_specs = ([tok(y.shape[1]) for y in ys] + [tok(d), mod(2), mod(3), mod(4), _const_spec((1, d))]
                + [_resident_spec(w.shape) for w in ws])
    return in_specs, [*ys, x2d, mods, mods, mods, g_norm, *ws]


def _outproj_router_kernel(*refs):
    *mix_refs, r_ref, xo_ref, h_ref, lg_ref = refs
    xl, h = _mix_residual(*mix_refs)
    xo_ref[...] = xl
    hi = h.astype(BF16)
    h_ref[...] = hi
    tm = h.shape[0]
    lo = (h - hi.astype(F32)).astype(BF16)
    r = _dot(jnp.concatenate([hi, lo], axis=0), r_ref[...])
    lg_ref[...] = (r[:tm, :LOGIT_LANES] + r[:tm, LOGIT_LANES:]) + (r[tm:, :LOGIT_LANES] + r[tm:, LOGIT_LANES:])


def _outproj_router(ys, x2d, mods, group_of_tile, g_norm, ws, router_w, *, tm):
    t, d = x2d.shape
    in_specs, args = _mix_specs(x2d, mods, group_of_tile, g_norm, ys, ws, tm)
    tok = lambda width: pl.BlockSpec((tm, width), lambda i: (i, 0))
    return pl.pallas_call(
        _outproj_router_kernel,
        grid=(t // tm,),
        in_specs=in_specs + [_resident_spec(router_w.shape)],
        out_specs=[tok(d), tok(d), tok(LOGIT_LANES)],
        out_shape=[jax.ShapeDtypeStruct((t, d), F32), jax.ShapeDtypeStruct((t, d), BF16),
                   jax.ShapeDtypeStruct((t, LOGIT_LANES), F32)],
        compiler_params=_params(("arbitrary",)),
        name="out_projection_router",
    )(*args, router_w)


def _outproj_ffn_kernel(*refs, chunks):
    *mix_refs, g2_ref, wg_ref, wu_ref, wd_ref, o_ref = refs
    xl, h = _mix_residual(*mix_refs)
    h = h.astype(BF16)
    acc = None
    for lo, hi in chunks:
        a = _silu(_dot(h, wg_ref[:, lo:hi].astype(BF16))) * _dot(h, wu_ref[:, lo:hi].astype(BF16))
        part = _dot(a.astype(BF16), wd_ref[lo:hi, :].astype(BF16))
        acc = part if acc is None else acc + part
    o_ref[...] = xl + g2_ref[0] * acc


def _outproj_ffn(ys, x2d, mods, group_of_tile, g_norm, ws, wg, wu, wd, *, tm):
    t, d = x2d.shape
    d_ff = wg.shape[1]
    step = 1024
    chunks = tuple((lo, min(lo + step, d_ff)) for lo in range(0, d_ff, step))
    in_specs, args = _mix_specs(x2d, mods, group_of_tile, g_norm, ys, ws, tm)
    return pl.pallas_call(
        functools.partial(_outproj_ffn_kernel, chunks=chunks),
        grid=(t // tm,),
        in_specs=in_specs + [pl.BlockSpec((1, 1, d), lambda i: (group_of_tile(i), 0, 5)),
                             _resident_spec(wg.shape), _resident_spec(wu.shape), _resident_spec(wd.shape)],
        out_specs=pl.BlockSpec((tm, d), lambda i: (i, 0)),
        out_shape=jax.ShapeDtypeStruct((t, d), F32),
        compiler_params=_params(("arbitrary",)),
        name="out_projection_ffn",
    )(*args, mods, wg, wu, wd)


def _route_kernel(*refs):
    for j in range(MOE_STEP_CHUNKS):
        _route_chunk(j, *refs)


def _route_chunk(j, lg_ref, h_ref, hl_ref, pos_ref, gsel_ref, meta_ref):
    toks = slice(j * MOE_CHUNK, (j + 1) * MOE_CHUNK)
    logits = lg_ref[toks, :]
    lane = lax.broadcasted_iota(jnp.int32, logits.shape, 1)
    lg = jnp.where(lane < N_EXPERTS, logits, -jnp.inf)
    v1 = jnp.max(lg, axis=-1, keepdims=True)
    i1 = jnp.min(jnp.where(lg == v1, lane, LOGIT_LANES), axis=-1, keepdims=True)
    lg2 = jnp.where(lane == i1, -jnp.inf, lg)
    v2 = jnp.max(lg2, axis=-1, keepdims=True)
    i2 = jnp.min(jnp.where(lg2 == v2, lane, LOGIT_LANES), axis=-1, keepdims=True)
    e2 = jnp.exp(v2 - v1)
    den = 1.0 + e2
    g_first = 1.0 / den
    g_second = e2 / den
    sel1 = lane == i1
    sel2 = lane == i2
    onehot = jnp.where(sel1, 1.0, jnp.where(sel2, 1.0, 0.0))

    r_i = lax.broadcasted_iota(jnp.int32, (MOE_CHUNK, MOE_CHUNK), 0)
    c_i = lax.broadcasted_iota(jnp.int32, (MOE_CHUNK, MOE_CHUNK), 1)
    earlier = jnp.where(c_i < r_i, 1.0, 0.0).astype(BF16)
    within = _dot(earlier, onehot.astype(BF16))
    count = jnp.sum(onehot, axis=0, keepdims=True)
    n_gran = jnp.floor((count + (MOE_GRANULE - 1)) * (1.0 / MOE_GRANULE))
    u_r = lax.broadcasted_iota(jnp.int32, (LOGIT_LANES, LOGIT_LANES), 0)
    u_c = lax.broadcasted_iota(jnp.int32, (LOGIT_LANES, LOGIT_LANES), 1)
    before = jnp.where(u_r < u_c, 1.0, 0.0).astype(BF16)
    n_gran8 = jnp.broadcast_to(n_gran, (SUBLANES, LOGIT_LANES))
    off_gran = _dot(n_gran8.astype(BF16), before)[0:1]
    lpos = off_gran * MOE_GRANULE + within
    p_first = jnp.sum(jnp.where(sel1, lpos, 0.0), axis=-1, keepdims=True)
    p_second = jnp.sum(jnp.where(sel2, lpos, 0.0), axis=-1, keepdims=True)
    pos = jnp.where(lane == 0, p_first, jnp.where(lane == 1, p_second, -1.0))
    pos_ref[toks, :] = pos
    gsel_ref[toks, :] = jnp.where(lane == 0, g_first, jnp.where(lane == 1, g_second, 0.0))
    rows8 = lax.broadcasted_iota(jnp.int32, (SUBLANES, LOGIT_LANES), 0)
    meta_ref[j] = jnp.where(rows8 == 0, off_gran, jnp.where(rows8 == 1, n_gran, 0.0)).astype(jnp.int32)

    pos_t = pos.T
    row_id = lax.broadcasted_iota(jnp.int32, (MOE_LOCAL_ROWS, MOE_CHUNK), 0).astype(F32)
    pick = jnp.where(row_id == pos_t[0:1], 1.0, jnp.where(row_id == pos_t[1:2], 1.0, 0.0)).astype(BF16)
    hl_ref[j] = _dot(pick, h_ref[toks, :]).astype(BF16)


def _route(logits, h2):
    t, d = h2.shape
    n_chunks = t // MOE_CHUNK
    per_step = MOE_STEP_CHUNKS
    tok = lambda width: pl.BlockSpec((per_step * MOE_CHUNK, width), lambda i: (i, 0))
    return pl.pallas_call(
        _route_kernel,
        grid=(n_chunks // per_step,),
        in_specs=[tok(LOGIT_LANES), tok(d)],
        out_specs=[pl.BlockSpec((per_step, MOE_LOCAL_ROWS, d), lambda i: (i, 0, 0)),
                   tok(LOGIT_LANES), tok(LOGIT_LANES),
                   pl.BlockSpec((per_step, SUBLANES, LOGIT_LANES), lambda i: (i, 0, 0))],
        out_shape=[jax.ShapeDtypeStruct((n_chunks, MOE_LOCAL_ROWS, d), BF16),
                   jax.ShapeDtypeStruct((t, LOGIT_LANES), F32),
                   jax.ShapeDtypeStruct((t, LOGIT_LANES), F32),
                   jax.ShapeDtypeStruct((n_chunks, SUBLANES, LOGIT_LANES), jnp.int32)],
        compiler_params=_params(("arbitrary",)),
        name="moe_route",
    )(logits, h2)


def _experts_kernel(meta_ref, hl_hbm, wg_ref, wu_ref, wd_ref, yl_hbm, he_s, ye_s, yb_s, zero_s,
                    gather_sem, store_sem, fill_sem, *, n_chunks):
    ws = pl.program_id(0)
    e = pl.program_id(1)
    f = pl.program_id(2)
    n_e = pl.num_programs(1)
    n_f = pl.num_programs(2)
    step = ws * n_e + e
    last_step = pl.num_programs(0) * n_e - 1

    def meta(ws_, e_, c, k):
        return meta_ref[((ws_ * n_chunks + c) * 2 + k) * N_EXPERTS + e_]

    def rows_of(granules):
        return pl.multiple_of(granules * MOE_GRANULE, MOE_GRANULE)

    def for_each_run(ws_, e_, fn):
        dst = 0
        for c in range(n_chunks):
            src, n = meta(ws_, e_, c, 0), meta(ws_, e_, c, 1)

            @pl.when(n > 0)
            def _(c=c, src=src, n=n, dst=dst):
                fn(ws_ * n_chunks + c, rows_of(src), rows_of(dst), rows_of(n))
            dst = dst + n
        return dst * MOE_GRANULE

    def gather_copy(slot):
        return lambda chunk, src, dst, rows: pltpu.make_async_copy(
            hl_hbm.at[chunk, pl.ds(src, rows)], he_s.at[slot, pl.ds(dst, rows)], gather_sem.at[slot])

    def store_copy(chunk, src, dst, rows):
        return pltpu.make_async_copy(yb_s.at[pl.ds(dst, rows)], yl_hbm.at[chunk, pl.ds(src, rows)], store_sem)

    def fill_tails(start):
        for c in range(n_chunks):
            used = meta(ws, N_EXPERTS - 1, c, 0) + meta(ws, N_EXPERTS - 1, c, 1)
            tail = MOE_LOCAL_ROWS // MOE_GRANULE - used

            @pl.when(tail > 0)
            def _(c=c, used=used, tail=tail):
                cp = pltpu.make_async_copy(zero_s.at[pl.ds(0, rows_of(tail))],
                                           yl_hbm.at[ws * n_chunks + c, pl.ds(rows_of(used), rows_of(tail))],
                                           fill_sem)
                cp.start() if start else cp.wait()

    slot = step % 2

    @pl.when(f == 0)
    def _():
        @pl.when(step == 0)
        def _():
            he_s[...] = jnp.zeros_like(he_s)
            zero_s[...] = jnp.zeros_like(zero_s)
            for_each_run(ws, e, lambda *run: gather_copy(slot)(*run).start())

        @pl.when(e == 0)
        def _():
            fill_tails(True)
            fill_tails(False)

        for_each_run(ws, e, lambda *run: gather_copy(slot)(*run).wait())

        @pl.when(step < last_step)
        def _():
            nxt = step + 1
            for_each_run(nxt // n_e, nxt % n_e, lambda *run: gather_copy(1 - slot)(*run).start())

    def m_tile(start, size, first):
        rows = pl.ds(pl.multiple_of(start, MOE_M_MIN), size)
        hrow = he_s[slot, rows, :]
        a = _silu(_dot(hrow, wg_ref[0].astype(BF16))) * _dot(hrow, wu_ref[0].astype(BF16))
        part = _dot(a.astype(BF16), wd_ref[0].astype(BF16))
        if first:
            ye_s[rows, :] = part
        else:
            ye_s[rows, :] += part

    n_rows = sum(meta(ws, e, c, 1) for c in range(n_chunks)) * MOE_GRANULE
    n_full = n_rows // MOE_M_TILE
    rem = n_rows - n_full * MOE_M_TILE
    n_min = (n_rows + MOE_M_MIN - 1) // MOE_M_MIN

    def all_tiles(first):
        def tile_pair(m, carry):
            m_tile(2 * m * MOE_M_TILE, MOE_M_TILE, first)
            m_tile((2 * m + 1) * MOE_M_TILE, MOE_M_TILE, first)
            return carry
        lax.fori_loop(0, n_full // 2, tile_pair, 0)

        @pl.when(n_full % 2 == 1)
        def _():
            m_tile((n_full - 1) * MOE_M_TILE, MOE_M_TILE, first)

        size = MOE_M_TILE
        while size >= MOE_M_MIN:
            lo = 0 if size == MOE_M_MIN else size // 2

            @pl.when((rem > lo) & (rem <= size))
            def _(size=size):
                m_tile(n_full * MOE_M_TILE, size, first)
            size //= 2

    @pl.when(f == 0)
    def _():
        all_tiles(True)

    @pl.when(f > 0)
    def _():
        all_tiles(False)

    @pl.when(f == n_f - 1)
    def _():
        @pl.when(step > 0)
        def _():
            prev = step - 1
            for_each_run(prev // n_e, prev % n_e, lambda *run: store_copy(*run).wait())

        def cast_tile(m, carry):
            rows = pl.ds(pl.multiple_of(m * MOE_M_MIN, MOE_M_MIN), MOE_M_MIN)
            yb_s[rows, :] = ye_s[rows, :].astype(BF16)
            return carry
        lax.fori_loop(0, n_min, cast_tile, 0)
        for_each_run(ws, e, lambda *run: store_copy(*run).start())

        @pl.when(step == last_step)
        def _():
            for_each_run(ws, e, lambda *run: store_copy(*run).wait())


def _experts(meta, hl, wg, wu, wd, *, n_chunks):
    total_chunks, local_rows, d = hl.shape
    n_exp, _, d_ff = wg.shape
    tf = MOE_FF_TILE
    max_rows = n_chunks * MOE_CHUNK
    grid_spec = pltpu.PrefetchScalarGridSpec(
        num_scalar_prefetch=1,
        grid=(total_chunks // n_chunks, n_exp, d_ff // tf),
        in_specs=[pl.BlockSpec(memory_space=pl.ANY),
                  pl.BlockSpec((1, d, tf), lambda i, e, f, m: (e, 0, f)),
                  pl.BlockSpec((1, d, tf), lambda i, e, f, m: (e, 0, f)),
                  pl.BlockSpec((1, tf, d), lambda i, e, f, m: (e, f, 0))],
        out_specs=pl.BlockSpec(memory_space=pl.ANY),
        scratch_shapes=[pltpu.VMEM((2, max_rows, d), BF16), pltpu.VMEM((max_rows, d), F32),
                        pltpu.VMEM((max_rows, d), BF16), pltpu.VMEM((local_rows, d), BF16),
                        pltpu.SemaphoreType.DMA((2,)), pltpu.SemaphoreType.DMA(()), pltpu.SemaphoreType.DMA(())])
    return pl.pallas_call(
        functools.partial(_experts_kernel, n_chunks=n_chunks),
        grid_spec=grid_spec,
        out_shape=jax.ShapeDtypeStruct(hl.shape, BF16),
        compiler_params=_params(("arbitrary", "arbitrary", "arbitrary")),
        name="moe_experts",
    )(meta, hl, wg, wu, wd)


def _scatter_kernel(yl_ref, pos_ref, gsel_ref, x_ref, g2_ref, gf_ref, o_ref):
    row_id = lax.broadcasted_iota(jnp.int32, (MOE_CHUNK, MOE_LOCAL_ROWS), 1).astype(F32)
    for j in range(MOE_STEP_CHUNKS):
        toks = slice(j * MOE_CHUNK, (j + 1) * MOE_CHUNK)
        pos = pos_ref[toks, :]
        gsel = gsel_ref[toks, :]
        yl = yl_ref[j]
        y = None
        for k in range(2):
            pick = jnp.where(pos[:, k:k + 1] == row_id, 1.0, 0.0).astype(BF16)
            term = gsel[:, k:k + 1] * _dot(pick, yl)
            y = term if y is None else y + term
        o_ref[toks, :] = _rms(x_ref[toks, :] + g2_ref[0] * y, gf_ref[...])


def _scatter_final(yl, pos, gsel, x2d, mods, group_of_step, g_final):
    t, d = x2d.shape
    per_step = MOE_STEP_CHUNKS
    tok = lambda width: pl.BlockSpec((per_step * MOE_CHUNK, width), lambda i: (i, 0))
    return pl.pallas_call(
        _scatter_kernel,
        grid=(t // (per_step * MOE_CHUNK),),
        in_specs=[pl.BlockSpec((per_step, MOE_LOCAL_ROWS, d), lambda i: (i, 0, 0)),
                  tok(LOGIT_LANES), tok(LOGIT_LANES), tok(d),
                  pl.BlockSpec((1, 1, d), lambda i: (group_of_step(i), 0, 5)), _const_spec((1, d))],
        out_specs=tok(d),
        out_shape=jax.ShapeDtypeStruct((t, d), F32),
        compiler_params=_params(("arbitrary",)),
        name="moe_scatter_final",
    )(yl, pos, gsel, x2d, mods, g_final)


def _rope_tables(n):
    t = jnp.arange(n)
    row = (t // GRID_W).astype(F32)
    col = (t % GRID_W).astype(F32)
    n_freq = HEAD_DIM // 4
    inv = ROPE_THETA ** (-jnp.arange(n_freq, dtype=F32) / n_freq)
    ang = jnp.stack([row[:, None] * inv, col[:, None] * inv], axis=1)
    cos = jnp.cos(ang)[:, :, None, :]
    sin = jnp.sin(ang)[:, :, None, :]
    cos_f = jnp.broadcast_to(cos, (n, 2, 2, n_freq)).reshape(n, HEAD_DIM)
    sin_s = jnp.concatenate([-sin, sin], axis=2).reshape(n, HEAD_DIM)
    reps = LANES // HEAD_DIM
    return jnp.tile(cos_f, (1, reps)), jnp.tile(sin_s, (1, reps))


def _pair_perm():
    pairs = SW_HEADS // SW_KV_HEADS
    order = [h for j in range(pairs) for h in (j, j + pairs)]
    return np.concatenate([np.arange(h * HEAD_DIM, (h + 1) * HEAD_DIM) for h in order])


def kernel(x, c, ctx, c_ctx, w_ada, b_ada, g_norm1, g_norm2, w_in, conv_w, na_rpb, sw_sink, g_mix, w_out,
           ffn_w_gate, ffn_w_up, ffn_w_down, w_router, moe_w_gate, moe_w_up, moe_w_down, g_final):
    bsz, n, d = x.shape
    ctx_len = ctx.shape[1]
    depth = w_in.shape[0]
    assert w_in.shape[2] == D_IN and n % GRID_W == 0
    assert depth == 2, "layer 0 dense on both streams, layer 1 routed experts on the latent stream only"
    tm = 512
    tm_ctx = ctx_len
    set_chunks = MOE_SET_CHUNKS
    assert n % tm == 0 and (n // GRID_W) % (NA_ROWS_PER_STEP * NA_STEP_GROUPS) == 0
    assert n % (SW_QBLOCK * SW_STEP_BLOCKS) == 0 and tm % MOE_CHUNK == 0
    assert n % (MOE_CHUNK * MOE_STEP_CHUNKS) == 0 and (bsz * n) % (MOE_CHUNK * set_chunks) == 0
    assert (MOE_CHUNK * set_chunks) % MOE_M_TILE == 0

    cond_rows = 2 * SUBLANES
    cond = jnp.zeros((cond_rows, d), F32).at[:bsz].set(c).at[bsz].set(c_ctx)
    mods_all = _ada(cond, w_ada, b_ada).reshape(depth, cond_rows, 1, 6 * d)

    lat_group = lambda i: i // (n // tm)
    ctx_group = lambda i: bsz
    cos_t, sin_t = _rope_tables(n)
    perm = _pair_perm()

    xl = x.reshape(bsz * n, d)
    xc = ctx.reshape(bsz * ctx_len, d)
    out = None
    for l in range(depth):
        last = l == depth - 1
        mods = mods_all[l]
        w_l = w_in[l]
        w_l = jnp.concatenate([w_l[:, :OFF_SQ], w_l[:, OFF_SQ:OFF_NK][:, perm], w_l[:, OFF_NK:]], axis=1).astype(BF16)
        g_a = g_mix[l, :CONV_CH].reshape(1, CONV_CH)
        g_b = g_mix[l, CONV_CH:CONV_CH + NA_DIM].reshape(1, NA_DIM)
        g_c = g_mix[l, CONV_CH + NA_DIM:][perm].reshape(1, SW_DIM)
        wo = w_out[l]
        wa = wo[:CONV_CH].astype(BF16)
        wb = wo[CONV_CH:CONV_CH + NA_DIM].astype(BF16)
        wc = wo[CONV_CH + NA_DIM:][perm].astype(BF16)
        gn1 = g_norm1[l].reshape(1, d)
        gn2 = g_norm2[l].reshape(1, d)
        sink = sw_sink[l].astype(F32) * LOG2E
        tab = _na_bias_table(na_rpb[l])

        ya, qn, qs, kn, vn, ks, vs = _inproj(xl, mods, lat_group, gn1, w_l, conv_w[l], g_a, cos_t, sin_t,
                                             seq_len=n, tm=tm, rope=True)
        cya, cqn, cqs, ckn, cvn, cks, cvs = _inproj(xc, mods, ctx_group, gn1, w_l, conv_w[l], g_a, cos_t, sin_t,
                                                    seq_len=ctx_len, tm=tm_ctx, rope=False)
        yb = _na_attention(qn, kn, vn, ckn, cvn, tab, g_b, bsz=bsz, seq_len=n, ctx_len=ctx_len)
        yc = _sw_attention(sink, qs, ks, vs, cks, cvs, g_c, bsz=bsz, seq_len=n, ctx_len=ctx_len)

        if l % 2 == 0:
            i = l // 2
            wg, wu, wd = ffn_w_gate[i], ffn_w_up[i], ffn_w_down[i]
            xl = _outproj_ffn((ya, yb, yc), xl, mods, lat_group, gn2, (wa, wb, wc), wg, wu, wd, tm=tm)
            if not last:
                cyb, cyc = _ctx_attention(sink, cqn, ckn, cvn, cqs, cks, cvs, g_b, g_c, bsz=bsz, ctx_len=ctx_len)
                xc = _outproj_ffn((cya, cyb, cyc), xc, mods, ctx_group, gn2, (wa, wb, wc), wg, wu, wd,
                                  tm=tm_ctx)
        else:
            i = l // 2
            wr = jnp.zeros((d, LOGIT_LANES), F32).at[:, :N_EXPERTS].set(w_router[i])
            wr_hi = wr.astype(BF16)
            wr_lo = (wr - wr_hi.astype(F32)).astype(BF16)
            wr2 = jnp.concatenate([wr_hi, wr_lo], axis=1)
            wg, wu, wd = moe_w_gate[i], moe_w_up[i], moe_w_down[i]
            xl, h2, logits = _outproj_router((ya, yb, yc), xl, mods, lat_group, gn2, (wa, wb, wc), wr2, tm=tm)
            hl, pos, gsel, meta = _route(logits, h2)
            meta = meta[:, :2, :N_EXPERTS].reshape(-1)
            yl = _experts(meta, hl, wg, wu, wd, n_chunks=set_chunks)
            step_tokens = MOE_CHUNK * MOE_STEP_CHUNKS
            out = _scatter_final(yl, pos, gsel, xl, mods, lambda i: i // (n // step_tokens), g_final.reshape(1, d))
    return out.reshape(bsz, n, d)
```

```python
import functools

import numpy as np
import jax
import jax.numpy as jnp
from jax import lax
from jax.experimental import pallas as pl
from jax.experimental.pallas import tpu as pltpu

F32 = jnp.float32
BF16 = jnp.bfloat16

GRID_W = 64
HEAD_DIM = 64
CONV_CH = 256
NA_HEADS = 6
NA_DIM = NA_HEADS * HEAD_DIM
SW_HEADS = 6
SW_KV_HEADS = 2
SW_DIM = SW_HEADS * HEAD_DIM
SW_KV_DIM = SW_KV_HEADS * HEAD_DIM
NA_WIN_R = 8
NA_WIN_C = 16
SW_RADIUS = 128
ROPE_THETA = 10000.0
N_EXPERTS = 8
EPS = 1e-6
NEG_INF = -1e30
OFF_H, OFF_B, OFF_C = 0, CONV_CH, 2 * CONV_CH
OFF_NQ = 3 * CONV_CH
OFF_SQ = OFF_NQ + NA_DIM
OFF_NK = OFF_SQ + SW_DIM
OFF_NV = OFF_NK + NA_DIM
OFF_SK = OFF_NV + NA_DIM
OFF_SV = OFF_SK + SW_KV_DIM
D_IN = OFF_SV + SW_KV_DIM
LOG2E = 1.4426950408889634
Q_SCALE = HEAD_DIM ** -0.5 * LOG2E

LANES = 128
SUBLANES = 8
VMEM_BYTES_V7X = 64 * 1024 * 1024
VMEM_LIMIT = VMEM_BYTES_V7X * 7 // 8

NA_ROWS_PER_STEP = 4
NA_KEY_ROWS = NA_ROWS_PER_STEP + NA_WIN_R
NA_BIAS_MASKED = 2 * NA_WIN_R - 1
MOE_CHUNK = 256
MOE_GRANULE = 2 * SUBLANES
MOE_LOCAL_ROWS = 640
MOE_SET_CHUNKS = 16
MOE_STEP_CHUNKS = 4
MOE_M_TILE = 512
MOE_M_MIN = 128
MOE_FF_TILE = 512
NA_STEP_GROUPS = 2
SW_QBLOCK = 256
SW_STEP_BLOCKS = 4
SW_SPAN = SW_QBLOCK + 2 * SW_RADIUS
LOGIT_LANES = LANES


def _params(semantics):
    return pltpu.CompilerParams(dimension_semantics=semantics, vmem_limit_bytes=VMEM_LIMIT)


def _const_spec(shape):
    nd = len(shape)
    return pl.BlockSpec(shape, lambda *_: (0,) * nd)


def _resident_spec(shape):
    nd = len(shape)
    return pl.BlockSpec(shape, lambda *_: (0,) * nd, pipeline_mode=pl.Buffered(1))


def _rms(x, g):
    y = x * lax.rsqrt(jnp.mean(x * x, axis=-1, keepdims=True) + EPS)
    return y * g


def _silu(x):
    return x / (1.0 + jnp.exp(-x))


def _dot(a, b):
    return jnp.dot(a, b, preferred_element_type=F32)


def _dot_nt(a, b):
    return lax.dot_general(a, b, (((1,), (1,)), ((), ())), preferred_element_type=F32)


def _ada_kernel(c_ref, w_ref, b_ref, o_ref):
    a = _silu(c_ref[...])
    o_ref[0] = jnp.dot(a, w_ref[0], preferred_element_type=F32,
                       precision=lax.Precision.HIGHEST) + b_ref[0]


def _ada(cond, w_ada, b_ada):
    depth, d, n_out = w_ada.shape
    rows = cond.shape[0]
    tn = n_out // 6
    return pl.pallas_call(
        _ada_kernel,
        grid=(depth, n_out // tn),
        in_specs=[pl.BlockSpec((rows, d), lambda l, j: (0, 0)),
                  pl.BlockSpec((1, d, tn), lambda l, j: (l, 0, j)),
                  pl.BlockSpec((1, 1, tn), lambda l, j: (l, 0, j))],
        out_specs=pl.BlockSpec((1, rows, tn), lambda l, j: (l, 0, j)),
        out_shape=jax.ShapeDtypeStruct((depth, rows, n_out), F32),
        compiler_params=_params(("arbitrary", "arbitrary")),
        name="ada_modulation",
    )(cond, w_ada, b_ada.reshape(depth, 1, n_out))


def _rope(x, cos, sin, lane_lo):
    rot = jnp.where(lane_lo, pltpu.roll(x, LANES - 16, 1), pltpu.roll(x, 16, 1))
    return x * cos + rot * sin


def _inproj_kernel(x_ref, xp_ref, xn_ref, sh_ref, sc_ref, gn_ref, w_ref, cw_ref, ga_ref, cos_ref, sin_ref,
                   ya_ref, qn_ref, qs_ref, kn_ref, vn_ref, ks_ref, vs_ref, u_s, *, tm, tiles_per_seq, rope):
    i = pl.program_id(0)
    g = gn_ref[...]
    sh = sh_ref[0]
    sc = sc_ref[0]

    def normmod(x):
        return _rms(x, g) * (1.0 + sc) + sh

    h = normmod(x_ref[...]).astype(BF16)
    p = _dot(h, w_ref[...])

    xh = jnp.concatenate([xp_ref[...], xn_ref[...]], axis=0)
    ph = _dot(normmod(xh).astype(BF16), w_ref[:, :OFF_NQ])
    uh = ph[:, OFF_C:OFF_C + CONV_CH] * ph[:, OFF_H:OFF_H + CONV_CH]
    pos = i % tiles_per_seq
    u = p[:, OFF_C:OFF_C + CONV_CH] * p[:, OFF_H:OFF_H + CONV_CH]
    u_s[0:SUBLANES] = jnp.where(pos == 0, 0.0, uh[0:SUBLANES])
    u_s[SUBLANES:SUBLANES + tm] = u
    u_s[SUBLANES + tm:2 * SUBLANES + tm] = jnp.where(pos == tiles_per_seq - 1, 0.0, uh[SUBLANES:])
    cw = cw_ref[...]
    conv = (u_s[SUBLANES - 1:SUBLANES - 1 + tm] * cw[0:1] + u * cw[1:2]
            + u_s[SUBLANES + 1:SUBLANES + 1 + tm] * cw[2:3])
    ya = p[:, OFF_B:OFF_B + CONV_CH] * conv
    ya_ref[...] = _rms(ya, ga_ref[...]).astype(BF16)

    qn_ref[...] = (p[:, OFF_NQ:OFF_NQ + NA_DIM] * Q_SCALE).astype(BF16)
    kn_ref[...] = p[:, OFF_NK:OFF_NK + NA_DIM].astype(BF16)
    vn_ref[...] = p[:, OFF_NV:OFF_NV + NA_DIM].astype(BF16)
    vs_ref[...] = p[:, OFF_SV:OFF_SV + SW_KV_DIM].astype(BF16)
    if rope:
        cos = cos_ref[...]
        sin = sin_ref[...]
        lane_lo = (lax.broadcasted_iota(jnp.int32, (tm, LANES), 1) % 32) < 16
        for j in range(SW_DIM // LANES):
            q = p[:, OFF_SQ + j * LANES:OFF_SQ + (j + 1) * LANES]
            qs_ref[:, j * LANES:(j + 1) * LANES] = (_rope(q, cos, sin, lane_lo) * Q_SCALE).astype(BF16)
        ks_ref[...] = _rope(p[:, OFF_SK:OFF_SK + SW_KV_DIM], cos, sin, lane_lo).astype(BF16)
    else:
        qs_ref[...] = (p[:, OFF_SQ:OFF_SQ + SW_DIM] * Q_SCALE).astype(BF16)
        ks_ref[...] = p[:, OFF_SK:OFF_SK + SW_KV_DIM].astype(BF16)


def _inproj(x2d, mods, group_of_tile, g_norm, w, conv_w, g_a, cos_t, sin_t, *, seq_len, tm, rope):
    t, d = x2d.shape
    tiles_per_seq = seq_len // tm
    blk8 = tm // SUBLANES
    last8 = t // SUBLANES - 1
    kern = functools.partial(_inproj_kernel, tm=tm, tiles_per_seq=tiles_per_seq, rope=rope)
    tok = lambda width: pl.BlockSpec((tm, width), lambda i: (i, 0))
    out_widths = (CONV_CH, NA_DIM, SW_DIM, NA_DIM, NA_DIM, SW_KV_DIM, SW_KV_DIM)
    return pl.pallas_call(
        kern,
        grid=(t // tm,),
        in_specs=[tok(d),
                  pl.BlockSpec((SUBLANES, d), lambda i: (jnp.maximum(i * blk8 - 1, 0), 0)),
                  pl.BlockSpec((SUBLANES, d), lambda i: (jnp.minimum((i + 1) * blk8, last8), 0)),
                  pl.BlockSpec((1, 1, d), lambda i: (group_of_tile(i), 0, 0)),
                  pl.BlockSpec((1, 1, d), lambda i: (group_of_tile(i), 0, 1)),
                  _const_spec((1, d)),
                  _resident_spec(w.shape),
                  _const_spec(conv_w.shape),
                  _const_spec((1, CONV_CH)),
                  pl.BlockSpec((tm, LANES), lambda i: (i % tiles_per_seq, 0)),
                  pl.BlockSpec((tm, LANES), lambda i: (i % tiles_per_seq, 0))],
        out_specs=[tok(wd) for wd in out_widths],
        out_shape=[jax.ShapeDtypeStruct((t, wd), BF16) for wd in out_widths],
        scratch_shapes=[pltpu.VMEM((tm + 2 * SUBLANES, CONV_CH), F32)],
        compiler_params=_params(("arbitrary",)),
        name="in_projection",
    )(x2d, x2d, x2d, mods, mods, g_norm, w, conv_w, g_a, cos_t, sin_t)


def _masked_attention(qm, k_loc, v_loc, add_loc, k_ctx, v_ctx, sink):
    s_ctx = _dot_nt(qm, k_ctx)
    m = jnp.max(s_ctx, axis=-1, keepdims=True)
    if k_loc is not None:
        s_loc = _dot_nt(qm, k_loc) + add_loc
        m = jnp.maximum(m, jnp.max(s_loc, axis=-1, keepdims=True))
    if sink is not None:
        m = jnp.maximum(m, sink)
    p_ctx = jnp.exp2(s_ctx - m)
    den = jnp.sum(p_ctx, axis=-1, keepdims=True)
    o = _dot(p_ctx.astype(BF16), v_ctx)
    if k_loc is not None:
        p_loc = jnp.exp2(s_loc - m)
        den = den + jnp.sum(p_loc, axis=-1, keepdims=True)
        o = o + _dot(p_loc.astype(BF16), v_loc)
    if sink is not None:
        den = den + jnp.exp2(sink - m)
    return o / den


def _pair_attention(q, k_loc, v_loc, add_loc, k_ctx, v_ctx, sinks, lane_first):
    m = q.shape[0]
    zero = jnp.zeros_like(q)
    q2 = jnp.concatenate([jnp.where(lane_first, q, zero), jnp.where(lane_first, zero, q)], axis=0)
    sink2 = None
    if sinks is not None:
        second = lax.broadcasted_iota(jnp.int32, (2 * m, 1), 0) >= m
        sink2 = jnp.where(second, sinks[1], sinks[0])
    o = _masked_attention(q2, k_loc, v_loc, add_loc, k_ctx, v_ctx, sink2)
    return jnp.where(lane_first, o[:m], o[m:])


def _lane_first(m):
    return lax.broadcasted_iota(jnp.int32, (m, LANES), 1) < HEAD_DIM


def _na_kernel(*refs, rows):
    first = pl.program_id(1) * NA_STEP_GROUPS
    last_group = rows // NA_ROWS_PER_STEP - 1
    rebuilds = (first <= 1) | (first + NA_STEP_GROUPS - 1 >= last_group)

    @pl.when(rebuilds)
    def _():
        for t in range(NA_STEP_GROUPS):
            _na_group(t, first + t, *refs, rows=rows, may_rebuild=True)

    @pl.when(jnp.logical_not(rebuilds))
    def _():
        for t in range(NA_STEP_GROUPS):
            _na_group(t, first + t, *refs, rows=rows, may_rebuild=False)


def _na_group(t, gidx, q_ref, k_ref, v_ref, kc_ref, vc_ref, tab_ref, g_ref, o_ref, y_s, bias_s, *, rows,
              may_rebuild):
    base = jnp.clip(gidx * NA_ROWS_PER_STEP - NA_WIN_R // 2, 0, rows - NA_KEY_ROWS)
    start = pl.multiple_of(base * GRID_W, GRID_W)
    nk = NA_KEY_ROWS * GRID_W
    mq = NA_ROWS_PER_STEP * GRID_W
    lane_first = _lane_first(mq)
    tile_first = _lane_first(GRID_W)

    def fill_bias(head):
        for qr in range(NA_ROWS_PER_STEP):
            r = gidx * NA_ROWS_PER_STEP + qr
            rs = jnp.clip(r - NA_WIN_R // 2, 0, rows - NA_WIN_R)
            row0 = (head % 2) * mq + qr * GRID_W
            for m in range(NA_KEY_ROWS // 2):
                idx = []
                for kr in (base + 2 * m, base + 2 * m + 1):
                    ok = (kr >= rs) & (kr < rs + NA_WIN_R)
                    idx.append(jnp.where(ok, kr - r + NA_WIN_R - 1, NA_BIAS_MASKED))
                bias_s[head // 2, row0:row0 + GRID_W, m * LANES:(m + 1) * LANES] = jnp.where(
                    tile_first, tab_ref[head, idx[0]], tab_ref[head, idx[1]])

    if may_rebuild:
        @pl.when((gidx <= 1) | (gidx == rows // NA_ROWS_PER_STEP - 1))
        def _():
            for head in range(NA_HEADS):
                fill_bias(head)

    qrows = slice(t * mq, (t + 1) * mq)
    for j in range(NA_DIM // LANES):
        cols = slice(j * LANES, (j + 1) * LANES)
        k_loc = k_ref[pl.ds(start, nk), cols]
        v_loc = v_ref[pl.ds(start, nk), cols]
        y_s[qrows, cols] = _pair_attention(q_ref[qrows, cols], k_loc, v_loc, bias_s[j], kc_ref[:, cols],
                                           vc_ref[:, cols], None, lane_first)
    o_ref[qrows, :] = _rms(y_s[qrows, :], g_ref[...]).astype(BF16)


def _na_bias_table(rpb):
    n_heads, n_dr, n_dc = rpb.shape
    pad = GRID_W - NA_WIN_C
    rp = jnp.pad(rpb.astype(F32), ((0, 0), (0, 0), (pad, pad)))
    toep = jnp.stack([rp[:, :, GRID_W - 1 - qc:2 * GRID_W - 1 - qc] for qc in range(GRID_W)], axis=2)
    qc = np.arange(GRID_W)
    cstart = np.clip(qc - NA_WIN_C // 2, 0, GRID_W - NA_WIN_C)
    col_ok = (qc[None, :] >= cstart[:, None]) & (qc[None, :] < cstart[:, None] + NA_WIN_C)
    tab = jnp.where(col_ok[None, None], toep * LOG2E, NEG_INF)
    masked = jnp.full((n_heads, NA_BIAS_MASKED + 1 - n_dr, GRID_W, GRID_W), NEG_INF, F32)
    tab = jnp.concatenate([tab, masked], axis=1)
    return jnp.tile(tab, (1, 1, 1, LANES // GRID_W))


def _na_attention(qn, kn, vn, kn_c, vn_c, tab, g_b, *, bsz, seq_len, ctx_len):
    rows = seq_len // GRID_W
    n_groups = rows // (NA_ROWS_PER_STEP * NA_STEP_GROUPS)
    mq = NA_ROWS_PER_STEP * GRID_W * NA_STEP_GROUPS
    kern = functools.partial(_na_kernel, rows=rows)
    return pl.pallas_call(
        kern,
        grid=(bsz, n_groups),
        in_specs=[pl.BlockSpec((mq, NA_DIM), lambda b, g: (b * n_groups + g, 0)),
                  pl.BlockSpec((seq_len, NA_DIM), lambda b, g: (b, 0)),
                  pl.BlockSpec((seq_len, NA_DIM), lambda b, g: (b, 0)),
                  pl.BlockSpec((ctx_len, NA_DIM), lambda b, g: (b, 0)),
                  pl.BlockSpec((ctx_len, NA_DIM), lambda b, g: (b, 0)),
                  _resident_spec(tab.shape),
                  _const_spec((1, NA_DIM))],
        out_specs=pl.BlockSpec((mq, NA_DIM), lambda b, g: (b * n_groups + g, 0)),
        out_shape=jax.ShapeDtypeStruct(qn.shape, BF16),
        scratch_shapes=[pltpu.VMEM((mq, NA_DIM), F32),
                        pltpu.VMEM((NA_HEADS // 2, 2 * NA_ROWS_PER_STEP * GRID_W, NA_KEY_ROWS * GRID_W), F32)],
        compiler_params=_params(("arbitrary", "arbitrary")),
        name="na_attention",
    )(qn, kn, vn, kn_c, vn_c, tab, g_b)


def _sw_kernel(*refs, seq_len):
    for t in range(SW_STEP_BLOCKS):
        _sw_block(t, pl.program_id(1) * SW_STEP_BLOCKS + t, *refs, seq_len=seq_len)


def _sw_block(t, i, sink_ref, q_ref, k_ref, v_ref, kc_ref, vc_ref, g_ref, o_ref, y_s, *, seq_len):
    s0 = i * SW_QBLOCK
    kstart = pl.multiple_of(jnp.clip(s0 - SW_RADIUS, 0, seq_len - SW_SPAN), SW_RADIUS)
    k_loc = k_ref[pl.ds(kstart, SW_SPAN), :]
    v_loc = v_ref[pl.ds(kstart, SW_SPAN), :]
    qpos = s0 + lax.broadcasted_iota(jnp.int32, (SW_QBLOCK, SW_SPAN), 0)
    kpos = kstart + lax.broadcasted_iota(jnp.int32, (SW_QBLOCK, SW_SPAN), 1)
    band = jnp.where(jnp.abs(kpos - qpos) <= SW_RADIUS, 0.0, NEG_INF).astype(F32)
    band2 = jnp.concatenate([band, band], axis=0)
    lane_first = _lane_first(SW_QBLOCK)
    pairs = SW_HEADS // SW_KV_HEADS
    qrows = slice(t * SW_QBLOCK, (t + 1) * SW_QBLOCK)
    for j in range(pairs):
        cols = slice(j * LANES, (j + 1) * LANES)
        sinks = (sink_ref[j], sink_ref[j + pairs])
        y_s[qrows, cols] = _pair_attention(q_ref[qrows, cols], k_loc, v_loc, band2, kc_ref[...], vc_ref[...],
                                           sinks, lane_first)
    o_ref[qrows, :] = _rms(y_s[qrows, :], g_ref[...]).astype(BF16)


def _sw_attention(sink, qs, ks, vs, ks_c, vs_c, g_c, *, bsz, seq_len, ctx_len):
    step_tokens = SW_QBLOCK * SW_STEP_BLOCKS
    nblk = seq_len // step_tokens
    kern = functools.partial(_sw_kernel, seq_len=seq_len)
    return pl.pallas_call(
        kern,
        grid=(bsz, nblk),
        in_specs=[pl.BlockSpec(memory_space=pltpu.SMEM),
                  pl.BlockSpec((step_tokens, SW_DIM), lambda b, i: (b * nblk + i, 0)),
                  pl.BlockSpec((seq_len, SW_KV_DIM), lambda b, i: (b, 0)),
                  pl.BlockSpec((seq_len, SW_KV_DIM), lambda b, i: (b, 0)),
                  pl.BlockSpec((ctx_len, SW_KV_DIM), lambda b, i: (b, 0)),
                  pl.BlockSpec((ctx_len, SW_KV_DIM), lambda b, i: (b, 0)),
                  _const_spec((1, SW_DIM))],
        out_specs=pl.BlockSpec((step_tokens, SW_DIM), lambda b, i: (b * nblk + i, 0)),
        out_shape=jax.ShapeDtypeStruct(qs.shape, BF16),
        scratch_shapes=[pltpu.VMEM((step_tokens, SW_DIM), F32)],
        compiler_params=_params(("arbitrary", "arbitrary")),
        name="sw_attention",
    )(sink, qs, ks, vs, ks_c, vs_c, g_c)


def _ctx_attn_kernel(sink_ref, qn_ref, kn_ref, vn_ref, qs_ref, ks_ref, vs_ref, gb_ref, gc_ref,
                     yb_ref, yc_ref, y_s, *, ctx_len):
    lane_first = _lane_first(ctx_len)
    for j in range(NA_DIM // LANES):
        cols = slice(j * LANES, (j + 1) * LANES)
        y_s[:, cols] = _pair_attention(qn_ref[:, cols], None, None, None, kn_ref[:, cols], vn_ref[:, cols],
                                       None, lane_first)
    yb_ref[...] = _rms(y_s[...], gb_ref[...]).astype(BF16)
    pairs = SW_HEADS // SW_KV_HEADS
    for j in range(pairs):
        cols = slice(j * LANES, (j + 1) * LANES)
        sinks = (sink_ref[j], sink_ref[j + pairs])
        y_s[:, cols] = _pair_attention(qs_ref[:, cols], None, None, None, ks_ref[...], vs_ref[...],
                                       sinks, lane_first)
    yc_ref[...] = _rms(y_s[...], gc_ref[...]).astype(BF16)


def _ctx_attention(sink, qn, kn, vn, qs, ks, vs, g_b, g_c, *, bsz, ctx_len):
    kern = functools.partial(_ctx_attn_kernel, ctx_len=ctx_len)
    tok = lambda width: pl.BlockSpec((ctx_len, width), lambda b: (b, 0))
    return pl.pallas_call(
        kern,
        grid=(bsz,),
        in_specs=[pl.BlockSpec(memory_space=pltpu.SMEM),
                  tok(NA_DIM), tok(NA_DIM), tok(NA_DIM), tok(SW_DIM), tok(SW_KV_DIM), tok(SW_KV_DIM),
                  _const_spec((1, NA_DIM)), _const_spec((1, SW_DIM))],
        out_specs=[tok(NA_DIM), tok(SW_DIM)],
        out_shape=[jax.ShapeDtypeStruct(qn.shape, BF16), jax.ShapeDtypeStruct(qs.shape, BF16)],
        scratch_shapes=[pltpu.VMEM((ctx_len, NA_DIM), F32)],
        compiler_params=_params(("arbitrary",)),
        name="ctx_attention",
    )(sink, qn, kn, vn, qs, ks, vs, g_b, g_c)


def _mix_residual(ya_ref, yb_ref, yc_ref, x_ref, g1_ref, sh_ref, sc_ref, gn_ref, wa_ref, wb_ref, wc_ref):
    mix = _dot(ya_ref[...], wa_ref[...]) + _dot(yb_ref[...], wb_ref[...]) + _dot(yc_ref[...], wc_ref[...])
    xl = x_ref[...] + g1_ref[0] * mix
    h = _rms(xl, gn_ref[...]) * (1.0 + sc_ref[0]) + sh_ref[0]
    return xl, h


def _mix_specs(x2d, mods, group_of_tile, g_norm, ys, ws, tm):
    d = x2d.shape[1]
    tok = lambda width: pl.BlockSpec((tm, width), lambda i: (i, 0))
    mod = lambda j: pl.BlockSpec((1, 1, d), lambda i: (group_of_tile(i), 0, j))
    in_specs = ([tok(y.shape[1]) for y in ys] + [tok(d), mod(2), mod(3), mod(4), _const_spec((1, d))]
                + [_resident_spec(w.shape) for w in ws])
    return in_specs, [*ys, x2d, mods, mods, mods, g_norm, *ws]


def _outproj_router_kernel(*refs):
    *mix_refs, r_ref, xo_ref, h_ref, lg_ref = refs
    xl, h = _mix_residual(*mix_refs)
    xo_ref[...] = xl
    hi = h.astype(BF16)
    h_ref[...] = hi
    tm = h.shape[0]
    lo = (h - hi.astype(F32)).astype(BF16)
    r = _dot(jnp.concatenate([hi, lo], axis=0), r_ref[...])
    lg_ref[...] = (r[:tm, :LOGIT_LANES] + r[:tm, LOGIT_LANES:]) + (r[tm:, :LOGIT_LANES] + r[tm:, LOGIT_LANES:])


def _outproj_router(ys, x2d, mods, group_of_tile, g_norm, ws, router_w, *, tm):
    t, d = x2d.shape
    in_specs, args = _mix_specs(x2d, mods, group_of_tile, g_norm, ys, ws, tm)
    tok = lambda width: pl.BlockSpec((tm, width), lambda i: (i, 0))
    return pl.pallas_call(
        _outproj_router_kernel,
        grid=(t // tm,),
        in_specs=in_specs + [_resident_spec(router_w.shape)],
        out_specs=[tok(d), tok(d), tok(LOGIT_LANES)],
        out_shape=[jax.ShapeDtypeStruct((t, d), F32), jax.ShapeDtypeStruct((t, d), BF16),
                   jax.ShapeDtypeStruct((t, LOGIT_LANES), F32)],
        compiler_params=_params(("arbitrary",)),
        name="out_projection_router",
    )(*args, router_w)


def _outproj_ffn_kernel(*refs, chunks):
    *mix_refs, g2_ref, wg_ref, wu_ref, wd_ref, o_ref = refs
    xl, h = _mix_residual(*mix_refs)
    h = h.astype(BF16)
    acc = None
    for lo, hi in chunks:
        a = _silu(_dot(h, wg_ref[:, lo:hi].astype(BF16))) * _dot(h, wu_ref[:, lo:hi].astype(BF16))
        part = _dot(a.astype(BF16), wd_ref[lo:hi, :].astype(BF16))
        acc = part if acc is None else acc + part
    o_ref[...] = xl + g2_ref[0] * acc


def _outproj_ffn(ys, x2d, mods, group_of_tile, g_norm, ws, wg, wu, wd, *, tm):
    t, d = x2d.shape
    d_ff = wg.shape[1]
    step = 1024
    chunks = tuple((lo, min(lo + step, d_ff)) for lo in range(0, d_ff, step))
    in_specs, args = _mix_specs(x2d, mods, group_of_tile, g_norm, ys, ws, tm)
    return pl.pallas_call(
        functools.partial(_outproj_ffn_kernel, chunks=chunks),
        grid=(t // tm,),
        in_specs=in_specs + [pl.BlockSpec((1, 1, d), lambda i: (group_of_tile(i), 0, 5)),
                             _resident_spec(wg.shape), _resident_spec(wu.shape), _resident_spec(wd.shape)],
        out_specs=pl.BlockSpec((tm, d), lambda i: (i, 0)),
        out_shape=jax.ShapeDtypeStruct((t, d), F32),
        compiler_params=_params(("arbitrary",)),
        name="out_projection_ffn",
    )(*args, mods, wg, wu, wd)


def _route_kernel(*refs):
    for j in range(MOE_STEP_CHUNKS):
        _route_chunk(j, *refs)


def _route_chunk(j, lg_ref, h_ref, hl_ref, pos_ref, gsel_ref, meta_ref):
    toks = slice(j * MOE_CHUNK, (j + 1) * MOE_CHUNK)
    logits = lg_ref[toks, :]
    lane = lax.broadcasted_iota(jnp.int32, logits.shape, 1)
    lg = jnp.where(lane < N_EXPERTS, logits, -jnp.inf)
    v1 = jnp.max(lg, axis=-1, keepdims=True)
    i1 = jnp.min(jnp.where(lg == v1, lane, LOGIT_LANES), axis=-1, keepdims=True)
    lg2 = jnp.where(lane == i1, -jnp.inf, lg)
    v2 = jnp.max(lg2, axis=-1, keepdims=True)
    i2 = jnp.min(jnp.where(lg2 == v2, lane, LOGIT_LANES), axis=-1, keepdims=True)
    e2 = jnp.exp(v2 - v1)
    den = 1.0 + e2
    g_first = 1.0 / den
    g_second = e2 / den
    sel1 = lane == i1
    sel2 = lane == i2
    onehot = jnp.where(sel1, 1.0, jnp.where(sel2, 1.0, 0.0))

    r_i = lax.broadcasted_iota(jnp.int32, (MOE_CHUNK, MOE_CHUNK), 0)
    c_i = lax.broadcasted_iota(jnp.int32, (MOE_CHUNK, MOE_CHUNK), 1)
    earlier = jnp.where(c_i < r_i, 1.0, 0.0).astype(BF16)
    within = _dot(earlier, onehot.astype(BF16))
    count = jnp.sum(onehot, axis=0, keepdims=True)
    n_gran = jnp.floor((count + (MOE_GRANULE - 1)) * (1.0 / MOE_GRANULE))
    u_r = lax.broadcasted_iota(jnp.int32, (LOGIT_LANES, LOGIT_LANES), 0)
    u_c = lax.broadcasted_iota(jnp.int32, (LOGIT_LANES, LOGIT_LANES), 1)
    before = jnp.where(u_r < u_c, 1.0, 0.0).astype(BF16)
    n_gran8 = jnp.broadcast_to(n_gran, (SUBLANES, LOGIT_LANES))
    off_gran = _dot(n_gran8.astype(BF16), before)[0:1]
    lpos = off_gran * MOE_GRANULE + within
    p_first = jnp.sum(jnp.where(sel1, lpos, 0.0), axis=-1, keepdims=True)
    p_second = jnp.sum(jnp.where(sel2, lpos, 0.0), axis=-1, keepdims=True)
    pos = jnp.where(lane == 0, p_first, jnp.where(lane == 1, p_second, -1.0))
    pos_ref[toks, :] = pos
    gsel_ref[toks, :] = jnp.where(lane == 0, g_first, jnp.where(lane == 1, g_second, 0.0))
    rows8 = lax.broadcasted_iota(jnp.int32, (SUBLANES, LOGIT_LANES), 0)
    meta_ref[j] = jnp.where(rows8 == 0, off_gran, jnp.where(rows8 == 1, n_gran, 0.0)).astype(jnp.int32)

    pos_t = pos.T
    row_id = lax.broadcasted_iota(jnp.int32, (MOE_LOCAL_ROWS, MOE_CHUNK), 0).astype(F32)
    pick = jnp.where(row_id == pos_t[0:1], 1.0, jnp.where(row_id == pos_t[1:2], 1.0, 0.0)).astype(BF16)
    hl_ref[j] = _dot(pick, h_ref[toks, :]).astype(BF16)


def _route(logits, h2):
    t, d = h2.shape
    n_chunks = t // MOE_CHUNK
    per_step = MOE_STEP_CHUNKS
    tok = lambda width: pl.BlockSpec((per_step * MOE_CHUNK, width), lambda i: (i, 0))
    return pl.pallas_call(
        _route_kernel,
        grid=(n_chunks // per_step,),
        in_specs=[tok(LOGIT_LANES), tok(d)],
        out_specs=[pl.BlockSpec((per_step, MOE_LOCAL_ROWS, d), lambda i: (i, 0, 0)),
                   tok(LOGIT_LANES), tok(LOGIT_LANES),
                   pl.BlockSpec((per_step, SUBLANES, LOGIT_LANES), lambda i: (i, 0, 0))],
        out_shape=[jax.ShapeDtypeStruct((n_chunks, MOE_LOCAL_ROWS, d), BF16),
                   jax.ShapeDtypeStruct((t, LOGIT_LANES), F32),
                   jax.ShapeDtypeStruct((t, LOGIT_LANES), F32),
                   jax.ShapeDtypeStruct((n_chunks, SUBLANES, LOGIT_LANES), jnp.int32)],
        compiler_params=_params(("arbitrary",)),
        name="moe_route",
    )(logits, h2)


def _experts_kernel(meta_ref, hl_hbm, wg_ref, wu_ref, wd_ref, yl_hbm, he_s, ye_s, yb_s, zero_s,
                    gather_sem, store_sem, fill_sem, *, n_chunks):
    ws = pl.program_id(0)
    e = pl.program_id(1)
    f = pl.program_id(2)
    n_e = pl.num_programs(1)
    n_f = pl.num_programs(2)
    step = ws * n_e + e
    last_step = pl.num_programs(0) * n_e - 1

    def meta(ws_, e_, c, k):
        return meta_ref[((ws_ * n_chunks + c) * 2 + k) * N_EXPERTS + e_]

    def rows_of(granules):
        return pl.multiple_of(granules * MOE_GRANULE, MOE_GRANULE)

    def for_each_run(ws_, e_, fn):
        dst = 0
        for c in range(n_chunks):
            src, n = meta(ws_, e_, c, 0), meta(ws_, e_, c, 1)

            @pl.when(n > 0)
            def _(c=c, src=src, n=n, dst=dst):
                fn(ws_ * n_chunks + c, rows_of(src), rows_of(dst), rows_of(n))
            dst = dst + n
        return dst * MOE_GRANULE

    def gather_copy(slot):
        return lambda chunk, src, dst, rows: pltpu.make_async_copy(
            hl_hbm.at[chunk, pl.ds(src, rows)], he_s.at[slot, pl.ds(dst, rows)], gather_sem.at[slot])

    def store_copy(chunk, src, dst, rows):
        return pltpu.make_async_copy(yb_s.at[pl.ds(dst, rows)], yl_hbm.at[chunk, pl.ds(src, rows)], store_sem)

    def fill_tails(start):
        for c in range(n_chunks):
            used = meta(ws, N_EXPERTS - 1, c, 0) + meta(ws, N_EXPERTS - 1, c, 1)
            tail = MOE_LOCAL_ROWS // MOE_GRANULE - used

            @pl.when(tail > 0)
            def _(c=c, used=used, tail=tail):
                cp = pltpu.make_async_copy(zero_s.at[pl.ds(0, rows_of(tail))],
                                           yl_hbm.at[ws * n_chunks + c, pl.ds(rows_of(used), rows_of(tail))],
                                           fill_sem)
                cp.start() if start else cp.wait()

    slot = step % 2

    @pl.when(f == 0)
    def _():
        @pl.when(step == 0)
        def _():
            he_s[...] = jnp.zeros_like(he_s)
            zero_s[...] = jnp.zeros_like(zero_s)
            for_each_run(ws, e, lambda *run: gather_copy(slot)(*run).start())

        @pl.when(e == 0)
        def _():
            fill_tails(True)
            fill_tails(False)

        for_each_run(ws, e, lambda *run: gather_copy(slot)(*run).wait())

        @pl.when(step < last_step)
        def _():
            nxt = step + 1
            for_each_run(nxt // n_e, nxt % n_e, lambda *run: gather_copy(1 - slot)(*run).start())

    def m_tile(start, size, mode):
        rows = pl.ds(pl.multiple_of(start, MOE_M_MIN), size)
        hrow = he_s[slot, rows, :]
        a = _silu(_dot(hrow, wg_ref[0].astype(BF16))) * _dot(hrow, wu_ref[0].astype(BF16))
        part = _dot(a.astype(BF16), wd_ref[0].astype(BF16))
        if mode == "first":
            ye_s[rows, :] = part
        elif mode == "middle":
            ye_s[rows, :] += part
        else:
            yb_s[rows, :] = (ye_s[rows, :] + part).astype(BF16)

    n_rows = sum(meta(ws, e, c, 1) for c in range(n_chunks)) * MOE_GRANULE
    n_full = n_rows // MOE_M_TILE
    rem = n_rows - n_full * MOE_M_TILE

    def all_tiles(mode):
        def tile_pair(m, carry):
            m_tile(2 * m * MOE_M_TILE, MOE_M_TILE, mode)
            m_tile((2 * m + 1) * MOE_M_TILE, MOE_M_TILE, mode)
            return carry
        lax.fori_loop(0, n_full // 2, tile_pair, 0)

        @pl.when(n_full % 2 == 1)
        def _():
            m_tile((n_full - 1) * MOE_M_TILE, MOE_M_TILE, mode)

        size = MOE_M_TILE
        while size >= MOE_M_MIN:
            lo = 0 if size == MOE_M_MIN else size // 2

            @pl.when((rem > lo) & (rem <= size))
            def _(size=size):
                m_tile(n_full * MOE_M_TILE, size, mode)
            size //= 2

    @pl.when(f == 0)
    def _():
        all_tiles("first")

    @pl.when((f > 0) & (f < n_f - 1))
    def _():
        all_tiles("middle")

    @pl.when(f == n_f - 1)
    def _():
        @pl.when(step > 0)
        def _():
            prev = step - 1
            for_each_run(prev // n_e, prev % n_e, lambda *run: store_copy(*run).wait())

        all_tiles("last")
        for_each_run(ws, e, lambda *run: store_copy(*run).start())

        @pl.when(step == last_step)
        def _():
            for_each_run(ws, e, lambda *run: store_copy(*run).wait())


def _experts(meta, hl, wg, wu, wd, *, n_chunks):
    total_chunks, local_rows, d = hl.shape
    n_exp, _, d_ff = wg.shape
    tf = MOE_FF_TILE
    assert d_ff // tf >= 2, "the first and the last hidden tile must be different grid steps"
    max_rows = n_chunks * MOE_CHUNK
    grid_spec = pltpu.PrefetchScalarGridSpec(
        num_scalar_prefetch=1,
        grid=(total_chunks // n_chunks, n_exp, d_ff // tf),
        in_specs=[pl.BlockSpec(memory_space=pl.ANY),
                  pl.BlockSpec((1, d, tf), lambda i, e, f, m: (e, 0, f)),
                  pl.BlockSpec((1, d, tf), lambda i, e, f, m: (e, 0, f)),
                  pl.BlockSpec((1, tf, d), lambda i, e, f, m: (e, f, 0))],
        out_specs=pl.BlockSpec(memory_space=pl.ANY),
        scratch_shapes=[pltpu.VMEM((2, max_rows, d), BF16), pltpu.VMEM((max_rows, d), F32),
                        pltpu.VMEM((max_rows, d), BF16), pltpu.VMEM((local_rows, d), BF16),
                        pltpu.SemaphoreType.DMA((2,)), pltpu.SemaphoreType.DMA(()), pltpu.SemaphoreType.DMA(())])
    return pl.pallas_call(
        functools.partial(_experts_kernel, n_chunks=n_chunks),
        grid_spec=grid_spec,
        out_shape=jax.ShapeDtypeStruct(hl.shape, BF16),
        compiler_params=_params(("arbitrary", "arbitrary", "arbitrary")),
        name="moe_experts",
    )(meta, hl, wg, wu, wd)


def _scatter_kernel(yl_ref, pos_ref, gsel_ref, x_ref, g2_ref, gf_ref, o_ref):
    row_id = lax.broadcasted_iota(jnp.int32, (MOE_CHUNK, MOE_LOCAL_ROWS), 1).astype(F32)
    for j in range(MOE_STEP_CHUNKS):
        toks = slice(j * MOE_CHUNK, (j + 1) * MOE_CHUNK)
        pos = pos_ref[toks, :]
        gsel = gsel_ref[toks, :]
        yl = yl_ref[j]
        y = None
        for k in range(2):
            pick = jnp.where(pos[:, k:k + 1] == row_id, 1.0, 0.0).astype(BF16)
            term = gsel[:, k:k + 1] * _dot(pick, yl)
            y = term if y is None else y + term
        o_ref[toks, :] = _rms(x_ref[toks, :] + g2_ref[0] * y, gf_ref[...])


def _scatter_final(yl, pos, gsel, x2d, mods, group_of_step, g_final):
    t, d = x2d.shape
    per_step = MOE_STEP_CHUNKS
    tok = lambda width: pl.BlockSpec((per_step * MOE_CHUNK, width), lambda i: (i, 0))
    return pl.pallas_call(
        _scatter_kernel,
        grid=(t // (per_step * MOE_CHUNK),),
        in_specs=[pl.BlockSpec((per_step, MOE_LOCAL_ROWS, d), lambda i: (i, 0, 0)),
                  tok(LOGIT_LANES), tok(LOGIT_LANES), tok(d),
                  pl.BlockSpec((1, 1, d), lambda i: (group_of_step(i), 0, 5)), _const_spec((1, d))],
        out_specs=tok(d),
        out_shape=jax.ShapeDtypeStruct((t, d), F32),
        compiler_params=_params(("arbitrary",)),
        name="moe_scatter_final",
    )(yl, pos, gsel, x2d, mods, g_final)


def _rope_tables(n):
    t = jnp.arange(n)
    row = (t // GRID_W).astype(F32)
    col = (t % GRID_W).astype(F32)
    n_freq = HEAD_DIM // 4
    inv = ROPE_THETA ** (-jnp.arange(n_freq, dtype=F32) / n_freq)
    ang = jnp.stack([row[:, None] * inv, col[:, None] * inv], axis=1)
    cos = jnp.cos(ang)[:, :, None, :]
    sin = jnp.sin(ang)[:, :, None, :]
    cos_f = jnp.broadcast_to(cos, (n, 2, 2, n_freq)).reshape(n, HEAD_DIM)
    sin_s = jnp.concatenate([-sin, sin], axis=2).reshape(n, HEAD_DIM)
    reps = LANES // HEAD_DIM
    return jnp.tile(cos_f, (1, reps)), jnp.tile(sin_s, (1, reps))


def _pair_perm():
    pairs = SW_HEADS // SW_KV_HEADS
    order = [h for j in range(pairs) for h in (j, j + pairs)]
    return np.concatenate([np.arange(h * HEAD_DIM, (h + 1) * HEAD_DIM) for h in order])


def kernel(x, c, ctx, c_ctx, w_ada, b_ada, g_norm1, g_norm2, w_in, conv_w, na_rpb, sw_sink, g_mix, w_out,
           ffn_w_gate, ffn_w_up, ffn_w_down, w_router, moe_w_gate, moe_w_up, moe_w_down, g_final):
    bsz, n, d = x.shape
    ctx_len = ctx.shape[1]
    depth = w_in.shape[0]
    assert w_in.shape[2] == D_IN and n % GRID_W == 0
    assert depth == 2, "layer 0 dense on both streams, layer 1 routed experts on the latent stream only"
    tm = 512
    tm_in = 1024
    tm_ctx = ctx_len
    set_chunks = MOE_SET_CHUNKS
    assert n % tm == 0 and n % tm_in == 0 and (n // GRID_W) % (NA_ROWS_PER_STEP * NA_STEP_GROUPS) == 0
    assert n % (SW_QBLOCK * SW_STEP_BLOCKS) == 0 and tm % MOE_CHUNK == 0
    assert n % (MOE_CHUNK * MOE_STEP_CHUNKS) == 0 and (bsz * n) % (MOE_CHUNK * set_chunks) == 0
    assert (MOE_CHUNK * set_chunks) % MOE_M_TILE == 0

    cond_rows = 2 * SUBLANES
    cond = jnp.zeros((cond_rows, d), F32).at[:bsz].set(c).at[bsz].set(c_ctx)
    mods_all = _ada(cond, w_ada, b_ada).reshape(depth, cond_rows, 1, 6 * d)

    lat_group = lambda i: i // (n // tm)
    ctx_group = lambda i: bsz
    cos_t, sin_t = _rope_tables(n)
    perm = _pair_perm()

    xl = x.reshape(bsz * n, d)
    xc = ctx.reshape(bsz * ctx_len, d)
    out = None
    for l in range(depth):
        last = l == depth - 1
        mods = mods_all[l]
        w_l = w_in[l]
        w_l = jnp.concatenate([w_l[:, :OFF_SQ], w_l[:, OFF_SQ:OFF_NK][:, perm], w_l[:, OFF_NK:]], axis=1).astype(BF16)
        g_a = g_mix[l, :CONV_CH].reshape(1, CONV_CH)
        g_b = g_mix[l, CONV_CH:CONV_CH + NA_DIM].reshape(1, NA_DIM)
        g_c = g_mix[l, CONV_CH + NA_DIM:][perm].reshape(1, SW_DIM)
        wo = w_out[l]
        wa = wo[:CONV_CH].astype(BF16)
        wb = wo[CONV_CH:CONV_CH + NA_DIM].astype(BF16)
        wc = wo[CONV_CH + NA_DIM:][perm].astype(BF16)
        gn1 = g_norm1[l].reshape(1, d)
        gn2 = g_norm2[l].reshape(1, d)
        sink = sw_sink[l].astype(F32) * LOG2E
        tab = _na_bias_table(na_rpb[l])

        ya, qn, qs, kn, vn, ks, vs = _inproj(xl, mods, lambda i: i // (n // tm_in), gn1, w_l, conv_w[l], g_a,
                                             cos_t, sin_t, seq_len=n, tm=tm_in, rope=True)
        cya, cqn, cqs, ckn, cvn, cks, cvs = _inproj(xc, mods, ctx_group, gn1, w_l, conv_w[l], g_a, cos_t, sin_t,
                                                    seq_len=ctx_len, tm=tm_ctx, rope=False)
        yb = _na_attention(qn, kn, vn, ckn, cvn, tab, g_b, bsz=bsz, seq_len=n, ctx_len=ctx_len)
        yc = _sw_attention(sink, qs, ks, vs, cks, cvs, g_c, bsz=bsz, seq_len=n, ctx_len=ctx_len)

        if l % 2 == 0:
            i = l // 2
            wg, wu, wd = ffn_w_gate[i], ffn_w_up[i], ffn_w_down[i]
            xl = _outproj_ffn((ya, yb, yc), xl, mods, lat_group, gn2, (wa, wb, wc), wg, wu, wd, tm=tm)
            if not last:
                cyb, cyc = _ctx_attention(sink, cqn, ckn, cvn, cqs, cks, cvs, g_b, g_c, bsz=bsz, ctx_len=ctx_len)
                xc = _outproj_ffn((cya, cyb, cyc), xc, mods, ctx_group, gn2, (wa, wb, wc), wg, wu, wd,
                                  tm=tm_ctx)
        else:
            i = l // 2
            wr = jnp.zeros((d, LOGIT_LANES), F32).at[:, :N_EXPERTS].set(w_router[i])
            wr_hi = wr.astype(BF16)
            wr_lo = (wr - wr_hi.astype(F32)).astype(BF16)
            wr2 = jnp.concatenate([wr_hi, wr_lo], axis=1)
            wg, wu, wd = moe_w_gate[i], moe_w_up[i], moe_w_down[i]
            xl, h2, logits = _outproj_router((ya, yb, yc), xl, mods, lat_group, gn2, (wa, wb, wc), wr2, tm=tm)
            hl, pos, gsel, meta = _route(logits, h2)
            meta = meta[:, :2, :N_EXPERTS].reshape(-1)
            yl = _experts(meta, hl, wg, wu, wd, n_chunks=set_chunks)
            step_tokens = MOE_CHUNK * MOE_STEP_CHUNKS
            out = _scatter_final(yl, pos, gsel, xl, mods, lambda i: i // (n // step_tokens), g_final.reshape(1, d))
    return out.reshape(bsz, n, d)
```

```python
import functools

import numpy as np
import jax
import jax.numpy as jnp
from jax import lax
from jax.experimental import pallas as pl
from jax.experimental.pallas import tpu as pltpu

F32 = jnp.float32
BF16 = jnp.bfloat16

GRID_W = 64
HEAD_DIM = 64
CONV_CH = 256
NA_HEADS = 6
NA_DIM = NA_HEADS * HEAD_DIM
SW_HEADS = 6
SW_KV_HEADS = 2
SW_DIM = SW_HEADS * HEAD_DIM
SW_KV_DIM = SW_KV_HEADS * HEAD_DIM
NA_WIN_R = 8
NA_WIN_C = 16
SW_RADIUS = 128
ROPE_THETA = 10000.0
N_EXPERTS = 8
EPS = 1e-6
NEG_INF = -1e30
OFF_H, OFF_B, OFF_C = 0, CONV_CH, 2 * CONV_CH
OFF_NQ = 3 * CONV_CH
OFF_SQ = OFF_NQ + NA_DIM
OFF_NK = OFF_SQ + SW_DIM
OFF_NV = OFF_NK + NA_DIM
OFF_SK = OFF_NV + NA_DIM
OFF_SV = OFF_SK + SW_KV_DIM
D_IN = OFF_SV + SW_KV_DIM
LOG2E = 1.4426950408889634
Q_SCALE = HEAD_DIM ** -0.5 * LOG2E

LANES = 128
SUBLANES = 8
VMEM_BYTES_V7X = 64 * 1024 * 1024
VMEM_LIMIT = VMEM_BYTES_V7X * 7 // 8

NA_ROWS_PER_STEP = 4
NA_KEY_ROWS = NA_ROWS_PER_STEP + NA_WIN_R
NA_BIAS_MASKED = 2 * NA_WIN_R - 1
MOE_CHUNK = 256
MOE_GRANULE = 2 * SUBLANES
MOE_LOCAL_ROWS = 640
MOE_SET_CHUNKS = 16
MOE_STEP_CHUNKS = 4
MOE_M_TILE = 512
MOE_M_MIN = 128
MOE_FF_TILE = 512
NA_STEP_GROUPS = 2
SW_QBLOCK = 256
SW_STEP_BLOCKS = 4
SW_SPAN = SW_QBLOCK + 2 * SW_RADIUS
LOGIT_LANES = LANES


def _params(semantics):
    return pltpu.CompilerParams(dimension_semantics=semantics, vmem_limit_bytes=VMEM_LIMIT)


def _const_spec(shape):
    nd = len(shape)
    return pl.BlockSpec(shape, lambda *_: (0,) * nd)


def _resident_spec(shape):
    nd = len(shape)
    return pl.BlockSpec(shape, lambda *_: (0,) * nd, pipeline_mode=pl.Buffered(1))


def _rms(x, g):
    y = x * lax.rsqrt(jnp.mean(x * x, axis=-1, keepdims=True) + EPS)
    return y * g


def _silu(x):
    return x / (1.0 + jnp.exp(-x))


def _dot(a, b):
    return jnp.dot(a, b, preferred_element_type=F32)


def _dot_nt(a, b):
    return lax.dot_general(a, b, (((1,), (1,)), ((), ())), preferred_element_type=F32)


def _ada_kernel(c_ref, w_ref, b_ref, o_ref):
    a = _silu(c_ref[...])
    o_ref[0] = jnp.dot(a, w_ref[0], preferred_element_type=F32,
                       precision=lax.Precision.HIGHEST) + b_ref[0]


def _ada(cond, w_ada, b_ada):
    depth, d, n_out = w_ada.shape
    rows = cond.shape[0]
    tn = n_out // 6
    return pl.pallas_call(
        _ada_kernel,
        grid=(depth, n_out // tn),
        in_specs=[pl.BlockSpec((rows, d), lambda l, j: (0, 0)),
                  pl.BlockSpec((1, d, tn), lambda l, j: (l, 0, j)),
                  pl.BlockSpec((1, 1, tn), lambda l, j: (l, 0, j))],
        out_specs=pl.BlockSpec((1, rows, tn), lambda l, j: (l, 0, j)),
        out_shape=jax.ShapeDtypeStruct((depth, rows, n_out), F32),
        compiler_params=_params(("arbitrary", "arbitrary")),
        name="ada_modulation",
    )(cond, w_ada, b_ada.reshape(depth, 1, n_out))


def _rope(x, cos, sin, lane_lo):
    rot = jnp.where(lane_lo, pltpu.roll(x, LANES - 16, 1), pltpu.roll(x, 16, 1))
    return x * cos + rot * sin


def _inproj_kernel(x_ref, xp_ref, xn_ref, sh_ref, sc_ref, gn_ref, w_ref, cw_ref, ga_ref, cos_ref, sin_ref,
                   ya_ref, qn_ref, qs_ref, kn_ref, vn_ref, ks_ref, vs_ref, u_s, *, tm, tiles_per_seq, rope):
    i = pl.program_id(0)
    g = gn_ref[...]
    sh = sh_ref[0]
    sc = sc_ref[0]

    def normmod(x):
        return _rms(x, g) * (1.0 + sc) + sh

    h = normmod(x_ref[...]).astype(BF16)
    p = _dot(h, w_ref[...])

    xh = jnp.concatenate([xp_ref[...], xn_ref[...]], axis=0)
    ph = _dot(normmod(xh).astype(BF16), w_ref[:, :OFF_NQ])
    uh = ph[:, OFF_C:OFF_C + CONV_CH] * ph[:, OFF_H:OFF_H + CONV_CH]
    pos = i % tiles_per_seq
    u = p[:, OFF_C:OFF_C + CONV_CH] * p[:, OFF_H:OFF_H + CONV_CH]
    u_s[0:SUBLANES] = jnp.where(pos == 0, 0.0, uh[0:SUBLANES])
    u_s[SUBLANES:SUBLANES + tm] = u
    u_s[SUBLANES + tm:2 * SUBLANES + tm] = jnp.where(pos == tiles_per_seq - 1, 0.0, uh[SUBLANES:])
    cw = cw_ref[...]
    conv = (u_s[SUBLANES - 1:SUBLANES - 1 + tm] * cw[0:1] + u * cw[1:2]
            + u_s[SUBLANES + 1:SUBLANES + 1 + tm] * cw[2:3])
    ya = p[:, OFF_B:OFF_B + CONV_CH] * conv
    ya_ref[...] = _rms(ya, ga_ref[...]).astype(BF16)

    qn_ref[...] = (p[:, OFF_NQ:OFF_NQ + NA_DIM] * Q_SCALE).astype(BF16)
    kn_ref[...] = p[:, OFF_NK:OFF_NK + NA_DIM].astype(BF16)
    vn_ref[...] = p[:, OFF_NV:OFF_NV + NA_DIM].astype(BF16)
    vs_ref[...] = p[:, OFF_SV:OFF_SV + SW_KV_DIM].astype(BF16)
    if rope:
        cos = cos_ref[...]
        sin = sin_ref[...]
        lane_lo = (lax.broadcasted_iota(jnp.int32, (tm, LANES), 1) % 32) < 16
        for j in range(SW_DIM // LANES):
            q = p[:, OFF_SQ + j * LANES:OFF_SQ + (j + 1) * LANES]
            qs_ref[:, j * LANES:(j + 1) * LANES] = (_rope(q, cos, sin, lane_lo) * Q_SCALE).astype(BF16)
        ks_ref[...] = _rope(p[:, OFF_SK:OFF_SK + SW_KV_DIM], cos, sin, lane_lo).astype(BF16)
    else:
        qs_ref[...] = (p[:, OFF_SQ:OFF_SQ + SW_DIM] * Q_SCALE).astype(BF16)
        ks_ref[...] = p[:, OFF_SK:OFF_SK + SW_KV_DIM].astype(BF16)


def _inproj(x2d, mods, group_of_tile, g_norm, w, conv_w, g_a, cos_t, sin_t, *, seq_len, tm, rope):
    t, d = x2d.shape
    tiles_per_seq = seq_len // tm
    blk8 = tm // SUBLANES
    last8 = t // SUBLANES - 1
    kern = functools.partial(_inproj_kernel, tm=tm, tiles_per_seq=tiles_per_seq, rope=rope)
    tok = lambda width: pl.BlockSpec((tm, width), lambda i: (i, 0))
    out_widths = (CONV_CH, NA_DIM, SW_DIM, NA_DIM, NA_DIM, SW_KV_DIM, SW_KV_DIM)
    return pl.pallas_call(
        kern,
        grid=(t // tm,),
        in_specs=[tok(d),
                  pl.BlockSpec((SUBLANES, d), lambda i: (jnp.maximum(i * blk8 - 1, 0), 0)),
                  pl.BlockSpec((SUBLANES, d), lambda i: (jnp.minimum((i + 1) * blk8, last8), 0)),
                  pl.BlockSpec((1, 1, d), lambda i: (group_of_tile(i), 0, 0)),
                  pl.BlockSpec((1, 1, d), lambda i: (group_of_tile(i), 0, 1)),
                  _const_spec((1, d)),
                  _resident_spec(w.shape),
                  _const_spec(conv_w.shape),
                  _const_spec((1, CONV_CH)),
                  pl.BlockSpec((tm, LANES), lambda i: (i % tiles_per_seq, 0)),
                  pl.BlockSpec((tm, LANES), lambda i: (i % tiles_per_seq, 0))],
        out_specs=[tok(wd) for wd in out_widths],
        out_shape=[jax.ShapeDtypeStruct((t, wd), BF16) for wd in out_widths],
        scratch_shapes=[pltpu.VMEM((tm + 2 * SUBLANES, CONV_CH), F32)],
        compiler_params=_params(("arbitrary",)),
        name="in_projection",
    )(x2d, x2d, x2d, mods, mods, g_norm, w, conv_w, g_a, cos_t, sin_t)


def _masked_attention(qm, k_loc, v_loc, add_loc, k_ctx, v_ctx, sink):
    s_ctx = _dot_nt(qm, k_ctx)
    m = jnp.max(s_ctx, axis=-1, keepdims=True)
    if k_loc is not None:
        s_loc = _dot_nt(qm, k_loc) + add_loc
        m = jnp.maximum(m, jnp.max(s_loc, axis=-1, keepdims=True))
    if sink is not None:
        m = jnp.maximum(m, sink)
    p_ctx = jnp.exp2(s_ctx - m)
    den = jnp.sum(p_ctx, axis=-1, keepdims=True)
    o = _dot(p_ctx.astype(BF16), v_ctx)
    if k_loc is not None:
        p_loc = jnp.exp2(s_loc - m)
        den = den + jnp.sum(p_loc, axis=-1, keepdims=True)
        o = o + _dot(p_loc.astype(BF16), v_loc)
    if sink is not None:
        den = den + jnp.exp2(sink - m)
    return o / den


def _pair_attention(q, k_loc, v_loc, add_loc, k_ctx, v_ctx, sinks, lane_first):
    m = q.shape[0]
    zero = jnp.zeros_like(q)
    q2 = jnp.concatenate([jnp.where(lane_first, q, zero), jnp.where(lane_first, zero, q)], axis=0)
    sink2 = None
    if sinks is not None:
        second = lax.broadcasted_iota(jnp.int32, (2 * m, 1), 0) >= m
        sink2 = jnp.where(second, sinks[1], sinks[0])
    o = _masked_attention(q2, k_loc, v_loc, add_loc, k_ctx, v_ctx, sink2)
    return jnp.where(lane_first, o[:m], o[m:])


def _lane_first(m):
    return lax.broadcasted_iota(jnp.int32, (m, LANES), 1) < HEAD_DIM


def _na_kernel(*refs, rows):
    first = pl.program_id(1) * NA_STEP_GROUPS
    last_group = rows // NA_ROWS_PER_STEP - 1
    rebuilds = (first <= 1) | (first + NA_STEP_GROUPS - 1 >= last_group)

    @pl.when(rebuilds)
    def _():
        for t in range(NA_STEP_GROUPS):
            _na_group(t, first + t, *refs, rows=rows, may_rebuild=True)

    @pl.when(jnp.logical_not(rebuilds))
    def _():
        for t in range(NA_STEP_GROUPS):
            _na_group(t, first + t, *refs, rows=rows, may_rebuild=False)


def _na_group(t, gidx, q_ref, k_ref, v_ref, kc_ref, vc_ref, tab_ref, g_ref, o_ref, y_s, bias_s, *, rows,
              may_rebuild):
    base = jnp.clip(gidx * NA_ROWS_PER_STEP - NA_WIN_R // 2, 0, rows - NA_KEY_ROWS)
    start = pl.multiple_of(base * GRID_W, GRID_W)
    nk = NA_KEY_ROWS * GRID_W
    mq = NA_ROWS_PER_STEP * GRID_W
    lane_first = _lane_first(mq)
    tile_first = _lane_first(GRID_W)

    def fill_bias(head):
        for qr in range(NA_ROWS_PER_STEP):
            r = gidx * NA_ROWS_PER_STEP + qr
            rs = jnp.clip(r - NA_WIN_R // 2, 0, rows - NA_WIN_R)
            row0 = (head % 2) * mq + qr * GRID_W
            for m in range(NA_KEY_ROWS // 2):
                idx = []
                for kr in (base + 2 * m, base + 2 * m + 1):
                    ok = (kr >= rs) & (kr < rs + NA_WIN_R)
                    idx.append(jnp.where(ok, kr - r + NA_WIN_R - 1, NA_BIAS_MASKED))
                bias_s[head // 2, row0:row0 + GRID_W, m * LANES:(m + 1) * LANES] = jnp.where(
                    tile_first, tab_ref[head, idx[0]], tab_ref[head, idx[1]])

    if may_rebuild:
        @pl.when((gidx <= 1) | (gidx == rows // NA_ROWS_PER_STEP - 1))
        def _():
            for head in range(NA_HEADS):
                fill_bias(head)

    qrows = slice(t * mq, (t + 1) * mq)
    for j in range(NA_DIM // LANES):
        cols = slice(j * LANES, (j + 1) * LANES)
        k_loc = k_ref[pl.ds(start, nk), cols]
        v_loc = v_ref[pl.ds(start, nk), cols]
        y_s[qrows, cols] = _pair_attention(q_ref[qrows, cols], k_loc, v_loc, bias_s[j], kc_ref[:, cols],
                                           vc_ref[:, cols], None, lane_first)
    o_ref[qrows, :] = _rms(y_s[qrows, :], g_ref[...]).astype(BF16)


def _na_bias_table(rpb):
    n_heads, n_dr, n_dc = rpb.shape
    pad = GRID_W - NA_WIN_C
    rp = jnp.pad(rpb.astype(F32), ((0, 0), (0, 0), (pad, pad)))
    toep = jnp.stack([rp[:, :, GRID_W - 1 - qc:2 * GRID_W - 1 - qc] for qc in range(GRID_W)], axis=2)
    qc = np.arange(GRID_W)
    cstart = np.clip(qc - NA_WIN_C // 2, 0, GRID_W - NA_WIN_C)
    col_ok = (qc[None, :] >= cstart[:, None]) & (qc[None, :] < cstart[:, None] + NA_WIN_C)
    tab = jnp.where(col_ok[None, None], toep * LOG2E, NEG_INF)
    masked = jnp.full((n_heads, NA_BIAS_MASKED + 1 - n_dr, GRID_W, GRID_W), NEG_INF, F32)
    tab = jnp.concatenate([tab, masked], axis=1)
    return jnp.tile(tab, (1, 1, 1, LANES // GRID_W))


def _na_attention(qn, kn, vn, kn_c, vn_c, tab, g_b, *, bsz, seq_len, ctx_len):
    rows = seq_len // GRID_W
    n_groups = rows // (NA_ROWS_PER_STEP * NA_STEP_GROUPS)
    mq = NA_ROWS_PER_STEP * GRID_W * NA_STEP_GROUPS
    kern = functools.partial(_na_kernel, rows=rows)
    return pl.pallas_call(
        kern,
        grid=(bsz, n_groups),
        in_specs=[pl.BlockSpec((mq, NA_DIM), lambda b, g: (b * n_groups + g, 0)),
                  pl.BlockSpec((seq_len, NA_DIM), lambda b, g: (b, 0)),
                  pl.BlockSpec((seq_len, NA_DIM), lambda b, g: (b, 0)),
                  pl.BlockSpec((ctx_len, NA_DIM), lambda b, g: (b, 0)),
                  pl.BlockSpec((ctx_len, NA_DIM), lambda b, g: (b, 0)),
                  _resident_spec(tab.shape),
                  _const_spec((1, NA_DIM))],
        out_specs=pl.BlockSpec((mq, NA_DIM), lambda b, g: (b * n_groups + g, 0)),
        out_shape=jax.ShapeDtypeStruct(qn.shape, BF16),
        scratch_shapes=[pltpu.VMEM((mq, NA_DIM), F32),
                        pltpu.VMEM((NA_HEADS // 2, 2 * NA_ROWS_PER_STEP * GRID_W, NA_KEY_ROWS * GRID_W), F32)],
        compiler_params=_params(("arbitrary", "arbitrary")),
        name="na_attention",
    )(qn, kn, vn, kn_c, vn_c, tab, g_b)


def _sw_kernel(*refs, seq_len):
    for t in range(SW_STEP_BLOCKS):
        _sw_block(t, pl.program_id(1) * SW_STEP_BLOCKS + t, *refs, seq_len=seq_len)


def _sw_block(t, i, sink_ref, q_ref, k_ref, v_ref, kc_ref, vc_ref, g_ref, o_ref, y_s, *, seq_len):
    s0 = i * SW_QBLOCK
    kstart = pl.multiple_of(jnp.clip(s0 - SW_RADIUS, 0, seq_len - SW_SPAN), SW_RADIUS)
    k_loc = k_ref[pl.ds(kstart, SW_SPAN), :]
    v_loc = v_ref[pl.ds(kstart, SW_SPAN), :]
    qpos = s0 + lax.broadcasted_iota(jnp.int32, (SW_QBLOCK, SW_SPAN), 0)
    kpos = kstart + lax.broadcasted_iota(jnp.int32, (SW_QBLOCK, SW_SPAN), 1)
    band = jnp.where(jnp.abs(kpos - qpos) <= SW_RADIUS, 0.0, NEG_INF).astype(F32)
    band2 = jnp.concatenate([band, band], axis=0)
    lane_first = _lane_first(SW_QBLOCK)
    pairs = SW_HEADS // SW_KV_HEADS
    qrows = slice(t * SW_QBLOCK, (t + 1) * SW_QBLOCK)
    for j in range(pairs):
        cols = slice(j * LANES, (j + 1) * LANES)
        sinks = (sink_ref[j], sink_ref[j + pairs])
        y_s[qrows, cols] = _pair_attention(q_ref[qrows, cols], k_loc, v_loc, band2, kc_ref[...], vc_ref[...],
                                           sinks, lane_first)
    o_ref[qrows, :] = _rms(y_s[qrows, :], g_ref[...]).astype(BF16)


def _sw_attention(sink, qs, ks, vs, ks_c, vs_c, g_c, *, bsz, seq_len, ctx_len):
    step_tokens = SW_QBLOCK * SW_STEP_BLOCKS
    nblk = seq_len // step_tokens
    kern = functools.partial(_sw_kernel, seq_len=seq_len)
    return pl.pallas_call(
        kern,
        grid=(bsz, nblk),
        in_specs=[pl.BlockSpec(memory_space=pltpu.SMEM),
                  pl.BlockSpec((step_tokens, SW_DIM), lambda b, i: (b * nblk + i, 0)),
                  pl.BlockSpec((seq_len, SW_KV_DIM), lambda b, i: (b, 0)),
                  pl.BlockSpec((seq_len, SW_KV_DIM), lambda b, i: (b, 0)),
                  pl.BlockSpec((ctx_len, SW_KV_DIM), lambda b, i: (b, 0)),
                  pl.BlockSpec((ctx_len, SW_KV_DIM), lambda b, i: (b, 0)),
                  _const_spec((1, SW_DIM))],
        out_specs=pl.BlockSpec((step_tokens, SW_DIM), lambda b, i: (b * nblk + i, 0)),
        out_shape=jax.ShapeDtypeStruct(qs.shape, BF16),
        scratch_shapes=[pltpu.VMEM((step_tokens, SW_DIM), F32)],
        compiler_params=_params(("arbitrary", "arbitrary")),
        name="sw_attention",
    )(sink, qs, ks, vs, ks_c, vs_c, g_c)


def _ctx_attn_kernel(sink_ref, qn_ref, kn_ref, vn_ref, qs_ref, ks_ref, vs_ref, gb_ref, gc_ref,
                     yb_ref, yc_ref, y_s, *, ctx_len):
    lane_first = _lane_first(ctx_len)
    for j in range(NA_DIM // LANES):
        cols = slice(j * LANES, (j + 1) * LANES)
        y_s[:, cols] = _pair_attention(qn_ref[:, cols], None, None, None, kn_ref[:, cols], vn_ref[:, cols],
                                       None, lane_first)
    yb_ref[...] = _rms(y_s[...], gb_ref[...]).astype(BF16)
    pairs = SW_HEADS // SW_KV_HEADS
    for j in range(pairs):
        cols = slice(j * LANES, (j + 1) * LANES)
        sinks = (sink_ref[j], sink_ref[j + pairs])
        y_s[:, cols] = _pair_attention(qs_ref[:, cols], None, None, None, ks_ref[...], vs_ref[...],
                                       sinks, lane_first)
    yc_ref[...] = _rms(y_s[...], gc_ref[...]).astype(BF16)


def _ctx_attention(sink, qn, kn, vn, qs, ks, vs, g_b, g_c, *, bsz, ctx_len):
    kern = functools.partial(_ctx_attn_kernel, ctx_len=ctx_len)
    tok = lambda width: pl.BlockSpec((ctx_len, width), lambda b: (b, 0))
    return pl.pallas_call(
        kern,
        grid=(bsz,),
        in_specs=[pl.BlockSpec(memory_space=pltpu.SMEM),
                  tok(NA_DIM), tok(NA_DIM), tok(NA_DIM), tok(SW_DIM), tok(SW_KV_DIM), tok(SW_KV_DIM),
                  _const_spec((1, NA_DIM)), _const_spec((1, SW_DIM))],
        out_specs=[tok(NA_DIM), tok(SW_DIM)],
        out_shape=[jax.ShapeDtypeStruct(qn.shape, BF16), jax.ShapeDtypeStruct(qs.shape, BF16)],
        scratch_shapes=[pltpu.VMEM((ctx_len, NA_DIM), F32)],
        compiler_params=_params(("arbitrary",)),
        name="ctx_attention",
    )(sink, qn, kn, vn, qs, ks, vs, g_b, g_c)


def _mix_residual(ya_ref, yb_ref, yc_ref, x_ref, g1_ref, sh_ref, sc_ref, gn_ref, wa_ref, wb_ref, wc_ref):
    mix = _dot(ya_ref[...], wa_ref[...]) + _dot(yb_ref[...], wb_ref[...]) + _dot(yc_ref[...], wc_ref[...])
    xl = x_ref[...] + g1_ref[0] * mix
    h = _rms(xl, gn_ref[...]) * (1.0 + sc_ref[0]) + sh_ref[0]
    return xl, h


def _mix_specs(x2d, mods, group_of_tile, g_norm, ys, ws, tm):
    d = x2d.shape[1]
    tok = lambda width: pl.BlockSpec((tm, width), lambda i: (i, 0))
    mod = lambda j: pl.BlockSpec((1, 1, d), lambda i: (group_of_tile(i), 0, j))
    in_specs = ([tok(y.shape[1]) for y in ys] + [tok(d), mod(2), mod(3), mod(4), _const_spec((1, d))]
                + [_resident_spec(w.shape) for w in ws])
    return in_specs, [*ys, x2d, mods, mods, mods, g_norm, *ws]


def _outproj_router_kernel(*refs):
    *mix_refs, r_ref, xo_ref, h_ref, lg_ref = refs
    xl, h = _mix_residual(*mix_refs)
    xo_ref[...] = xl
    hi = h.astype(BF16)
    h_ref[...] = hi
    tm = h.shape[0]
    lo = (h - hi.astype(F32)).astype(BF16)
    r = _dot(jnp.concatenate([hi, lo], axis=0), r_ref[...])
    lg_ref[...] = (r[:tm, :LOGIT_LANES] + r[:tm, LOGIT_LANES:]) + (r[tm:, :LOGIT_LANES] + r[tm:, LOGIT_LANES:])


def _outproj_router(ys, x2d, mods, group_of_tile, g_norm, ws, router_w, *, tm):
    t, d = x2d.shape
    in_specs, args = _mix_specs(x2d, mods, group_of_tile, g_norm, ys, ws, tm)
    tok = lambda width: pl.BlockSpec((tm, width), lambda i: (i, 0))
    return pl.pallas_call(
        _outproj_router_kernel,
        grid=(t // tm,),
        in_specs=in_specs + [_resident_spec(router_w.shape)],
        out_specs=[tok(d), tok(d), tok(LOGIT_LANES)],
        out_shape=[jax.ShapeDtypeStruct((t, d), F32), jax.ShapeDtypeStruct((t, d), BF16),
                   jax.ShapeDtypeStruct((t, LOGIT_LANES), F32)],
        compiler_params=_params(("arbitrary",)),
        name="out_projection_router",
    )(*args, router_w)


def _outproj_ffn_kernel(*refs, chunks):
    *mix_refs, g2_ref, wg_ref, wu_ref, wd_ref, o_ref = refs
    xl, h = _mix_residual(*mix_refs)
    h = h.astype(BF16)
    acc = None
    for lo, hi in chunks:
        a = _silu(_dot(h, wg_ref[:, lo:hi].astype(BF16))) * _dot(h, wu_ref[:, lo:hi].astype(BF16))
        part = _dot(a.astype(BF16), wd_ref[lo:hi, :].astype(BF16))
        acc = part if acc is None else acc + part
    o_ref[...] = xl + g2_ref[0] * acc


def _outproj_ffn(ys, x2d, mods, group_of_tile, g_norm, ws, wg, wu, wd, *, tm):
    t, d = x2d.shape
    d_ff = wg.shape[1]
    step = 1024
    chunks = tuple((lo, min(lo + step, d_ff)) for lo in range(0, d_ff, step))
    in_specs, args = _mix_specs(x2d, mods, group_of_tile, g_norm, ys, ws, tm)
    return pl.pallas_call(
        functools.partial(_outproj_ffn_kernel, chunks=chunks),
        grid=(t // tm,),
        in_specs=in_specs + [pl.BlockSpec((1, 1, d), lambda i: (group_of_tile(i), 0, 5)),
                             _resident_spec(wg.shape), _resident_spec(wu.shape), _resident_spec(wd.shape)],
        out_specs=pl.BlockSpec((tm, d), lambda i: (i, 0)),
        out_shape=jax.ShapeDtypeStruct((t, d), F32),
        compiler_params=_params(("arbitrary",)),
        name="out_projection_ffn",
    )(*args, mods, wg, wu, wd)


def _route_kernel(*refs):
    for j in range(MOE_STEP_CHUNKS):
        _route_chunk(j, *refs)


def _route_chunk(j, lg_ref, h_ref, hl_ref, pos_ref, gsel_ref, meta_ref):
    toks = slice(j * MOE_CHUNK, (j + 1) * MOE_CHUNK)
    logits = lg_ref[toks, :]
    lane = lax.broadcasted_iota(jnp.int32, logits.shape, 1)
    lg = jnp.where(lane < N_EXPERTS, logits, -jnp.inf)
    v1 = jnp.max(lg, axis=-1, keepdims=True)
    i1 = jnp.min(jnp.where(lg == v1, lane, LOGIT_LANES), axis=-1, keepdims=True)
    lg2 = jnp.where(lane == i1, -jnp.inf, lg)
    v2 = jnp.max(lg2, axis=-1, keepdims=True)
    i2 = jnp.min(jnp.where(lg2 == v2, lane, LOGIT_LANES), axis=-1, keepdims=True)
    e2 = jnp.exp(v2 - v1)
    den = 1.0 + e2
    g_first = 1.0 / den
    g_second = e2 / den
    sel1 = lane == i1
    sel2 = lane == i2
    onehot = jnp.where(sel1, 1.0, jnp.where(sel2, 1.0, 0.0))

    r_i = lax.broadcasted_iota(jnp.int32, (MOE_CHUNK, MOE_CHUNK), 0)
    c_i = lax.broadcasted_iota(jnp.int32, (MOE_CHUNK, MOE_CHUNK), 1)
    earlier = jnp.where(c_i < r_i, 1.0, 0.0).astype(BF16)
    within = _dot(earlier, onehot.astype(BF16))
    count = jnp.sum(onehot, axis=0, keepdims=True)
    n_gran = jnp.floor((count + (MOE_GRANULE - 1)) * (1.0 / MOE_GRANULE))
    u_r = lax.broadcasted_iota(jnp.int32, (LOGIT_LANES, LOGIT_LANES), 0)
    u_c = lax.broadcasted_iota(jnp.int32, (LOGIT_LANES, LOGIT_LANES), 1)
    before = jnp.where(u_r < u_c, 1.0, 0.0).astype(BF16)
    n_gran8 = jnp.broadcast_to(n_gran, (SUBLANES, LOGIT_LANES))
    off_gran = _dot(n_gran8.astype(BF16), before)[0:1]
    lpos = off_gran * MOE_GRANULE + within
    p_first = jnp.sum(jnp.where(sel1, lpos, 0.0), axis=-1, keepdims=True)
    p_second = jnp.sum(jnp.where(sel2, lpos, 0.0), axis=-1, keepdims=True)
    pos = jnp.where(lane == 0, p_first, jnp.where(lane == 1, p_second, -1.0))
    pos_ref[toks, :] = pos
    gsel_ref[toks, :] = jnp.where(lane == 0, g_first, jnp.where(lane == 1, g_second, 0.0))
    rows8 = lax.broadcasted_iota(jnp.int32, (SUBLANES, LOGIT_LANES), 0)
    meta_ref[j] = jnp.where(rows8 == 0, off_gran, jnp.where(rows8 == 1, n_gran, 0.0)).astype(jnp.int32)

    pos_t = pos.T
    row_id = lax.broadcasted_iota(jnp.int32, (MOE_LOCAL_ROWS, MOE_CHUNK), 0).astype(F32)
    pick = jnp.where(row_id == pos_t[0:1], 1.0, jnp.where(row_id == pos_t[1:2], 1.0, 0.0)).astype(BF16)
    hl_ref[j] = _dot(pick, h_ref[toks, :]).astype(BF16)


def _route(logits, h2):
    t, d = h2.shape
    n_chunks = t // MOE_CHUNK
    per_step = MOE_STEP_CHUNKS
    tok = lambda width: pl.BlockSpec((per_step * MOE_CHUNK, width), lambda i: (i, 0))
    return pl.pallas_call(
        _route_kernel,
        grid=(n_chunks // per_step,),
        in_specs=[tok(LOGIT_LANES), tok(d)],
        out_specs=[pl.BlockSpec((per_step, MOE_LOCAL_ROWS, d), lambda i: (i, 0, 0)),
                   tok(LOGIT_LANES), tok(LOGIT_LANES),
                   pl.BlockSpec((per_step, SUBLANES, LOGIT_LANES), lambda i: (i, 0, 0))],
        out_shape=[jax.ShapeDtypeStruct((n_chunks, MOE_LOCAL_ROWS, d), BF16),
                   jax.ShapeDtypeStruct((t, LOGIT_LANES), F32),
                   jax.ShapeDtypeStruct((t, LOGIT_LANES), F32),
                   jax.ShapeDtypeStruct((n_chunks, SUBLANES, LOGIT_LANES), jnp.int32)],
        compiler_params=_params(("arbitrary",)),
        name="moe_route",
    )(logits, h2)


def _experts_kernel(meta_ref, hl_hbm, wg_ref, wu_ref, wd_ref, yl_hbm, he_s, ye_s, yb_s, zero_s,
                    gather_sem, store_sem, fill_sem, *, n_chunks):
    ws = pl.program_id(0)
    e = pl.program_id(1)
    f = pl.program_id(2)
    n_e = pl.num_programs(1)
    n_f = pl.num_programs(2)
    step = ws * n_e + e
    last_step = pl.num_programs(0) * n_e - 1

    def meta(ws_, e_, c, k):
        return meta_ref[((ws_ * n_chunks + c) * 2 + k) * N_EXPERTS + e_]

    def rows_of(granules):
        return pl.multiple_of(granules * MOE_GRANULE, MOE_GRANULE)

    def for_each_run(ws_, e_, fn):
        dst = 0
        for c in range(n_chunks):
            src, n = meta(ws_, e_, c, 0), meta(ws_, e_, c, 1)

            @pl.when(n > 0)
            def _(c=c, src=src, n=n, dst=dst):
                fn(ws_ * n_chunks + c, rows_of(src), rows_of(dst), rows_of(n))
            dst = dst + n
        return dst * MOE_GRANULE

    def gather_copy(slot):
        return lambda chunk, src, dst, rows: pltpu.make_async_copy(
            hl_hbm.at[chunk, pl.ds(src, rows)], he_s.at[slot, pl.ds(dst, rows)], gather_sem.at[slot])

    def store_copy(chunk, src, dst, rows):
        return pltpu.make_async_copy(yb_s.at[pl.ds(dst, rows)], yl_hbm.at[chunk, pl.ds(src, rows)], store_sem)

    def fill_tails(start):
        for c in range(n_chunks):
            used = meta(ws, N_EXPERTS - 1, c, 0) + meta(ws, N_EXPERTS - 1, c, 1)
            tail = MOE_LOCAL_ROWS // MOE_GRANULE - used

            @pl.when(tail > 0)
            def _(c=c, used=used, tail=tail):
                cp = pltpu.make_async_copy(zero_s.at[pl.ds(0, rows_of(tail))],
                                           yl_hbm.at[ws * n_chunks + c, pl.ds(rows_of(used), rows_of(tail))],
                                           fill_sem)
                cp.start() if start else cp.wait()

    slot = step % 2

    @pl.when(f == 0)
    def _():
        @pl.when(step == 0)
        def _():
            he_s[...] = jnp.zeros_like(he_s)
            zero_s[...] = jnp.zeros_like(zero_s)
            for_each_run(ws, e, lambda *run: gather_copy(slot)(*run).start())

        @pl.when(e == 0)
        def _():
            fill_tails(True)
            fill_tails(False)

        for_each_run(ws, e, lambda *run: gather_copy(slot)(*run).wait())

        @pl.when(step < last_step)
        def _():
            nxt = step + 1
            for_each_run(nxt // n_e, nxt % n_e, lambda *run: gather_copy(1 - slot)(*run).start())

    def m_tile(start, size, mode):
        rows = pl.ds(pl.multiple_of(start, MOE_M_MIN), size)
        hrow = he_s[slot, rows, :]
        a = _silu(_dot(hrow, wg_ref[0, 0])) * _dot(hrow, wu_ref[0, 0])
        part = _dot(a.astype(BF16), wd_ref[0, 0])
        if mode == "first":
            ye_s[rows, :] = part
        elif mode == "middle":
            ye_s[rows, :] += part
        else:
            yb_s[rows, :] = (ye_s[rows, :] + part).astype(BF16)

    n_rows = sum(meta(ws, e, c, 1) for c in range(n_chunks)) * MOE_GRANULE
    n_full = n_rows // MOE_M_TILE
    rem = n_rows - n_full * MOE_M_TILE

    def all_tiles(mode):
        def tile_pair(m, carry):
            m_tile(2 * m * MOE_M_TILE, MOE_M_TILE, mode)
            m_tile((2 * m + 1) * MOE_M_TILE, MOE_M_TILE, mode)
            return carry
        lax.fori_loop(0, n_full // 2, tile_pair, 0)

        @pl.when(n_full % 2 == 1)
        def _():
            m_tile((n_full - 1) * MOE_M_TILE, MOE_M_TILE, mode)

        size = MOE_M_TILE
        while size >= MOE_M_MIN:
            lo = 0 if size == MOE_M_MIN else size // 2

            @pl.when((rem > lo) & (rem <= size))
            def _(size=size):
                m_tile(n_full * MOE_M_TILE, size, mode)
            size //= 2

    @pl.when(f == 0)
    def _():
        all_tiles("first")

    @pl.when((f > 0) & (f < n_f - 1))
    def _():
        all_tiles("middle")

    @pl.when(f == n_f - 1)
    def _():
        @pl.when(step > 0)
        def _():
            prev = step - 1
            for_each_run(prev // n_e, prev % n_e, lambda *run: store_copy(*run).wait())

        all_tiles("last")
        for_each_run(ws, e, lambda *run: store_copy(*run).start())

        @pl.when(step == last_step)
        def _():
            for_each_run(ws, e, lambda *run: store_copy(*run).wait())


def _expert_weight_tiles(wg, wu, wd):
    n_exp, d, d_ff = wg.shape
    tf = MOE_FF_TILE
    n_f = d_ff // tf
    cols = lambda w: jnp.transpose(w.reshape(n_exp, d, n_f, tf), (0, 2, 1, 3)).astype(BF16)
    return cols(wg), cols(wu), wd.reshape(n_exp, n_f, tf, d).astype(BF16)


def _experts(meta, hl, wg, wu, wd, *, n_chunks):
    total_chunks, local_rows, d = hl.shape
    n_exp, n_f, _, tf = wg.shape
    assert n_f >= 2, "the first and the last hidden tile must be different grid steps"
    max_rows = n_chunks * MOE_CHUNK
    grid_spec = pltpu.PrefetchScalarGridSpec(
        num_scalar_prefetch=1,
        grid=(total_chunks // n_chunks, n_exp, n_f),
        in_specs=[pl.BlockSpec(memory_space=pl.ANY),
                  pl.BlockSpec((1, 1, d, tf), lambda i, e, f, m: (e, f, 0, 0)),
                  pl.BlockSpec((1, 1, d, tf), lambda i, e, f, m: (e, f, 0, 0)),
                  pl.BlockSpec((1, 1, tf, d), lambda i, e, f, m: (e, f, 0, 0))],
        out_specs=pl.BlockSpec(memory_space=pl.ANY),
        scratch_shapes=[pltpu.VMEM((2, max_rows, d), BF16), pltpu.VMEM((max_rows, d), F32),
                        pltpu.VMEM((max_rows, d), BF16), pltpu.VMEM((local_rows, d), BF16),
                        pltpu.SemaphoreType.DMA((2,)), pltpu.SemaphoreType.DMA(()), pltpu.SemaphoreType.DMA(())])
    return pl.pallas_call(
        functools.partial(_experts_kernel, n_chunks=n_chunks),
        grid_spec=grid_spec,
        out_shape=jax.ShapeDtypeStruct(hl.shape, BF16),
        compiler_params=_params(("arbitrary", "arbitrary", "arbitrary")),
        name="moe_experts",
    )(meta, hl, wg, wu, wd)


def _scatter_kernel(yl_ref, pos_ref, gsel_ref, x_ref, g2_ref, gf_ref, o_ref):
    row_id = lax.broadcasted_iota(jnp.int32, (MOE_CHUNK, MOE_LOCAL_ROWS), 1).astype(F32)
    for j in range(MOE_STEP_CHUNKS):
        toks = slice(j * MOE_CHUNK, (j + 1) * MOE_CHUNK)
        pos = pos_ref[toks, :]
        gsel = gsel_ref[toks, :]
        yl = yl_ref[j]
        y = None
        for k in range(2):
            pick = jnp.where(pos[:, k:k + 1] == row_id, 1.0, 0.0).astype(BF16)
            term = gsel[:, k:k + 1] * _dot(pick, yl)
            y = term if y is None else y + term
        o_ref[toks, :] = _rms(x_ref[toks, :] + g2_ref[0] * y, gf_ref[...])


def _scatter_final(yl, pos, gsel, x2d, mods, group_of_step, g_final):
    t, d = x2d.shape
    per_step = MOE_STEP_CHUNKS
    tok = lambda width: pl.BlockSpec((per_step * MOE_CHUNK, width), lambda i: (i, 0))
    return pl.pallas_call(
        _scatter_kernel,
        grid=(t // (per_step * MOE_CHUNK),),
        in_specs=[pl.BlockSpec((per_step, MOE_LOCAL_ROWS, d), lambda i: (i, 0, 0)),
                  tok(LOGIT_LANES), tok(LOGIT_LANES), tok(d),
                  pl.BlockSpec((1, 1, d), lambda i: (group_of_step(i), 0, 5)), _const_spec((1, d))],
        out_specs=tok(d),
        out_shape=jax.ShapeDtypeStruct((t, d), F32),
        compiler_params=_params(("arbitrary",)),
        name="moe_scatter_final",
    )(yl, pos, gsel, x2d, mods, g_final)


def _rope_tables(n):
    t = jnp.arange(n)
    row = (t // GRID_W).astype(F32)
    col = (t % GRID_W).astype(F32)
    n_freq = HEAD_DIM // 4
    inv = ROPE_THETA ** (-jnp.arange(n_freq, dtype=F32) / n_freq)
    ang = jnp.stack([row[:, None] * inv, col[:, None] * inv], axis=1)
    cos = jnp.cos(ang)[:, :, None, :]
    sin = jnp.sin(ang)[:, :, None, :]
    cos_f = jnp.broadcast_to(cos, (n, 2, 2, n_freq)).reshape(n, HEAD_DIM)
    sin_s = jnp.concatenate([-sin, sin], axis=2).reshape(n, HEAD_DIM)
    reps = LANES // HEAD_DIM
    return jnp.tile(cos_f, (1, reps)), jnp.tile(sin_s, (1, reps))


def _pair_perm():
    pairs = SW_HEADS // SW_KV_HEADS
    order = [h for j in range(pairs) for h in (j, j + pairs)]
    return np.concatenate([np.arange(h * HEAD_DIM, (h + 1) * HEAD_DIM) for h in order])


def kernel(x, c, ctx, c_ctx, w_ada, b_ada, g_norm1, g_norm2, w_in, conv_w, na_rpb, sw_sink, g_mix, w_out,
           ffn_w_gate, ffn_w_up, ffn_w_down, w_router, moe_w_gate, moe_w_up, moe_w_down, g_final):
    bsz, n, d = x.shape
    ctx_len = ctx.shape[1]
    depth = w_in.shape[0]
    assert w_in.shape[2] == D_IN and n % GRID_W == 0
    assert depth == 2, "layer 0 dense on both streams, layer 1 routed experts on the latent stream only"
    tm = 512
    tm_in = 1024
    tm_ctx = ctx_len
    set_chunks = MOE_SET_CHUNKS
    assert n % tm == 0 and n % tm_in == 0 and (n // GRID_W) % (NA_ROWS_PER_STEP * NA_STEP_GROUPS) == 0
    assert n % (SW_QBLOCK * SW_STEP_BLOCKS) == 0 and tm % MOE_CHUNK == 0
    assert n % (MOE_CHUNK * MOE_STEP_CHUNKS) == 0 and (bsz * n) % (MOE_CHUNK * set_chunks) == 0
    assert (MOE_CHUNK * set_chunks) % MOE_M_TILE == 0

    cond_rows = 2 * SUBLANES
    cond = jnp.zeros((cond_rows, d), F32).at[:bsz].set(c).at[bsz].set(c_ctx)
    mods_all = _ada(cond, w_ada, b_ada).reshape(depth, cond_rows, 1, 6 * d)

    lat_group = lambda i: i // (n // tm)
    ctx_group = lambda i: bsz
    cos_t, sin_t = _rope_tables(n)
    perm = _pair_perm()

    xl = x.reshape(bsz * n, d)
    xc = ctx.reshape(bsz * ctx_len, d)
    out = None
    for l in range(depth):
        last = l == depth - 1
        mods = mods_all[l]
        w_l = w_in[l]
        w_l = jnp.concatenate([w_l[:, :OFF_SQ], w_l[:, OFF_SQ:OFF_NK][:, perm], w_l[:, OFF_NK:]], axis=1).astype(BF16)
        g_a = g_mix[l, :CONV_CH].reshape(1, CONV_CH)
        g_b = g_mix[l, CONV_CH:CONV_CH + NA_DIM].reshape(1, NA_DIM)
        g_c = g_mix[l, CONV_CH + NA_DIM:][perm].reshape(1, SW_DIM)
        wo = w_out[l]
        wa = wo[:CONV_CH].astype(BF16)
        wb = wo[CONV_CH:CONV_CH + NA_DIM].astype(BF16)
        wc = wo[CONV_CH + NA_DIM:][perm].astype(BF16)
        gn1 = g_norm1[l].reshape(1, d)
        gn2 = g_norm2[l].reshape(1, d)
        sink = sw_sink[l].astype(F32) * LOG2E
        tab = _na_bias_table(na_rpb[l])

        ya, qn, qs, kn, vn, ks, vs = _inproj(xl, mods, lambda i: i // (n // tm_in), gn1, w_l, conv_w[l], g_a,
                                             cos_t, sin_t, seq_len=n, tm=tm_in, rope=True)
        cya, cqn, cqs, ckn, cvn, cks, cvs = _inproj(xc, mods, ctx_group, gn1, w_l, conv_w[l], g_a, cos_t, sin_t,
                                                    seq_len=ctx_len, tm=tm_ctx, rope=False)
        yb = _na_attention(qn, kn, vn, ckn, cvn, tab, g_b, bsz=bsz, seq_len=n, ctx_len=ctx_len)
        yc = _sw_attention(sink, qs, ks, vs, cks, cvs, g_c, bsz=bsz, seq_len=n, ctx_len=ctx_len)

        if l % 2 == 0:
            i = l // 2
            wg, wu, wd = ffn_w_gate[i], ffn_w_up[i], ffn_w_down[i]
            xl = _outproj_ffn((ya, yb, yc), xl, mods, lat_group, gn2, (wa, wb, wc), wg, wu, wd, tm=tm)
            if not last:
                cyb, cyc = _ctx_attention(sink, cqn, ckn, cvn, cqs, cks, cvs, g_b, g_c, bsz=bsz, ctx_len=ctx_len)
                xc = _outproj_ffn((cya, cyb, cyc), xc, mods, ctx_group, gn2, (wa, wb, wc), wg, wu, wd,
                                  tm=tm_ctx)
        else:
            i = l // 2
            wr = jnp.zeros((d, LOGIT_LANES), F32).at[:, :N_EXPERTS].set(w_router[i])
            wr_hi = wr.astype(BF16)
            wr_lo = (wr - wr_hi.astype(F32)).astype(BF16)
            wr2 = jnp.concatenate([wr_hi, wr_lo], axis=1)
            wg, wu, wd = _expert_weight_tiles(moe_w_gate[i], moe_w_up[i], moe_w_down[i])
            xl, h2, logits = _outproj_router((ya, yb, yc), xl, mods, lambda i: i // (n // tm_in), gn2,
                                             (wa, wb, wc), wr2, tm=tm_in)
            hl, pos, gsel, meta = _route(logits, h2)
            meta = meta[:, :2, :N_EXPERTS].reshape(-1)
            yl = _experts(meta, hl, wg, wu, wd, n_chunks=set_chunks)
            step_tokens = MOE_CHUNK * MOE_STEP_CHUNKS
            out = _scatter_final(yl, pos, gsel, xl, mods, lambda i: i // (n // step_tokens), g_final.reshape(1, d))
    return out.reshape(bsz, n, d)
```

```python
import functools

import numpy as np
import jax
import jax.numpy as jnp
from jax import lax
from jax.experimental import pallas as pl
from jax.experimental.pallas import tpu as pltpu

F32 = jnp.float32
BF16 = jnp.bfloat16

GRID_W = 64
HEAD_DIM = 64
CONV_CH = 256
NA_HEADS = 6
NA_DIM = NA_HEADS * HEAD_DIM
SW_HEADS = 6
SW_KV_HEADS = 2
SW_DIM = SW_HEADS * HEAD_DIM
SW_KV_DIM = SW_KV_HEADS * HEAD_DIM
NA_WIN_R = 8
NA_WIN_C = 16
SW_RADIUS = 128
ROPE_THETA = 10000.0
N_EXPERTS = 8
EPS = 1e-6
NEG_INF = -1e30
OFF_H, OFF_B, OFF_C = 0, CONV_CH, 2 * CONV_CH
OFF_NQ = 3 * CONV_CH
OFF_SQ = OFF_NQ + NA_DIM
OFF_NK = OFF_SQ + SW_DIM
OFF_NV = OFF_NK + NA_DIM
OFF_SK = OFF_NV + NA_DIM
OFF_SV = OFF_SK + SW_KV_DIM
D_IN = OFF_SV + SW_KV_DIM
LOG2E = 1.4426950408889634
Q_SCALE = HEAD_DIM ** -0.5 * LOG2E

LANES = 128
SUBLANES = 8
VMEM_BYTES_V7X = 64 * 1024 * 1024
VMEM_LIMIT = VMEM_BYTES_V7X * 7 // 8

NA_ROWS_PER_STEP = 4
NA_KEY_ROWS = NA_ROWS_PER_STEP + NA_WIN_R
NA_BIAS_MASKED = 2 * NA_WIN_R - 1
MOE_CHUNK = 256
MOE_GRANULE = 2 * SUBLANES
MOE_LOCAL_ROWS = 640
MOE_SET_CHUNKS = 16
MOE_STEP_CHUNKS = 4
MOE_M_TILE = 512
MOE_M_MIN = 128
MOE_FF_TILE = 512
NA_STEP_GROUPS = 2
SW_QBLOCK = 256
SW_STEP_BLOCKS = 4
SW_SPAN = SW_QBLOCK + 2 * SW_RADIUS
LOGIT_LANES = LANES


def _params(semantics):
    return pltpu.CompilerParams(dimension_semantics=semantics, vmem_limit_bytes=VMEM_LIMIT)


def _const_spec(shape):
    nd = len(shape)
    return pl.BlockSpec(shape, lambda *_: (0,) * nd)


def _resident_spec(shape):
    nd = len(shape)
    return pl.BlockSpec(shape, lambda *_: (0,) * nd, pipeline_mode=pl.Buffered(1))


def _rms(x, g):
    y = x * lax.rsqrt(jnp.mean(x * x, axis=-1, keepdims=True) + EPS)
    return y * g


def _silu(x):
    return x / (1.0 + jnp.exp(-x))


def _dot(a, b):
    return jnp.dot(a, b, preferred_element_type=F32)


def _dot_nt(a, b):
    return lax.dot_general(a, b, (((1,), (1,)), ((), ())), preferred_element_type=F32)


def _ada_kernel(c_ref, w_ref, b_ref, o_ref):
    a = _silu(c_ref[...])
    o_ref[0] = jnp.dot(a, w_ref[0], preferred_element_type=F32,
                       precision=lax.Precision.HIGHEST) + b_ref[0]


def _ada(cond, w_ada, b_ada):
    depth, d, n_out = w_ada.shape
    rows = cond.shape[0]
    tn = n_out // 6
    return pl.pallas_call(
        _ada_kernel,
        grid=(depth, n_out // tn),
        in_specs=[pl.BlockSpec((rows, d), lambda l, j: (0, 0)),
                  pl.BlockSpec((1, d, tn), lambda l, j: (l, 0, j)),
                  pl.BlockSpec((1, 1, tn), lambda l, j: (l, 0, j))],
        out_specs=pl.BlockSpec((1, rows, tn), lambda l, j: (l, 0, j)),
        out_shape=jax.ShapeDtypeStruct((depth, rows, n_out), F32),
        compiler_params=_params(("arbitrary", "arbitrary")),
        name="ada_modulation",
    )(cond, w_ada, b_ada.reshape(depth, 1, n_out))


def _rope(x, cos, sin, lane_lo):
    rot = jnp.where(lane_lo, pltpu.roll(x, LANES - 16, 1), pltpu.roll(x, 16, 1))
    return x * cos + rot * sin


def _inproj_kernel(x_ref, xp_ref, xn_ref, sh_ref, sc_ref, gn_ref, w_ref, cw_ref, ga_ref, cos_ref, sin_ref,
                   ya_ref, qn_ref, qs_ref, kn_ref, vn_ref, ks_ref, vs_ref, u_s, *, tm, tiles_per_seq, rope):
    i = pl.program_id(0)
    g = gn_ref[...]
    sh = sh_ref[0]
    sc = sc_ref[0]

    def normmod(x):
        return _rms(x, g) * (1.0 + sc) + sh

    h = normmod(x_ref[...]).astype(BF16)
    p = _dot(h, w_ref[...])

    xh = jnp.concatenate([xp_ref[...], xn_ref[...]], axis=0)
    ph = _dot(normmod(xh).astype(BF16), w_ref[:, :OFF_NQ])
    uh = ph[:, OFF_C:OFF_C + CONV_CH] * ph[:, OFF_H:OFF_H + CONV_CH]
    pos = i % tiles_per_seq
    u = p[:, OFF_C:OFF_C + CONV_CH] * p[:, OFF_H:OFF_H + CONV_CH]
    u_s[0:SUBLANES] = jnp.where(pos == 0, 0.0, uh[0:SUBLANES])
    u_s[SUBLANES:SUBLANES + tm] = u
    u_s[SUBLANES + tm:2 * SUBLANES + tm] = jnp.where(pos == tiles_per_seq - 1, 0.0, uh[SUBLANES:])
    cw = cw_ref[...]
    conv = (u_s[SUBLANES - 1:SUBLANES - 1 + tm] * cw[0:1] + u * cw[1:2]
            + u_s[SUBLANES + 1:SUBLANES + 1 + tm] * cw[2:3])
    ya = p[:, OFF_B:OFF_B + CONV_CH] * conv
    ya_ref[...] = _rms(ya, ga_ref[...]).astype(BF16)

    qn_ref[...] = (p[:, OFF_NQ:OFF_NQ + NA_DIM] * Q_SCALE).astype(BF16)
    kn_ref[...] = p[:, OFF_NK:OFF_NK + NA_DIM].astype(BF16)
    vn_ref[...] = p[:, OFF_NV:OFF_NV + NA_DIM].astype(BF16)
    vs_ref[...] = p[:, OFF_SV:OFF_SV + SW_KV_DIM].astype(BF16)
    if rope:
        cos = cos_ref[...]
        sin = sin_ref[...]
        lane_lo = (lax.broadcasted_iota(jnp.int32, (tm, LANES), 1) % 32) < 16
        for j in range(SW_DIM // LANES):
            q = p[:, OFF_SQ + j * LANES:OFF_SQ + (j + 1) * LANES]
            qs_ref[:, j * LANES:(j + 1) * LANES] = (_rope(q, cos, sin, lane_lo) * Q_SCALE).astype(BF16)
        ks_ref[...] = _rope(p[:, OFF_SK:OFF_SK + SW_KV_DIM], cos, sin, lane_lo).astype(BF16)
    else:
        qs_ref[...] = (p[:, OFF_SQ:OFF_SQ + SW_DIM] * Q_SCALE).astype(BF16)
        ks_ref[...] = p[:, OFF_SK:OFF_SK + SW_KV_DIM].astype(BF16)


def _inproj(x2d, mods, group_of_tile, g_norm, w, conv_w, g_a, cos_t, sin_t, *, seq_len, tm, rope):
    t, d = x2d.shape
    tiles_per_seq = seq_len // tm
    blk8 = tm // SUBLANES
    last8 = t // SUBLANES - 1
    kern = functools.partial(_inproj_kernel, tm=tm, tiles_per_seq=tiles_per_seq, rope=rope)
    tok = lambda width: pl.BlockSpec((tm, width), lambda i: (i, 0))
    out_widths = (CONV_CH, NA_DIM, SW_DIM, NA_DIM, NA_DIM, SW_KV_DIM, SW_KV_DIM)
    return pl.pallas_call(
        kern,
        grid=(t // tm,),
        in_specs=[tok(d),
                  pl.BlockSpec((SUBLANES, d), lambda i: (jnp.maximum(i * blk8 - 1, 0), 0)),
                  pl.BlockSpec((SUBLANES, d), lambda i: (jnp.minimum((i + 1) * blk8, last8), 0)),
                  pl.BlockSpec((1, 1, d), lambda i: (group_of_tile(i), 0, 0)),
                  pl.BlockSpec((1, 1, d), lambda i: (group_of_tile(i), 0, 1)),
                  _const_spec((1, d)),
                  _resident_spec(w.shape),
                  _const_spec(conv_w.shape),
                  _const_spec((1, CONV_CH)),
                  pl.BlockSpec((tm, LANES), lambda i: (i % tiles_per_seq, 0)),
                  pl.BlockSpec((tm, LANES), lambda i: (i % tiles_per_seq, 0))],
        out_specs=[tok(wd) for wd in out_widths],
        out_shape=[jax.ShapeDtypeStruct((t, wd), BF16) for wd in out_widths],
        scratch_shapes=[pltpu.VMEM((tm + 2 * SUBLANES, CONV_CH), F32)],
        compiler_params=_params(("arbitrary",)),
        name="in_projection",
    )(x2d, x2d, x2d, mods, mods, g_norm, w, conv_w, g_a, cos_t, sin_t)


def _masked_attention(qm, k_loc, v_loc, add_loc, k_ctx, v_ctx, sink):
    s_ctx = _dot_nt(qm, k_ctx)
    m = jnp.max(s_ctx, axis=-1, keepdims=True)
    if k_loc is not None:
        s_loc = _dot_nt(qm, k_loc) + add_loc
        m = jnp.maximum(m, jnp.max(s_loc, axis=-1, keepdims=True))
    if sink is not None:
        m = jnp.maximum(m, sink)
    p_ctx = jnp.exp2(s_ctx - m)
    den = jnp.sum(p_ctx, axis=-1, keepdims=True)
    o = _dot(p_ctx.astype(BF16), v_ctx)
    if k_loc is not None:
        p_loc = jnp.exp2(s_loc - m)
        den = den + jnp.sum(p_loc, axis=-1, keepdims=True)
        o = o + _dot(p_loc.astype(BF16), v_loc)
    if sink is not None:
        den = den + jnp.exp2(sink - m)
    return o / den


def _pair_attention(q, k_loc, v_loc, add_loc, k_ctx, v_ctx, sinks, lane_first):
    m = q.shape[0]
    zero = jnp.zeros_like(q)
    q2 = jnp.concatenate([jnp.where(lane_first, q, zero), jnp.where(lane_first, zero, q)], axis=0)
    sink2 = None
    if sinks is not None:
        second = lax.broadcasted_iota(jnp.int32, (2 * m, 1), 0) >= m
        sink2 = jnp.where(second, sinks[1], sinks[0])
    o = _masked_attention(q2, k_loc, v_loc, add_loc, k_ctx, v_ctx, sink2)
    return jnp.where(lane_first, o[:m], o[m:])


def _lane_first(m):
    return lax.broadcasted_iota(jnp.int32, (m, LANES), 1) < HEAD_DIM


def _na_kernel(*refs, rows):
    first = pl.program_id(1) * NA_STEP_GROUPS
    last_group = rows // NA_ROWS_PER_STEP - 1
    rebuilds = (first <= 1) | (first + NA_STEP_GROUPS - 1 >= last_group)

    @pl.when(rebuilds)
    def _():
        for t in range(NA_STEP_GROUPS):
            _na_group(t, first + t, *refs, rows=rows, may_rebuild=True)

    @pl.when(jnp.logical_not(rebuilds))
    def _():
        for t in range(NA_STEP_GROUPS):
            _na_group(t, first + t, *refs, rows=rows, may_rebuild=False)


def _na_group(t, gidx, q_ref, k_ref, v_ref, kc_ref, vc_ref, tab_ref, g_ref, o_ref, y_s, bias_s, *, rows,
              may_rebuild):
    base = jnp.clip(gidx * NA_ROWS_PER_STEP - NA_WIN_R // 2, 0, rows - NA_KEY_ROWS)
    start = pl.multiple_of(base * GRID_W, GRID_W)
    nk = NA_KEY_ROWS * GRID_W
    mq = NA_ROWS_PER_STEP * GRID_W
    lane_first = _lane_first(mq)
    tile_first = _lane_first(GRID_W)

    def fill_bias(head):
        for qr in range(NA_ROWS_PER_STEP):
            r = gidx * NA_ROWS_PER_STEP + qr
            rs = jnp.clip(r - NA_WIN_R // 2, 0, rows - NA_WIN_R)
            row0 = (head % 2) * mq + qr * GRID_W
            for m in range(NA_KEY_ROWS // 2):
                idx = []
                for kr in (base + 2 * m, base + 2 * m + 1):
                    ok = (kr >= rs) & (kr < rs + NA_WIN_R)
                    idx.append(jnp.where(ok, kr - r + NA_WIN_R - 1, NA_BIAS_MASKED))
                bias_s[head // 2, row0:row0 + GRID_W, m * LANES:(m + 1) * LANES] = jnp.where(
                    tile_first, tab_ref[head, idx[0]], tab_ref[head, idx[1]])

    if may_rebuild:
        @pl.when((gidx <= 1) | (gidx == rows // NA_ROWS_PER_STEP - 1))
        def _():
            for head in range(NA_HEADS):
                fill_bias(head)

    qrows = slice(t * mq, (t + 1) * mq)
    for j in range(NA_DIM // LANES):
        cols = slice(j * LANES, (j + 1) * LANES)
        k_loc = k_ref[pl.ds(start, nk), cols]
        v_loc = v_ref[pl.ds(start, nk), cols]
        y_s[qrows, cols] = _pair_attention(q_ref[qrows, cols], k_loc, v_loc, bias_s[j], kc_ref[:, cols],
                                           vc_ref[:, cols], None, lane_first)
    o_ref[qrows, :] = _rms(y_s[qrows, :], g_ref[...]).astype(BF16)


def _na_bias_table(rpb):
    n_heads, n_dr, n_dc = rpb.shape
    pad = GRID_W - NA_WIN_C
    rp = jnp.pad(rpb.astype(F32), ((0, 0), (0, 0), (pad, pad)))
    toep = jnp.stack([rp[:, :, GRID_W - 1 - qc:2 * GRID_W - 1 - qc] for qc in range(GRID_W)], axis=2)
    qc = np.arange(GRID_W)
    cstart = np.clip(qc - NA_WIN_C // 2, 0, GRID_W - NA_WIN_C)
    col_ok = (qc[None, :] >= cstart[:, None]) & (qc[None, :] < cstart[:, None] + NA_WIN_C)
    tab = jnp.where(col_ok[None, None], toep * LOG2E, NEG_INF)
    masked = jnp.full((n_heads, NA_BIAS_MASKED + 1 - n_dr, GRID_W, GRID_W), NEG_INF, F32)
    tab = jnp.concatenate([tab, masked], axis=1)
    return jnp.tile(tab, (1, 1, 1, LANES // GRID_W))


def _na_attention(qn, kn, vn, kn_c, vn_c, tab, g_b, *, bsz, seq_len, ctx_len):
    rows = seq_len // GRID_W
    n_groups = rows // (NA_ROWS_PER_STEP * NA_STEP_GROUPS)
    mq = NA_ROWS_PER_STEP * GRID_W * NA_STEP_GROUPS
    kern = functools.partial(_na_kernel, rows=rows)
    return pl.pallas_call(
        kern,
        grid=(bsz, n_groups),
        in_specs=[pl.BlockSpec((mq, NA_DIM), lambda b, g: (b * n_groups + g, 0)),
                  pl.BlockSpec((seq_len, NA_DIM), lambda b, g: (b, 0)),
                  pl.BlockSpec((seq_len, NA_DIM), lambda b, g: (b, 0)),
                  pl.BlockSpec((ctx_len, NA_DIM), lambda b, g: (b, 0)),
                  pl.BlockSpec((ctx_len, NA_DIM), lambda b, g: (b, 0)),
                  _resident_spec(tab.shape),
                  _const_spec((1, NA_DIM))],
        out_specs=pl.BlockSpec((mq, NA_DIM), lambda b, g: (b * n_groups + g, 0)),
        out_shape=jax.ShapeDtypeStruct(qn.shape, BF16),
        scratch_shapes=[pltpu.VMEM((mq, NA_DIM), F32),
                        pltpu.VMEM((NA_HEADS // 2, 2 * NA_ROWS_PER_STEP * GRID_W, NA_KEY_ROWS * GRID_W), F32)],
        compiler_params=_params(("arbitrary", "arbitrary")),
        name="na_attention",
    )(qn, kn, vn, kn_c, vn_c, tab, g_b)


def _sw_kernel(*refs, seq_len):
    for t in range(SW_STEP_BLOCKS):
        _sw_block(t, pl.program_id(1) * SW_STEP_BLOCKS + t, *refs, seq_len=seq_len)


def _sw_block(t, i, sink_ref, q_ref, k_ref, v_ref, kc_ref, vc_ref, g_ref, o_ref, y_s, *, seq_len):
    s0 = i * SW_QBLOCK
    kstart = pl.multiple_of(jnp.clip(s0 - SW_RADIUS, 0, seq_len - SW_SPAN), SW_RADIUS)
    k_loc = k_ref[pl.ds(kstart, SW_SPAN), :]
    v_loc = v_ref[pl.ds(kstart, SW_SPAN), :]
    qpos = s0 + lax.broadcasted_iota(jnp.int32, (SW_QBLOCK, SW_SPAN), 0)
    kpos = kstart + lax.broadcasted_iota(jnp.int32, (SW_QBLOCK, SW_SPAN), 1)
    band = jnp.where(jnp.abs(kpos - qpos) <= SW_RADIUS, 0.0, NEG_INF).astype(F32)
    band2 = jnp.concatenate([band, band], axis=0)
    lane_first = _lane_first(SW_QBLOCK)
    pairs = SW_HEADS // SW_KV_HEADS
    qrows = slice(t * SW_QBLOCK, (t + 1) * SW_QBLOCK)
    for j in range(pairs):
        cols = slice(j * LANES, (j + 1) * LANES)
        sinks = (sink_ref[j], sink_ref[j + pairs])
        y_s[qrows, cols] = _pair_attention(q_ref[qrows, cols], k_loc, v_loc, band2, kc_ref[...], vc_ref[...],
                                           sinks, lane_first)
    o_ref[qrows, :] = _rms(y_s[qrows, :], g_ref[...]).astype(BF16)


def _sw_attention(sink, qs, ks, vs, ks_c, vs_c, g_c, *, bsz, seq_len, ctx_len):
    step_tokens = SW_QBLOCK * SW_STEP_BLOCKS
    nblk = seq_len // step_tokens
    kern = functools.partial(_sw_kernel, seq_len=seq_len)
    return pl.pallas_call(
        kern,
        grid=(bsz, nblk),
        in_specs=[pl.BlockSpec(memory_space=pltpu.SMEM),
                  pl.BlockSpec((step_tokens, SW_DIM), lambda b, i: (b * nblk + i, 0)),
                  pl.BlockSpec((seq_len, SW_KV_DIM), lambda b, i: (b, 0)),
                  pl.BlockSpec((seq_len, SW_KV_DIM), lambda b, i: (b, 0)),
                  pl.BlockSpec((ctx_len, SW_KV_DIM), lambda b, i: (b, 0)),
                  pl.BlockSpec((ctx_len, SW_KV_DIM), lambda b, i: (b, 0)),
                  _const_spec((1, SW_DIM))],
        out_specs=pl.BlockSpec((step_tokens, SW_DIM), lambda b, i: (b * nblk + i, 0)),
        out_shape=jax.ShapeDtypeStruct(qs.shape, BF16),
        scratch_shapes=[pltpu.VMEM((step_tokens, SW_DIM), F32)],
        compiler_params=_params(("arbitrary", "arbitrary")),
        name="sw_attention",
    )(sink, qs, ks, vs, ks_c, vs_c, g_c)


def _ctx_attn_kernel(sink_ref, qn_ref, kn_ref, vn_ref, qs_ref, ks_ref, vs_ref, gb_ref, gc_ref,
                     yb_ref, yc_ref, y_s, *, ctx_len):
    lane_first = _lane_first(ctx_len)
    for j in range(NA_DIM // LANES):
        cols = slice(j * LANES, (j + 1) * LANES)
        y_s[:, cols] = _pair_attention(qn_ref[:, cols], None, None, None, kn_ref[:, cols], vn_ref[:, cols],
                                       None, lane_first)
    yb_ref[...] = _rms(y_s[...], gb_ref[...]).astype(BF16)
    pairs = SW_HEADS // SW_KV_HEADS
    for j in range(pairs):
        cols = slice(j * LANES, (j + 1) * LANES)
        sinks = (sink_ref[j], sink_ref[j + pairs])
        y_s[:, cols] = _pair_attention(qs_ref[:, cols], None, None, None, ks_ref[...], vs_ref[...],
                                       sinks, lane_first)
    yc_ref[...] = _rms(y_s[...], gc_ref[...]).astype(BF16)


def _ctx_attention(sink, qn, kn, vn, qs, ks, vs, g_b, g_c, *, bsz, ctx_len):
    kern = functools.partial(_ctx_attn_kernel, ctx_len=ctx_len)
    tok = lambda width: pl.BlockSpec((ctx_len, width), lambda b: (b, 0))
    return pl.pallas_call(
        kern,
        grid=(bsz,),
        in_specs=[pl.BlockSpec(memory_space=pltpu.SMEM),
                  tok(NA_DIM), tok(NA_DIM), tok(NA_DIM), tok(SW_DIM), tok(SW_KV_DIM), tok(SW_KV_DIM),
                  _const_spec((1, NA_DIM)), _const_spec((1, SW_DIM))],
        out_specs=[tok(NA_DIM), tok(SW_DIM)],
        out_shape=[jax.ShapeDtypeStruct(qn.shape, BF16), jax.ShapeDtypeStruct(qs.shape, BF16)],
        scratch_shapes=[pltpu.VMEM((ctx_len, NA_DIM), F32)],
        compiler_params=_params(("arbitrary",)),
        name="ctx_attention",
    )(sink, qn, kn, vn, qs, ks, vs, g_b, g_c)


def _mix_residual(ya_ref, yb_ref, yc_ref, x_ref, g1_ref, sh_ref, sc_ref, gn_ref, wa_ref, wb_ref, wc_ref):
    mix = _dot(ya_ref[...], wa_ref[...]) + _dot(yb_ref[...], wb_ref[...]) + _dot(yc_ref[...], wc_ref[...])
    xl = x_ref[...] + g1_ref[0] * mix
    h = _rms(xl, gn_ref[...]) * (1.0 + sc_ref[0]) + sh_ref[0]
    return xl, h


def _mix_specs(x2d, mods, group_of_tile, g_norm, ys, ws, tm):
    d = x2d.shape[1]
    tok = lambda width: pl.BlockSpec((tm, width), lambda i: (i, 0))
    mod = lambda j: pl.BlockSpec((1, 1, d), lambda i: (group_of_tile(i), 0, j))
    in_specs = ([tok(y.shape[1]) for y in ys] + [tok(d), mod(2), mod(3), mod(4), _const_spec((1, d))]
                + [_resident_spec(w.shape) for w in ws])
    return in_specs, [*ys, x2d, mods, mods, mods, g_norm, *ws]


def _outproj_router_kernel(*refs):
    *mix_refs, r_ref, xo_ref, h_ref, lg_ref = refs
    xl, h = _mix_residual(*mix_refs)
    xo_ref[...] = xl
    hi = h.astype(BF16)
    h_ref[...] = hi
    tm = h.shape[0]
    lo = (h - hi.astype(F32)).astype(BF16)
    r = _dot(jnp.concatenate([hi, lo], axis=0), r_ref[...])
    lg_ref[...] = (r[:tm, :LOGIT_LANES] + r[:tm, LOGIT_LANES:]) + (r[tm:, :LOGIT_LANES] + r[tm:, LOGIT_LANES:])


def _outproj_router(ys, x2d, mods, group_of_tile, g_norm, ws, router_w, *, tm):
    t, d = x2d.shape
    in_specs, args = _mix_specs(x2d, mods, group_of_tile, g_norm, ys, ws, tm)
    tok = lambda width: pl.BlockSpec((tm, width), lambda i: (i, 0))
    return pl.pallas_call(
        _outproj_router_kernel,
        grid=(t // tm,),
        in_specs=in_specs + [_resident_spec(router_w.shape)],
        out_specs=[tok(d), tok(d), tok(LOGIT_LANES)],
        out_shape=[jax.ShapeDtypeStruct((t, d), F32), jax.ShapeDtypeStruct((t, d), BF16),
                   jax.ShapeDtypeStruct((t, LOGIT_LANES), F32)],
        compiler_params=_params(("arbitrary",)),
        name="out_projection_router",
    )(*args, router_w)


def _outproj_ffn_kernel(*refs, chunks):
    *mix_refs, g2_ref, wg_ref, wu_ref, wd_ref, o_ref = refs
    xl, h = _mix_residual(*mix_refs)
    h = h.astype(BF16)
    acc = None
    for lo, hi in chunks:
        a = _silu(_dot(h, wg_ref[:, lo:hi].astype(BF16))) * _dot(h, wu_ref[:, lo:hi].astype(BF16))
        part = _dot(a.astype(BF16), wd_ref[lo:hi, :].astype(BF16))
        acc = part if acc is None else acc + part
    o_ref[...] = xl + g2_ref[0] * acc


def _outproj_ffn(ys, x2d, mods, group_of_tile, g_norm, ws, wg, wu, wd, *, tm):
    t, d = x2d.shape
    d_ff = wg.shape[1]
    step = 1024
    chunks = tuple((lo, min(lo + step, d_ff)) for lo in range(0, d_ff, step))
    in_specs, args = _mix_specs(x2d, mods, group_of_tile, g_norm, ys, ws, tm)
    return pl.pallas_call(
        functools.partial(_outproj_ffn_kernel, chunks=chunks),
        grid=(t // tm,),
        in_specs=in_specs + [pl.BlockSpec((1, 1, d), lambda i: (group_of_tile(i), 0, 5)),
                             _resident_spec(wg.shape), _resident_spec(wu.shape), _resident_spec(wd.shape)],
        out_specs=pl.BlockSpec((tm, d), lambda i: (i, 0)),
        out_shape=jax.ShapeDtypeStruct((t, d), F32),
        compiler_params=_params(("arbitrary",)),
        name="out_projection_ffn",
    )(*args, mods, wg, wu, wd)


def _route_kernel(*refs):
    for j in range(MOE_STEP_CHUNKS):
        _route_chunk(j, *refs)


def _route_chunk(j, lg_ref, h_ref, hl_ref, pos_ref, gsel_ref, meta_ref):
    toks = slice(j * MOE_CHUNK, (j + 1) * MOE_CHUNK)
    logits = lg_ref[toks, :]
    lane = lax.broadcasted_iota(jnp.int32, logits.shape, 1)
    lg = jnp.where(lane < N_EXPERTS, logits, -jnp.inf)
    v1 = jnp.max(lg, axis=-1, keepdims=True)
    i1 = jnp.min(jnp.where(lg == v1, lane, LOGIT_LANES), axis=-1, keepdims=True)
    lg2 = jnp.where(lane == i1, -jnp.inf, lg)
    v2 = jnp.max(lg2, axis=-1, keepdims=True)
    i2 = jnp.min(jnp.where(lg2 == v2, lane, LOGIT_LANES), axis=-1, keepdims=True)
    e2 = jnp.exp(v2 - v1)
    den = 1.0 + e2
    g_first = 1.0 / den
    g_second = e2 / den
    sel1 = lane == i1
    sel2 = lane == i2
    onehot = jnp.where(sel1, 1.0, jnp.where(sel2, 1.0, 0.0))

    r_i = lax.broadcasted_iota(jnp.int32, (MOE_CHUNK, MOE_CHUNK), 0)
    c_i = lax.broadcasted_iota(jnp.int32, (MOE_CHUNK, MOE_CHUNK), 1)
    earlier = jnp.where(c_i < r_i, 1.0, 0.0).astype(BF16)
    within = _dot(earlier, onehot.astype(BF16))
    count = jnp.sum(onehot, axis=0, keepdims=True)
    n_gran = jnp.floor((count + (MOE_GRANULE - 1)) * (1.0 / MOE_GRANULE))
    u_r = lax.broadcasted_iota(jnp.int32, (LOGIT_LANES, LOGIT_LANES), 0)
    u_c = lax.broadcasted_iota(jnp.int32, (LOGIT_LANES, LOGIT_LANES), 1)
    before = jnp.where(u_r < u_c, 1.0, 0.0).astype(BF16)
    n_gran8 = jnp.broadcast_to(n_gran, (SUBLANES, LOGIT_LANES))
    off_gran = _dot(n_gran8.astype(BF16), before)[0:1]
    lpos = off_gran * MOE_GRANULE + within
    p_first = jnp.sum(jnp.where(sel1, lpos, 0.0), axis=-1, keepdims=True)
    p_second = jnp.sum(jnp.where(sel2, lpos, 0.0), axis=-1, keepdims=True)
    pos = jnp.where(lane == 0, p_first, jnp.where(lane == 1, p_second, -1.0))
    pos_ref[toks, :] = pos
    gsel_ref[toks, :] = jnp.where(lane == 0, g_first, jnp.where(lane == 1, g_second, 0.0))
    rows8 = lax.broadcasted_iota(jnp.int32, (SUBLANES, LOGIT_LANES), 0)
    meta_ref[j] = jnp.where(rows8 == 0, off_gran, jnp.where(rows8 == 1, n_gran, 0.0)).astype(jnp.int32)

    pos_t = pos.T
    row_id = lax.broadcasted_iota(jnp.int32, (MOE_LOCAL_ROWS, MOE_CHUNK), 0).astype(F32)
    pick = jnp.where(row_id == pos_t[0:1], 1.0, jnp.where(row_id == pos_t[1:2], 1.0, 0.0)).astype(BF16)
    hl_ref[j] = _dot(pick, h_ref[toks, :]).astype(BF16)


def _route(logits, h2):
    t, d = h2.shape
    n_chunks = t // MOE_CHUNK
    per_step = MOE_STEP_CHUNKS
    tok = lambda width: pl.BlockSpec((per_step * MOE_CHUNK, width), lambda i: (i, 0))
    return pl.pallas_call(
        _route_kernel,
        grid=(n_chunks // per_step,),
        in_specs=[tok(LOGIT_LANES), tok(d)],
        out_specs=[pl.BlockSpec((per_step, MOE_LOCAL_ROWS, d), lambda i: (i, 0, 0)),
                   tok(LOGIT_LANES), tok(LOGIT_LANES),
                   pl.BlockSpec((per_step, SUBLANES, LOGIT_LANES), lambda i: (i, 0, 0))],
        out_shape=[jax.ShapeDtypeStruct((n_chunks, MOE_LOCAL_ROWS, d), BF16),
                   jax.ShapeDtypeStruct((t, LOGIT_LANES), F32),
                   jax.ShapeDtypeStruct((t, LOGIT_LANES), F32),
                   jax.ShapeDtypeStruct((n_chunks, SUBLANES, LOGIT_LANES), jnp.int32)],
        compiler_params=_params(("arbitrary",)),
        name="moe_route",
    )(logits, h2)


def _experts_kernel(meta_ref, hl_hbm, wg_ref, wu_ref, wd_ref, yl_hbm, he_s, ye_s, yb_s, zero_s,
                    gather_sem, store_sem, fill_sem, *, n_chunks):
    ws = pl.program_id(0)
    e = pl.program_id(1)
    f = pl.program_id(2)
    n_e = pl.num_programs(1)
    n_f = pl.num_programs(2)
    step = ws * n_e + e
    last_step = pl.num_programs(0) * n_e - 1

    def meta(ws_, e_, c, k):
        return meta_ref[((ws_ * n_chunks + c) * 2 + k) * N_EXPERTS + e_]

    def rows_of(granules):
        return pl.multiple_of(granules * MOE_GRANULE, MOE_GRANULE)

    def for_each_run(ws_, e_, fn):
        dst = 0
        for c in range(n_chunks):
            src, n = meta(ws_, e_, c, 0), meta(ws_, e_, c, 1)

            @pl.when(n > 0)
            def _(c=c, src=src, n=n, dst=dst):
                fn(ws_ * n_chunks + c, rows_of(src), rows_of(dst), rows_of(n))
            dst = dst + n
        return dst * MOE_GRANULE

    def gather_copy(slot):
        return lambda chunk, src, dst, rows: pltpu.make_async_copy(
            hl_hbm.at[chunk, pl.ds(src, rows)], he_s.at[slot, pl.ds(dst, rows)], gather_sem.at[slot])

    def store_copy(chunk, src, dst, rows):
        return pltpu.make_async_copy(yb_s.at[pl.ds(dst, rows)], yl_hbm.at[chunk, pl.ds(src, rows)], store_sem)

    def fill_tails(start):
        for c in range(n_chunks):
            used = meta(ws, N_EXPERTS - 1, c, 0) + meta(ws, N_EXPERTS - 1, c, 1)
            tail = MOE_LOCAL_ROWS // MOE_GRANULE - used

            @pl.when(tail > 0)
            def _(c=c, used=used, tail=tail):
                cp = pltpu.make_async_copy(zero_s.at[pl.ds(0, rows_of(tail))],
                                           yl_hbm.at[ws * n_chunks + c, pl.ds(rows_of(used), rows_of(tail))],
                                           fill_sem)
                cp.start() if start else cp.wait()

    slot = step % 2

    @pl.when(f == 0)
    def _():
        @pl.when(step == 0)
        def _():
            he_s[...] = jnp.zeros_like(he_s)
            zero_s[...] = jnp.zeros_like(zero_s)
            for_each_run(ws, e, lambda *run: gather_copy(slot)(*run).start())

        @pl.when(e == 0)
        def _():
            fill_tails(True)
            fill_tails(False)

        for_each_run(ws, e, lambda *run: gather_copy(slot)(*run).wait())

        @pl.when(step < last_step)
        def _():
            nxt = step + 1
            for_each_run(nxt // n_e, nxt % n_e, lambda *run: gather_copy(1 - slot)(*run).start())

    def m_tile(start, size, mode):
        rows = pl.ds(pl.multiple_of(start, MOE_M_MIN), size)
        hrow = he_s[slot, rows, :]
        a = _silu(_dot(hrow, wg_ref[0].astype(BF16))) * _dot(hrow, wu_ref[0].astype(BF16))
        part = _dot(a.astype(BF16), wd_ref[0].astype(BF16))
        if mode == "first":
            ye_s[rows, :] = part
        elif mode == "middle":
            ye_s[rows, :] += part
        else:
            yb_s[rows, :] = (ye_s[rows, :] + part).astype(BF16)

    n_rows = sum(meta(ws, e, c, 1) for c in range(n_chunks)) * MOE_GRANULE
    n_full = n_rows // MOE_M_TILE
    rem = n_rows - n_full * MOE_M_TILE

    def all_tiles(mode):
        balanced = (n_full == 2) & (rem > 0) & (rem <= MOE_M_MIN)

        @pl.when(balanced)
        def _():
            m_tile(0, MOE_M_TILE, mode)
            m_tile(MOE_M_TILE, MOE_M_TILE, mode)
            m_tile(2 * MOE_M_TILE, MOE_M_MIN, mode)

        @pl.when(jnp.logical_not(balanced))
        def _():
            def tile_pair(m, carry):
                m_tile(2 * m * MOE_M_TILE, MOE_M_TILE, mode)
                m_tile((2 * m + 1) * MOE_M_TILE, MOE_M_TILE, mode)
                return carry
            lax.fori_loop(0, n_full // 2, tile_pair, 0)

            @pl.when(n_full % 2 == 1)
            def _():
                m_tile((n_full - 1) * MOE_M_TILE, MOE_M_TILE, mode)

            size = MOE_M_TILE
            while size >= MOE_M_MIN:
                lo = 0 if size == MOE_M_MIN else size // 2

                @pl.when((rem > lo) & (rem <= size))
                def _(size=size):
                    m_tile(n_full * MOE_M_TILE, size, mode)
                size //= 2

    @pl.when(f == 0)
    def _():
        all_tiles("first")

    @pl.when((f > 0) & (f < n_f - 1))
    def _():
        all_tiles("middle")

    @pl.when(f == n_f - 1)
    def _():
        @pl.when(step > 0)
        def _():
            prev = step - 1
            for_each_run(prev // n_e, prev % n_e, lambda *run: store_copy(*run).wait())

        all_tiles("last")
        for_each_run(ws, e, lambda *run: store_copy(*run).start())

        @pl.when(step == last_step)
        def _():
            for_each_run(ws, e, lambda *run: store_copy(*run).wait())


def _experts(meta, hl, wg, wu, wd, *, n_chunks):
    total_chunks, local_rows, d = hl.shape
    n_exp, _, d_ff = wg.shape
    tf = MOE_FF_TILE
    assert d_ff // tf >= 2, "the first and the last hidden tile must be different grid steps"
    max_rows = n_chunks * MOE_CHUNK
    grid_spec = pltpu.PrefetchScalarGridSpec(
        num_scalar_prefetch=1,
        grid=(total_chunks // n_chunks, n_exp, d_ff // tf),
        in_specs=[pl.BlockSpec(memory_space=pl.ANY),
                  pl.BlockSpec((1, d, tf), lambda i, e, f, m: (e, 0, f)),
                  pl.BlockSpec((1, d, tf), lambda i, e, f, m: (e, 0, f)),
                  pl.BlockSpec((1, tf, d), lambda i, e, f, m: (e, f, 0))],
        out_specs=pl.BlockSpec(memory_space=pl.ANY),
        scratch_shapes=[pltpu.VMEM((2, max_rows, d), BF16), pltpu.VMEM((max_rows, d), F32),
                        pltpu.VMEM((max_rows, d), BF16), pltpu.VMEM((local_rows, d), BF16),
                        pltpu.SemaphoreType.DMA((2,)), pltpu.SemaphoreType.DMA(()), pltpu.SemaphoreType.DMA(())])
    return pl.pallas_call(
        functools.partial(_experts_kernel, n_chunks=n_chunks),
        grid_spec=grid_spec,
        out_shape=jax.ShapeDtypeStruct(hl.shape, BF16),
        compiler_params=_params(("arbitrary", "arbitrary", "arbitrary")),
        name="moe_experts",
    )(meta, hl, wg, wu, wd)


def _scatter_kernel(yl_ref, pos_ref, gsel_ref, x_ref, g2_ref, gf_ref, o_ref):
    row_id = lax.broadcasted_iota(jnp.int32, (MOE_CHUNK, MOE_LOCAL_ROWS), 1).astype(F32)
    for j in range(MOE_STEP_CHUNKS):
        toks = slice(j * MOE_CHUNK, (j + 1) * MOE_CHUNK)
        pos = pos_ref[toks, :]
        gsel = gsel_ref[toks, :]
        yl = yl_ref[j]
        y = None
        for k in range(2):
            pick = jnp.where(pos[:, k:k + 1] == row_id, 1.0, 0.0).astype(BF16)
            term = gsel[:, k:k + 1] * _dot(pick, yl)
            y = term if y is None else y + term
        o_ref[toks, :] = _rms(x_ref[toks, :] + g2_ref[0] * y, gf_ref[...])


def _scatter_final(yl, pos, gsel, x2d, mods, group_of_step, g_final):
    t, d = x2d.shape
    per_step = MOE_STEP_CHUNKS
    tok = lambda width: pl.BlockSpec((per_step * MOE_CHUNK, width), lambda i: (i, 0))
    return pl.pallas_call(
        _scatter_kernel,
        grid=(t // (per_step * MOE_CHUNK),),
        in_specs=[pl.BlockSpec((per_step, MOE_LOCAL_ROWS, d), lambda i: (i, 0, 0)),
                  tok(LOGIT_LANES), tok(LOGIT_LANES), tok(d),
                  pl.BlockSpec((1, 1, d), lambda i: (group_of_step(i), 0, 5)), _const_spec((1, d))],
        out_specs=tok(d),
        out_shape=jax.ShapeDtypeStruct((t, d), F32),
        compiler_params=_params(("arbitrary",)),
        name="moe_scatter_final",
    )(yl, pos, gsel, x2d, mods, g_final)


def _rope_tables(n):
    t = jnp.arange(n)
    row = (t // GRID_W).astype(F32)
    col = (t % GRID_W).astype(F32)
    n_freq = HEAD_DIM // 4
    inv = ROPE_THETA ** (-jnp.arange(n_freq, dtype=F32) / n_freq)
    ang = jnp.stack([row[:, None] * inv, col[:, None] * inv], axis=1)
    cos = jnp.cos(ang)[:, :, None, :]
    sin = jnp.sin(ang)[:, :, None, :]
    cos_f = jnp.broadcast_to(cos, (n, 2, 2, n_freq)).reshape(n, HEAD_DIM)
    sin_s = jnp.concatenate([-sin, sin], axis=2).reshape(n, HEAD_DIM)
    reps = LANES // HEAD_DIM
    return jnp.tile(cos_f, (1, reps)), jnp.tile(sin_s, (1, reps))


def _pair_perm():
    pairs = SW_HEADS // SW_KV_HEADS
    order = [h for j in range(pairs) for h in (j, j + pairs)]
    return np.concatenate([np.arange(h * HEAD_DIM, (h + 1) * HEAD_DIM) for h in order])


def kernel(x, c, ctx, c_ctx, w_ada, b_ada, g_norm1, g_norm2, w_in, conv_w, na_rpb, sw_sink, g_mix, w_out,
           ffn_w_gate, ffn_w_up, ffn_w_down, w_router, moe_w_gate, moe_w_up, moe_w_down, g_final):
    bsz, n, d = x.shape
    ctx_len = ctx.shape[1]
    depth = w_in.shape[0]
    assert w_in.shape[2] == D_IN and n % GRID_W == 0
    assert depth == 2, "layer 0 dense on both streams, layer 1 routed experts on the latent stream only"
    tm = 512
    tm_in = 1024
    tm_ctx = ctx_len
    set_chunks = MOE_SET_CHUNKS
    assert n % tm == 0 and n % tm_in == 0 and (n // GRID_W) % (NA_ROWS_PER_STEP * NA_STEP_GROUPS) == 0
    assert n % (SW_QBLOCK * SW_STEP_BLOCKS) == 0 and tm % MOE_CHUNK == 0
    assert n % (MOE_CHUNK * MOE_STEP_CHUNKS) == 0 and (bsz * n) % (MOE_CHUNK * set_chunks) == 0
    assert (MOE_CHUNK * set_chunks) % MOE_M_TILE == 0

    cond_rows = 2 * SUBLANES
    cond = jnp.zeros((cond_rows, d), F32).at[:bsz].set(c).at[bsz].set(c_ctx)
    mods_all = _ada(cond, w_ada, b_ada).reshape(depth, cond_rows, 1, 6 * d)

    lat_group = lambda i: i // (n // tm)
    ctx_group = lambda i: bsz
    cos_t, sin_t = _rope_tables(n)
    perm = _pair_perm()

    xl = x.reshape(bsz * n, d)
    xc = ctx.reshape(bsz * ctx_len, d)
    out = None
    for l in range(depth):
        last = l == depth - 1
        mods = mods_all[l]
        w_l = w_in[l]
        w_l = jnp.concatenate([w_l[:, :OFF_SQ], w_l[:, OFF_SQ:OFF_NK][:, perm], w_l[:, OFF_NK:]], axis=1).astype(BF16)
        g_a = g_mix[l, :CONV_CH].reshape(1, CONV_CH)
        g_b = g_mix[l, CONV_CH:CONV_CH + NA_DIM].reshape(1, NA_DIM)
        g_c = g_mix[l, CONV_CH + NA_DIM:][perm].reshape(1, SW_DIM)
        wo = w_out[l]
        wa = wo[:CONV_CH].astype(BF16)
        wb = wo[CONV_CH:CONV_CH + NA_DIM].astype(BF16)
        wc = wo[CONV_CH + NA_DIM:][perm].astype(BF16)
        gn1 = g_norm1[l].reshape(1, d)
        gn2 = g_norm2[l].reshape(1, d)
        sink = sw_sink[l].astype(F32) * LOG2E
        tab = _na_bias_table(na_rpb[l])

        ya, qn, qs, kn, vn, ks, vs = _inproj(xl, mods, lambda i: i // (n // tm_in), gn1, w_l, conv_w[l], g_a,
                                             cos_t, sin_t, seq_len=n, tm=tm_in, rope=True)
        cya, cqn, cqs, ckn, cvn, cks, cvs = _inproj(xc, mods, ctx_group, gn1, w_l, conv_w[l], g_a, cos_t, sin_t,
                                                    seq_len=ctx_len, tm=tm_ctx, rope=False)
        yb = _na_attention(qn, kn, vn, ckn, cvn, tab, g_b, bsz=bsz, seq_len=n, ctx_len=ctx_len)
        yc = _sw_attention(sink, qs, ks, vs, cks, cvs, g_c, bsz=bsz, seq_len=n, ctx_len=ctx_len)

        if l % 2 == 0:
            i = l // 2
            wg, wu, wd = ffn_w_gate[i], ffn_w_up[i], ffn_w_down[i]
            xl = _outproj_ffn((ya, yb, yc), xl, mods, lat_group, gn2, (wa, wb, wc), wg, wu, wd, tm=tm)
            if not last:
                cyb, cyc = _ctx_attention(sink, cqn, ckn, cvn, cqs, cks, cvs, g_b, g_c, bsz=bsz, ctx_len=ctx_len)
                xc = _outproj_ffn((cya, cyb, cyc), xc, mods, ctx_group, gn2, (wa, wb, wc), wg, wu, wd,
                                  tm=tm_ctx)
        else:
            i = l // 2
            wr = jnp.zeros((d, LOGIT_LANES), F32).at[:, :N_EXPERTS].set(w_router[i])
            wr_hi = wr.astype(BF16)
            wr_lo = (wr - wr_hi.astype(F32)).astype(BF16)
            wr2 = jnp.concatenate([wr_hi, wr_lo], axis=1)
            wg, wu, wd = moe_w_gate[i], moe_w_up[i], moe_w_down[i]
            xl, h2, logits = _outproj_router((ya, yb, yc), xl, mods, lambda i: i // (n // tm_in), gn2,
                                             (wa, wb, wc), wr2, tm=tm_in)
            hl, pos, gsel, meta = _route(logits, h2)
            meta = meta[:, :2, :N_EXPERTS].reshape(-1)
            yl = _experts(meta, hl, wg, wu, wd, n_chunks=set_chunks)
            step_tokens = MOE_CHUNK * MOE_STEP_CHUNKS
            out = _scatter_final(yl, pos, gsel, xl, mods, lambda i: i // (n // step_tokens), g_final.reshape(1, d))
    return out.reshape(bsz, n, d)
```

```python
import functools

import numpy as np
import jax
import jax.numpy as jnp
from jax import lax
from jax.experimental import pallas as pl
from jax.experimental.pallas import tpu as pltpu

F32 = jnp.float32
BF16 = jnp.bfloat16

GRID_W = 64
HEAD_DIM = 64
CONV_CH = 256
NA_HEADS = 6
NA_DIM = NA_HEADS * HEAD_DIM
SW_HEADS = 6
SW_KV_HEADS = 2
SW_DIM = SW_HEADS * HEAD_DIM
SW_KV_DIM = SW_KV_HEADS * HEAD_DIM
NA_WIN_R = 8
NA_WIN_C = 16
SW_RADIUS = 128
ROPE_THETA = 10000.0
N_EXPERTS = 8
EPS = 1e-6
NEG_INF = -1e30
OFF_H, OFF_B, OFF_C = 0, CONV_CH, 2 * CONV_CH
OFF_NQ = 3 * CONV_CH
OFF_SQ = OFF_NQ + NA_DIM
OFF_NK = OFF_SQ + SW_DIM
OFF_NV = OFF_NK + NA_DIM
OFF_SK = OFF_NV + NA_DIM
OFF_SV = OFF_SK + SW_KV_DIM
D_IN = OFF_SV + SW_KV_DIM
LOG2E = 1.4426950408889634
Q_SCALE = HEAD_DIM ** -0.5 * LOG2E

LANES = 128
SUBLANES = 8
VMEM_BYTES_V7X = 64 * 1024 * 1024
VMEM_LIMIT = VMEM_BYTES_V7X * 7 // 8

NA_ROWS_PER_STEP = 4
NA_KEY_ROWS = NA_ROWS_PER_STEP + NA_WIN_R
NA_BIAS_MASKED = 2 * NA_WIN_R - 1
MOE_CHUNK = 256
MOE_GRANULE = 2 * SUBLANES
MOE_LOCAL_ROWS = 640
MOE_SET_CHUNKS = 16
MOE_STEP_CHUNKS = 4
MOE_M_TILE = 512
MOE_M_MIN = 128
MOE_FF_TILE = 512
NA_STEP_GROUPS = 2
SW_QBLOCK = 256
SW_STEP_BLOCKS = 8
SW_SPAN = SW_QBLOCK + 2 * SW_RADIUS
LOGIT_LANES = LANES


def _params(semantics):
    return pltpu.CompilerParams(dimension_semantics=semantics, vmem_limit_bytes=VMEM_LIMIT)


def _const_spec(shape):
    nd = len(shape)
    return pl.BlockSpec(shape, lambda *_: (0,) * nd)


def _resident_spec(shape):
    nd = len(shape)
    return pl.BlockSpec(shape, lambda *_: (0,) * nd, pipeline_mode=pl.Buffered(1))


def _rms(x, g):
    y = x * lax.rsqrt(jnp.mean(x * x, axis=-1, keepdims=True) + EPS)
    return y * g


def _silu(x):
    return x / (1.0 + jnp.exp(-x))


def _dot(a, b):
    return jnp.dot(a, b, preferred_element_type=F32)


def _dot_nt(a, b):
    return lax.dot_general(a, b, (((1,), (1,)), ((), ())), preferred_element_type=F32)


def _ada_kernel(c_ref, w_ref, b_ref, o_ref):
    a = _silu(c_ref[...])
    o_ref[0] = jnp.dot(a, w_ref[0], preferred_element_type=F32,
                       precision=lax.Precision.HIGHEST) + b_ref[0]


def _ada(cond, w_ada, b_ada):
    depth, d, n_out = w_ada.shape
    rows = cond.shape[0]
    tn = n_out // 6
    return pl.pallas_call(
        _ada_kernel,
        grid=(depth, n_out // tn),
        in_specs=[pl.BlockSpec((rows, d), lambda l, j: (0, 0)),
                  pl.BlockSpec((1, d, tn), lambda l, j: (l, 0, j)),
                  pl.BlockSpec((1, 1, tn), lambda l, j: (l, 0, j))],
        out_specs=pl.BlockSpec((1, rows, tn), lambda l, j: (l, 0, j)),
        out_shape=jax.ShapeDtypeStruct((depth, rows, n_out), F32),
        compiler_params=_params(("arbitrary", "arbitrary")),
        name="ada_modulation",
    )(cond, w_ada, b_ada.reshape(depth, 1, n_out))


def _rope(x, cos, sin, lane_lo):
    rot = jnp.where(lane_lo, pltpu.roll(x, LANES - 16, 1), pltpu.roll(x, 16, 1))
    return x * cos + rot * sin


def _inproj_kernel(x_ref, xp_ref, xn_ref, sh_ref, sc_ref, gn_ref, w_ref, cw_ref, ga_ref, cos_ref, sin_ref,
                   ya_ref, qn_ref, qs_ref, kn_ref, vn_ref, ks_ref, vs_ref, u_s, *, tm, tiles_per_seq, rope):
    i = pl.program_id(0)
    g = gn_ref[...]
    sh = sh_ref[0]
    sc = sc_ref[0]

    def normmod(x):
        return _rms(x, g) * (1.0 + sc) + sh

    h = normmod(x_ref[...]).astype(BF16)
    p = _dot(h, w_ref[...])

    xh = jnp.concatenate([xp_ref[...], xn_ref[...]], axis=0)
    ph = _dot(normmod(xh).astype(BF16), w_ref[:, :OFF_NQ])
    uh = ph[:, OFF_C:OFF_C + CONV_CH] * ph[:, OFF_H:OFF_H + CONV_CH]
    pos = i % tiles_per_seq
    u = p[:, OFF_C:OFF_C + CONV_CH] * p[:, OFF_H:OFF_H + CONV_CH]
    u_s[0:SUBLANES] = jnp.where(pos == 0, 0.0, uh[0:SUBLANES])
    u_s[SUBLANES:SUBLANES + tm] = u
    u_s[SUBLANES + tm:2 * SUBLANES + tm] = jnp.where(pos == tiles_per_seq - 1, 0.0, uh[SUBLANES:])
    cw = cw_ref[...]
    conv = (u_s[SUBLANES - 1:SUBLANES - 1 + tm] * cw[0:1] + u * cw[1:2]
            + u_s[SUBLANES + 1:SUBLANES + 1 + tm] * cw[2:3])
    ya = p[:, OFF_B:OFF_B + CONV_CH] * conv
    ya_ref[...] = _rms(ya, ga_ref[...]).astype(BF16)

    qn_ref[...] = (p[:, OFF_NQ:OFF_NQ + NA_DIM] * Q_SCALE).astype(BF16)
    kn_ref[...] = p[:, OFF_NK:OFF_NK + NA_DIM].astype(BF16)
    vn_ref[...] = p[:, OFF_NV:OFF_NV + NA_DIM].astype(BF16)
    vs_ref[...] = p[:, OFF_SV:OFF_SV + SW_KV_DIM].astype(BF16)
    if rope:
        cos = cos_ref[...]
        sin = sin_ref[...]
        lane_lo = (lax.broadcasted_iota(jnp.int32, (tm, LANES), 1) % 32) < 16
        for j in range(SW_DIM // LANES):
            q = p[:, OFF_SQ + j * LANES:OFF_SQ + (j + 1) * LANES]
            qs_ref[:, j * LANES:(j + 1) * LANES] = (_rope(q, cos, sin, lane_lo) * Q_SCALE).astype(BF16)
        ks_ref[...] = _rope(p[:, OFF_SK:OFF_SK + SW_KV_DIM], cos, sin, lane_lo).astype(BF16)
    else:
        qs_ref[...] = (p[:, OFF_SQ:OFF_SQ + SW_DIM] * Q_SCALE).astype(BF16)
        ks_ref[...] = p[:, OFF_SK:OFF_SK + SW_KV_DIM].astype(BF16)


def _inproj(x2d, mods, group_of_tile, g_norm, w, conv_w, g_a, cos_t, sin_t, *, seq_len, tm, rope):
    t, d = x2d.shape
    tiles_per_seq = seq_len // tm
    blk8 = tm // SUBLANES
    last8 = t // SUBLANES - 1
    kern = functools.partial(_inproj_kernel, tm=tm, tiles_per_seq=tiles_per_seq, rope=rope)
    tok = lambda width: pl.BlockSpec((tm, width), lambda i: (i, 0))
    out_widths = (CONV_CH, NA_DIM, SW_DIM, NA_DIM, NA_DIM, SW_KV_DIM, SW_KV_DIM)
    return pl.pallas_call(
        kern,
        grid=(t // tm,),
        in_specs=[tok(d),
                  pl.BlockSpec((SUBLANES, d), lambda i: (jnp.maximum(i * blk8 - 1, 0), 0)),
                  pl.BlockSpec((SUBLANES, d), lambda i: (jnp.minimum((i + 1) * blk8, last8), 0)),
                  pl.BlockSpec((1, 1, d), lambda i: (group_of_tile(i), 0, 0)),
                  pl.BlockSpec((1, 1, d), lambda i: (group_of_tile(i), 0, 1)),
                  _const_spec((1, d)),
                  _resident_spec(w.shape),
                  _const_spec(conv_w.shape),
                  _const_spec((1, CONV_CH)),
                  pl.BlockSpec((tm, LANES), lambda i: (i % tiles_per_seq, 0)),
                  pl.BlockSpec((tm, LANES), lambda i: (i % tiles_per_seq, 0))],
        out_specs=[tok(wd) for wd in out_widths],
        out_shape=[jax.ShapeDtypeStruct((t, wd), BF16) for wd in out_widths],
        scratch_shapes=[pltpu.VMEM((tm + 2 * SUBLANES, CONV_CH), F32)],
        compiler_params=_params(("arbitrary",)),
        name="in_projection",
    )(x2d, x2d, x2d, mods, mods, g_norm, w, conv_w, g_a, cos_t, sin_t)


def _masked_attention(qm, k_loc, v_loc, add_loc, k_ctx, v_ctx, sink):
    s_ctx = _dot_nt(qm, k_ctx)
    m = jnp.max(s_ctx, axis=-1, keepdims=True)
    if k_loc is not None:
        s_loc = _dot_nt(qm, k_loc) + add_loc
        m = jnp.maximum(m, jnp.max(s_loc, axis=-1, keepdims=True))
    if sink is not None:
        m = jnp.maximum(m, sink)
    p_ctx = jnp.exp2(s_ctx - m)
    den = jnp.sum(p_ctx, axis=-1, keepdims=True)
    o = _dot(p_ctx.astype(BF16), v_ctx)
    if k_loc is not None:
        p_loc = jnp.exp2(s_loc - m)
        den = den + jnp.sum(p_loc, axis=-1, keepdims=True)
        o = o + _dot(p_loc.astype(BF16), v_loc)
    if sink is not None:
        den = den + jnp.exp2(sink - m)
    return o / den


def _pair_attention(q, k_loc, v_loc, add_loc, k_ctx, v_ctx, sinks, lane_first):
    m = q.shape[0]
    zero = jnp.zeros_like(q)
    q2 = jnp.concatenate([jnp.where(lane_first, q, zero), jnp.where(lane_first, zero, q)], axis=0)
    sink2 = None
    if sinks is not None:
        second = lax.broadcasted_iota(jnp.int32, (2 * m, 1), 0) >= m
        sink2 = jnp.where(second, sinks[1], sinks[0])
    o = _masked_attention(q2, k_loc, v_loc, add_loc, k_ctx, v_ctx, sink2)
    return jnp.where(lane_first, o[:m], o[m:])


def _lane_first(m):
    return lax.broadcasted_iota(jnp.int32, (m, LANES), 1) < HEAD_DIM


def _na_kernel(*refs, rows):
    first = pl.program_id(1) * NA_STEP_GROUPS
    last_group = rows // NA_ROWS_PER_STEP - 1
    rebuilds = (first <= 1) | (first + NA_STEP_GROUPS - 1 >= last_group)

    @pl.when(rebuilds)
    def _():
        for t in range(NA_STEP_GROUPS):
            _na_group(t, first + t, *refs, rows=rows, may_rebuild=True)

    @pl.when(jnp.logical_not(rebuilds))
    def _():
        for t in range(NA_STEP_GROUPS):
            _na_group(t, first + t, *refs, rows=rows, may_rebuild=False)


def _na_group(t, gidx, q_ref, k_ref, v_ref, kc_ref, vc_ref, tab_ref, g_ref, o_ref, y_s, bias_s, *, rows,
              may_rebuild):
    base = jnp.clip(gidx * NA_ROWS_PER_STEP - NA_WIN_R // 2, 0, rows - NA_KEY_ROWS)
    start = pl.multiple_of(base * GRID_W, GRID_W)
    nk = NA_KEY_ROWS * GRID_W
    mq = NA_ROWS_PER_STEP * GRID_W
    lane_first = _lane_first(mq)
    tile_first = _lane_first(GRID_W)

    def fill_bias(head):
        for qr in range(NA_ROWS_PER_STEP):
            r = gidx * NA_ROWS_PER_STEP + qr
            rs = jnp.clip(r - NA_WIN_R // 2, 0, rows - NA_WIN_R)
            row0 = (head % 2) * mq + qr * GRID_W
            for m in range(NA_KEY_ROWS // 2):
                idx = []
                for kr in (base + 2 * m, base + 2 * m + 1):
                    ok = (kr >= rs) & (kr < rs + NA_WIN_R)
                    idx.append(jnp.where(ok, kr - r + NA_WIN_R - 1, NA_BIAS_MASKED))
                bias_s[head // 2, row0:row0 + GRID_W, m * LANES:(m + 1) * LANES] = jnp.where(
                    tile_first, tab_ref[head, idx[0]], tab_ref[head, idx[1]])

    if may_rebuild:
        @pl.when((gidx <= 1) | (gidx == rows // NA_ROWS_PER_STEP - 1))
        def _():
            for head in range(NA_HEADS):
                fill_bias(head)

    qrows = slice(t * mq, (t + 1) * mq)
    for j in range(NA_DIM // LANES):
        cols = slice(j * LANES, (j + 1) * LANES)
        k_loc = k_ref[pl.ds(start, nk), cols]
        v_loc = v_ref[pl.ds(start, nk), cols]
        y_s[qrows, cols] = _pair_attention(q_ref[qrows, cols], k_loc, v_loc, bias_s[j], kc_ref[:, cols],
                                           vc_ref[:, cols], None, lane_first)
    o_ref[qrows, :] = _rms(y_s[qrows, :], g_ref[...]).astype(BF16)


def _na_bias_table(rpb):
    n_heads, n_dr, n_dc = rpb.shape
    pad = GRID_W - NA_WIN_C
    rp = jnp.pad(rpb.astype(F32), ((0, 0), (0, 0), (pad, pad)))
    toep = jnp.stack([rp[:, :, GRID_W - 1 - qc:2 * GRID_W - 1 - qc] for qc in range(GRID_W)], axis=2)
    qc = np.arange(GRID_W)
    cstart = np.clip(qc - NA_WIN_C // 2, 0, GRID_W - NA_WIN_C)
    col_ok = (qc[None, :] >= cstart[:, None]) & (qc[None, :] < cstart[:, None] + NA_WIN_C)
    tab = jnp.where(col_ok[None, None], toep * LOG2E, NEG_INF)
    masked = jnp.full((n_heads, NA_BIAS_MASKED + 1 - n_dr, GRID_W, GRID_W), NEG_INF, F32)
    tab = jnp.concatenate([tab, masked], axis=1)
    return jnp.tile(tab, (1, 1, 1, LANES // GRID_W))


def _na_attention(qn, kn, vn, kn_c, vn_c, tab, g_b, *, bsz, seq_len, ctx_len):
    rows = seq_len // GRID_W
    n_groups = rows // (NA_ROWS_PER_STEP * NA_STEP_GROUPS)
    mq = NA_ROWS_PER_STEP * GRID_W * NA_STEP_GROUPS
    kern = functools.partial(_na_kernel, rows=rows)
    return pl.pallas_call(
        kern,
        grid=(bsz, n_groups),
        in_specs=[pl.BlockSpec((mq, NA_DIM), lambda b, g: (b * n_groups + g, 0)),
                  pl.BlockSpec((seq_len, NA_DIM), lambda b, g: (b, 0)),
                  pl.BlockSpec((seq_len, NA_DIM), lambda b, g: (b, 0)),
                  pl.BlockSpec((ctx_len, NA_DIM), lambda b, g: (b, 0)),
                  pl.BlockSpec((ctx_len, NA_DIM), lambda b, g: (b, 0)),
                  _resident_spec(tab.shape),
                  _const_spec((1, NA_DIM))],
        out_specs=pl.BlockSpec((mq, NA_DIM), lambda b, g: (b * n_groups + g, 0)),
        out_shape=jax.ShapeDtypeStruct(qn.shape, BF16),
        scratch_shapes=[pltpu.VMEM((mq, NA_DIM), F32),
                        pltpu.VMEM((NA_HEADS // 2, 2 * NA_ROWS_PER_STEP * GRID_W, NA_KEY_ROWS * GRID_W), F32)],
        compiler_params=_params(("arbitrary", "arbitrary")),
        name="na_attention",
    )(qn, kn, vn, kn_c, vn_c, tab, g_b)


def _sw_kernel(*refs, seq_len):
    for t in range(SW_STEP_BLOCKS):
        _sw_block(t, pl.program_id(1) * SW_STEP_BLOCKS + t, *refs, seq_len=seq_len)


def _sw_block(t, i, sink_ref, q_ref, k_ref, v_ref, kc_ref, vc_ref, g_ref, o_ref, y_s, *, seq_len):
    s0 = i * SW_QBLOCK
    kstart = pl.multiple_of(jnp.clip(s0 - SW_RADIUS, 0, seq_len - SW_SPAN), SW_RADIUS)
    k_loc = k_ref[pl.ds(kstart, SW_SPAN), :]
    v_loc = v_ref[pl.ds(kstart, SW_SPAN), :]
    qpos = s0 + lax.broadcasted_iota(jnp.int32, (SW_QBLOCK, SW_SPAN), 0)
    kpos = kstart + lax.broadcasted_iota(jnp.int32, (SW_QBLOCK, SW_SPAN), 1)
    band = jnp.where(jnp.abs(kpos - qpos) <= SW_RADIUS, 0.0, NEG_INF).astype(F32)
    band2 = jnp.concatenate([band, band], axis=0)
    lane_first = _lane_first(SW_QBLOCK)
    pairs = SW_HEADS // SW_KV_HEADS
    qrows = slice(t * SW_QBLOCK, (t + 1) * SW_QBLOCK)
    for j in range(pairs):
        cols = slice(j * LANES, (j + 1) * LANES)
        sinks = (sink_ref[j], sink_ref[j + pairs])
        y_s[qrows, cols] = _pair_attention(q_ref[qrows, cols], k_loc, v_loc, band2, kc_ref[...], vc_ref[...],
                                           sinks, lane_first)
    o_ref[qrows, :] = _rms(y_s[qrows, :], g_ref[...]).astype(BF16)


def _sw_attention(sink, qs, ks, vs, ks_c, vs_c, g_c, *, bsz, seq_len, ctx_len):
    step_tokens = SW_QBLOCK * SW_STEP_BLOCKS
    nblk = seq_len // step_tokens
    kern = functools.partial(_sw_kernel, seq_len=seq_len)
    return pl.pallas_call(
        kern,
        grid=(bsz, nblk),
        in_specs=[pl.BlockSpec(memory_space=pltpu.SMEM),
                  pl.BlockSpec((step_tokens, SW_DIM), lambda b, i: (b * nblk + i, 0)),
                  pl.BlockSpec((seq_len, SW_KV_DIM), lambda b, i: (b, 0)),
                  pl.BlockSpec((seq_len, SW_KV_DIM), lambda b, i: (b, 0)),
                  pl.BlockSpec((ctx_len, SW_KV_DIM), lambda b, i: (b, 0)),
                  pl.BlockSpec((ctx_len, SW_KV_DIM), lambda b, i: (b, 0)),
                  _const_spec((1, SW_DIM))],
        out_specs=pl.BlockSpec((step_tokens, SW_DIM), lambda b, i: (b * nblk + i, 0)),
        out_shape=jax.ShapeDtypeStruct(qs.shape, BF16),
        scratch_shapes=[pltpu.VMEM((step_tokens, SW_DIM), F32)],
        compiler_params=_params(("arbitrary", "arbitrary")),
        name="sw_attention",
    )(sink, qs, ks, vs, ks_c, vs_c, g_c)


def _ctx_attn_kernel(sink_ref, qn_ref, kn_ref, vn_ref, qs_ref, ks_ref, vs_ref, gb_ref, gc_ref,
                     yb_ref, yc_ref, y_s, *, ctx_len):
    lane_first = _lane_first(ctx_len)
    for j in range(NA_DIM // LANES):
        cols = slice(j * LANES, (j + 1) * LANES)
        y_s[:, cols] = _pair_attention(qn_ref[:, cols], None, None, None, kn_ref[:, cols], vn_ref[:, cols],
                                       None, lane_first)
    yb_ref[...] = _rms(y_s[...], gb_ref[...]).astype(BF16)
    pairs = SW_HEADS // SW_KV_HEADS
    for j in range(pairs):
        cols = slice(j * LANES, (j + 1) * LANES)
        sinks = (sink_ref[j], sink_ref[j + pairs])
        y_s[:, cols] = _pair_attention(qs_ref[:, cols], None, None, None, ks_ref[...], vs_ref[...],
                                       sinks, lane_first)
    yc_ref[...] = _rms(y_s[...], gc_ref[...]).astype(BF16)


def _ctx_attention(sink, qn, kn, vn, qs, ks, vs, g_b, g_c, *, bsz, ctx_len):
    kern = functools.partial(_ctx_attn_kernel, ctx_len=ctx_len)
    tok = lambda width: pl.BlockSpec((ctx_len, width), lambda b: (b, 0))
    return pl.pallas_call(
        kern,
        grid=(bsz,),
        in_specs=[pl.BlockSpec(memory_space=pltpu.SMEM),
                  tok(NA_DIM), tok(NA_DIM), tok(NA_DIM), tok(SW_DIM), tok(SW_KV_DIM), tok(SW_KV_DIM),
                  _const_spec((1, NA_DIM)), _const_spec((1, SW_DIM))],
        out_specs=[tok(NA_DIM), tok(SW_DIM)],
        out_shape=[jax.ShapeDtypeStruct(qn.shape, BF16), jax.ShapeDtypeStruct(qs.shape, BF16)],
        scratch_shapes=[pltpu.VMEM((ctx_len, NA_DIM), F32)],
        compiler_params=_params(("arbitrary",)),
        name="ctx_attention",
    )(sink, qn, kn, vn, qs, ks, vs, g_b, g_c)


def _mix_residual(ya_ref, yb_ref, yc_ref, x_ref, g1_ref, sh_ref, sc_ref, gn_ref, wa_ref, wb_ref, wc_ref):
    mix = _dot(ya_ref[...], wa_ref[...]) + _dot(yb_ref[...], wb_ref[...]) + _dot(yc_ref[...], wc_ref[...])
    xl = x_ref[...] + g1_ref[0] * mix
    h = _rms(xl, gn_ref[...]) * (1.0 + sc_ref[0]) + sh_ref[0]
    return xl, h


def _mix_specs(x2d, mods, group_of_tile, g_norm, ys, ws, tm):
    d = x2d.shape[1]
    tok = lambda width: pl.BlockSpec((tm, width), lambda i: (i, 0))
    mod = lambda j: pl.BlockSpec((1, 1, d), lambda i: (group_of_tile(i), 0, j))
    in_specs = ([tok(y.shape[1]) for y in ys] + [tok(d), mod(2), mod(3), mod(4), _const_spec((1, d))]
                + [_resident_spec(w.shape) for w in ws])
    return in_specs, [*ys, x2d, mods, mods, mods, g_norm, *ws]


def _outproj_router_kernel(*refs):
    *mix_refs, r_ref, xo_ref, h_ref, lg_ref = refs
    xl, h = _mix_residual(*mix_refs)
    xo_ref[...] = xl
    hi = h.astype(BF16)
    h_ref[...] = hi
    tm = h.shape[0]
    lo = (h - hi.astype(F32)).astype(BF16)
    r = _dot(jnp.concatenate([hi, lo], axis=0), r_ref[...])
    lg_ref[...] = (r[:tm, :LOGIT_LANES] + r[:tm, LOGIT_LANES:]) + (r[tm:, :LOGIT_LANES] + r[tm:, LOGIT_LANES:])


def _outproj_router(ys, x2d, mods, group_of_tile, g_norm, ws, router_w, *, tm):
    t, d = x2d.shape
    in_specs, args = _mix_specs(x2d, mods, group_of_tile, g_norm, ys, ws, tm)
    tok = lambda width: pl.BlockSpec((tm, width), lambda i: (i, 0))
    return pl.pallas_call(
        _outproj_router_kernel,
        grid=(t // tm,),
        in_specs=in_specs + [_resident_spec(router_w.shape)],
        out_specs=[tok(d), tok(d), tok(LOGIT_LANES)],
        out_shape=[jax.ShapeDtypeStruct((t, d), F32), jax.ShapeDtypeStruct((t, d), BF16),
                   jax.ShapeDtypeStruct((t, LOGIT_LANES), F32)],
        compiler_params=_params(("arbitrary",)),
        name="out_projection_router",
    )(*args, router_w)


def _outproj_ffn_kernel(*refs, chunks):
    *mix_refs, g2_ref, wg_ref, wu_ref, wd_ref, o_ref = refs
    xl, h = _mix_residual(*mix_refs)
    h = h.astype(BF16)
    acc = None
    for lo, hi in chunks:
        a = _silu(_dot(h, wg_ref[:, lo:hi].astype(BF16))) * _dot(h, wu_ref[:, lo:hi].astype(BF16))
        part = _dot(a.astype(BF16), wd_ref[lo:hi, :].astype(BF16))
        acc = part if acc is None else acc + part
    o_ref[...] = xl + g2_ref[0] * acc


def _outproj_ffn(ys, x2d, mods, group_of_tile, g_norm, ws, wg, wu, wd, *, tm):
    t, d = x2d.shape
    d_ff = wg.shape[1]
    step = 1024
    chunks = tuple((lo, min(lo + step, d_ff)) for lo in range(0, d_ff, step))
    in_specs, args = _mix_specs(x2d, mods, group_of_tile, g_norm, ys, ws, tm)
    return pl.pallas_call(
        functools.partial(_outproj_ffn_kernel, chunks=chunks),
        grid=(t // tm,),
        in_specs=in_specs + [pl.BlockSpec((1, 1, d), lambda i: (group_of_tile(i), 0, 5)),
                             _resident_spec(wg.shape), _resident_spec(wu.shape), _resident_spec(wd.shape)],
        out_specs=pl.BlockSpec((tm, d), lambda i: (i, 0)),
        out_shape=jax.ShapeDtypeStruct((t, d), F32),
        compiler_params=_params(("arbitrary",)),
        name="out_projection_ffn",
    )(*args, mods, wg, wu, wd)


def _route_kernel(*refs):
    for j in range(MOE_STEP_CHUNKS):
        _route_chunk(j, *refs)


def _route_chunk(j, lg_ref, h_ref, hl_ref, pos_ref, gsel_ref, meta_ref):
    toks = slice(j * MOE_CHUNK, (j + 1) * MOE_CHUNK)
    logits = lg_ref[toks, :]
    lane = lax.broadcasted_iota(jnp.int32, logits.shape, 1)
    lg = jnp.where(lane < N_EXPERTS, logits, -jnp.inf)
    v1 = jnp.max(lg, axis=-1, keepdims=True)
    i1 = jnp.min(jnp.where(lg == v1, lane, LOGIT_LANES), axis=-1, keepdims=True)
    lg2 = jnp.where(lane == i1, -jnp.inf, lg)
    v2 = jnp.max(lg2, axis=-1, keepdims=True)
    i2 = jnp.min(jnp.where(lg2 == v2, lane, LOGIT_LANES), axis=-1, keepdims=True)
    e2 = jnp.exp(v2 - v1)
    den = 1.0 + e2
    g_first = 1.0 / den
    g_second = e2 / den
    sel1 = lane == i1
    sel2 = lane == i2
    onehot = jnp.where(sel1, 1.0, jnp.where(sel2, 1.0, 0.0))

    r_i = lax.broadcasted_iota(jnp.int32, (MOE_CHUNK, MOE_CHUNK), 0)
    c_i = lax.broadcasted_iota(jnp.int32, (MOE_CHUNK, MOE_CHUNK), 1)
    earlier = jnp.where(c_i < r_i, 1.0, 0.0).astype(BF16)
    within = _dot(earlier, onehot.astype(BF16))
    count = jnp.sum(onehot, axis=0, keepdims=True)
    n_gran = jnp.floor((count + (MOE_GRANULE - 1)) * (1.0 / MOE_GRANULE))
    u_r = lax.broadcasted_iota(jnp.int32, (LOGIT_LANES, LOGIT_LANES), 0)
    u_c = lax.broadcasted_iota(jnp.int32, (LOGIT_LANES, LOGIT_LANES), 1)
    before = jnp.where(u_r < u_c, 1.0, 0.0).astype(BF16)
    n_gran8 = jnp.broadcast_to(n_gran, (SUBLANES, LOGIT_LANES))
    off_gran = _dot(n_gran8.astype(BF16), before)[0:1]
    lpos = off_gran * MOE_GRANULE + within
    p_first = jnp.sum(jnp.where(sel1, lpos, 0.0), axis=-1, keepdims=True)
    p_second = jnp.sum(jnp.where(sel2, lpos, 0.0), axis=-1, keepdims=True)
    pos = jnp.where(lane == 0, p_first, jnp.where(lane == 1, p_second, -1.0))
    pos_ref[toks, :] = pos
    gsel_ref[toks, :] = jnp.where(lane == 0, g_first, jnp.where(lane == 1, g_second, 0.0))
    rows8 = lax.broadcasted_iota(jnp.int32, (SUBLANES, LOGIT_LANES), 0)
    meta_ref[j] = jnp.where(rows8 == 0, off_gran, jnp.where(rows8 == 1, n_gran, 0.0)).astype(jnp.int32)

    pos_t = pos.T
    row_id = lax.broadcasted_iota(jnp.int32, (MOE_LOCAL_ROWS, MOE_CHUNK), 0).astype(F32)
    pick = jnp.where(row_id == pos_t[0:1], 1.0, jnp.where(row_id == pos_t[1:2], 1.0, 0.0)).astype(BF16)
    hl_ref[j] = _dot(pick, h_ref[toks, :]).astype(BF16)


def _route(logits, h2):
    t, d = h2.shape
    n_chunks = t // MOE_CHUNK
    per_step = MOE_STEP_CHUNKS
    tok = lambda width: pl.BlockSpec((per_step * MOE_CHUNK, width), lambda i: (i, 0))
    return pl.pallas_call(
        _route_kernel,
        grid=(n_chunks // per_step,),
        in_specs=[tok(LOGIT_LANES), tok(d)],
        out_specs=[pl.BlockSpec((per_step, MOE_LOCAL_ROWS, d), lambda i: (i, 0, 0)),
                   tok(LOGIT_LANES), tok(LOGIT_LANES),
                   pl.BlockSpec((per_step, SUBLANES, LOGIT_LANES), lambda i: (i, 0, 0))],
        out_shape=[jax.ShapeDtypeStruct((n_chunks, MOE_LOCAL_ROWS, d), BF16),
                   jax.ShapeDtypeStruct((t, LOGIT_LANES), F32),
                   jax.ShapeDtypeStruct((t, LOGIT_LANES), F32),
                   jax.ShapeDtypeStruct((n_chunks, SUBLANES, LOGIT_LANES), jnp.int32)],
        compiler_params=_params(("arbitrary",)),
        name="moe_route",
    )(logits, h2)


def _experts_kernel(meta_ref, hl_hbm, wg_ref, wu_ref, wd_ref, yl_hbm, he_s, ye_s, yb_s, zero_s,
                    gather_sem, store_sem, fill_sem, *, n_chunks):
    ws = pl.program_id(0)
    e = pl.program_id(1)
    f = pl.program_id(2)
    n_e = pl.num_programs(1)
    n_f = pl.num_programs(2)
    step = ws * n_e + e
    last_step = pl.num_programs(0) * n_e - 1

    def meta(ws_, e_, c, k):
        return meta_ref[((ws_ * n_chunks + c) * 2 + k) * N_EXPERTS + e_]

    def rows_of(granules):
        return pl.multiple_of(granules * MOE_GRANULE, MOE_GRANULE)

    def for_each_run(ws_, e_, fn):
        dst = 0
        for c in range(n_chunks):
            src, n = meta(ws_, e_, c, 0), meta(ws_, e_, c, 1)

            @pl.when(n > 0)
            def _(c=c, src=src, n=n, dst=dst):
                fn(ws_ * n_chunks + c, rows_of(src), rows_of(dst), rows_of(n))
            dst = dst + n
        return dst * MOE_GRANULE

    def gather_copy(slot):
        return lambda chunk, src, dst, rows: pltpu.make_async_copy(
            hl_hbm.at[chunk, pl.ds(src, rows)], he_s.at[slot, pl.ds(dst, rows)], gather_sem.at[slot])

    def store_copy(chunk, src, dst, rows):
        return pltpu.make_async_copy(yb_s.at[pl.ds(dst, rows)], yl_hbm.at[chunk, pl.ds(src, rows)], store_sem)

    def fill_tails(start):
        for c in range(n_chunks):
            used = meta(ws, N_EXPERTS - 1, c, 0) + meta(ws, N_EXPERTS - 1, c, 1)
            tail = MOE_LOCAL_ROWS // MOE_GRANULE - used

            @pl.when(tail > 0)
            def _(c=c, used=used, tail=tail):
                cp = pltpu.make_async_copy(zero_s.at[pl.ds(0, rows_of(tail))],
                                           yl_hbm.at[ws * n_chunks + c, pl.ds(rows_of(used), rows_of(tail))],
                                           fill_sem)
                cp.start() if start else cp.wait()

    slot = step % 2

    @pl.when(f == 0)
    def _():
        @pl.when(step == 0)
        def _():
            he_s[...] = jnp.zeros_like(he_s)
            zero_s[...] = jnp.zeros_like(zero_s)
            for_each_run(ws, e, lambda *run: gather_copy(slot)(*run).start())

        @pl.when(e == 0)
        def _():
            fill_tails(True)
            fill_tails(False)

        for_each_run(ws, e, lambda *run: gather_copy(slot)(*run).wait())

        @pl.when(step < last_step)
        def _():
            nxt = step + 1
            for_each_run(nxt // n_e, nxt % n_e, lambda *run: gather_copy(1 - slot)(*run).start())

    def m_tile(start, size, mode):
        rows = pl.ds(pl.multiple_of(start, MOE_M_MIN), size)
        hrow = he_s[slot, rows, :]
        a = _silu(_dot(hrow, wg_ref[0].astype(BF16))) * _dot(hrow, wu_ref[0].astype(BF16))
        part = _dot(a.astype(BF16), wd_ref[0].astype(BF16))
        if mode == "first":
            ye_s[rows, :] = part
        elif mode == "middle":
            ye_s[rows, :] += part
        else:
            yb_s[rows, :] = (ye_s[rows, :] + part).astype(BF16)

    n_rows = sum(meta(ws, e, c, 1) for c in range(n_chunks)) * MOE_GRANULE
    n_full = n_rows // MOE_M_TILE
    rem = n_rows - n_full * MOE_M_TILE

    def all_tiles(mode):
        def tile_pair(m, carry):
            m_tile(2 * m * MOE_M_TILE, MOE_M_TILE, mode)
            m_tile((2 * m + 1) * MOE_M_TILE, MOE_M_TILE, mode)
            return carry
        lax.fori_loop(0, n_full // 2, tile_pair, 0)

        @pl.when(n_full % 2 == 1)
        def _():
            m_tile((n_full - 1) * MOE_M_TILE, MOE_M_TILE, mode)

        size = MOE_M_TILE
        while size >= MOE_M_MIN:
            lo = 0 if size == MOE_M_MIN else size // 2

            @pl.when((rem > lo) & (rem <= size))
            def _(size=size):
                m_tile(n_full * MOE_M_TILE, size, mode)
            size //= 2

    @pl.when(f == 0)
    def _():
        all_tiles("first")

    @pl.when((f > 0) & (f < n_f - 1))
    def _():
        all_tiles("middle")

    @pl.when(f == n_f - 1)
    def _():
        @pl.when(step > 0)
        def _():
            prev = step - 1
            for_each_run(prev // n_e, prev % n_e, lambda *run: store_copy(*run).wait())

        all_tiles("last")
        for_each_run(ws, e, lambda *run: store_copy(*run).start())

        @pl.when(step == last_step)
        def _():
            for_each_run(ws, e, lambda *run: store_copy(*run).wait())


def _experts(meta, hl, wg, wu, wd, *, n_chunks):
    total_chunks, local_rows, d = hl.shape
    n_exp, _, d_ff = wg.shape
    tf = MOE_FF_TILE
    assert d_ff // tf >= 2, "the first and the last hidden tile must be different grid steps"
    max_rows = n_chunks * MOE_CHUNK
    grid_spec = pltpu.PrefetchScalarGridSpec(
        num_scalar_prefetch=1,
        grid=(total_chunks // n_chunks, n_exp, d_ff // tf),
        in_specs=[pl.BlockSpec(memory_space=pl.ANY),
                  pl.BlockSpec((1, d, tf), lambda i, e, f, m: (e, 0, f)),
                  pl.BlockSpec((1, d, tf), lambda i, e, f, m: (e, 0, f)),
                  pl.BlockSpec((1, tf, d), lambda i, e, f, m: (e, f, 0))],
        out_specs=pl.BlockSpec(memory_space=pl.ANY),
        scratch_shapes=[pltpu.VMEM((2, max_rows, d), BF16), pltpu.VMEM((max_rows, d), F32),
                        pltpu.VMEM((max_rows, d), BF16), pltpu.VMEM((local_rows, d), BF16),
                        pltpu.SemaphoreType.DMA((2,)), pltpu.SemaphoreType.DMA(()), pltpu.SemaphoreType.DMA(())])
    return pl.pallas_call(
        functools.partial(_experts_kernel, n_chunks=n_chunks),
        grid_spec=grid_spec,
        out_shape=jax.ShapeDtypeStruct(hl.shape, BF16),
        compiler_params=_params(("arbitrary", "arbitrary", "arbitrary")),
        name="moe_experts",
    )(meta, hl, wg, wu, wd)


def _scatter_kernel(yl_ref, pos_ref, gsel_ref, x_ref, g2_ref, gf_ref, o_ref):
    row_id = lax.broadcasted_iota(jnp.int32, (MOE_CHUNK, MOE_LOCAL_ROWS), 1).astype(F32)
    for j in range(MOE_STEP_CHUNKS):
        toks = slice(j * MOE_CHUNK, (j + 1) * MOE_CHUNK)
        pos = pos_ref[toks, :]
        gsel = gsel_ref[toks, :]
        yl = yl_ref[j]
        y = None
        for k in range(2):
            pick = jnp.where(pos[:, k:k + 1] == row_id, 1.0, 0.0).astype(BF16)
            term = gsel[:, k:k + 1] * _dot(pick, yl)
            y = term if y is None else y + term
        o_ref[toks, :] = _rms(x_ref[toks, :] + g2_ref[0] * y, gf_ref[...])


def _scatter_final(yl, pos, gsel, x2d, mods, group_of_step, g_final):
    t, d = x2d.shape
    per_step = MOE_STEP_CHUNKS
    tok = lambda width: pl.BlockSpec((per_step * MOE_CHUNK, width), lambda i: (i, 0))
    return pl.pallas_call(
        _scatter_kernel,
        grid=(t // (per_step * MOE_CHUNK),),
        in_specs=[pl.BlockSpec((per_step, MOE_LOCAL_ROWS, d), lambda i: (i, 0, 0)),
                  tok(LOGIT_LANES), tok(LOGIT_LANES), tok(d),
                  pl.BlockSpec((1, 1, d), lambda i: (group_of_step(i), 0, 5)), _const_spec((1, d))],
        out_specs=tok(d),
        out_shape=jax.ShapeDtypeStruct((t, d), F32),
        compiler_params=_params(("arbitrary",)),
        name="moe_scatter_final",
    )(yl, pos, gsel, x2d, mods, g_final)


def _rope_tables(n):
    t = jnp.arange(n)
    row = (t // GRID_W).astype(F32)
    col = (t % GRID_W).astype(F32)
    n_freq = HEAD_DIM // 4
    inv = ROPE_THETA ** (-jnp.arange(n_freq, dtype=F32) / n_freq)
    ang = jnp.stack([row[:, None] * inv, col[:, None] * inv], axis=1)
    cos = jnp.cos(ang)[:, :, None, :]
    sin = jnp.sin(ang)[:, :, None, :]
    cos_f = jnp.broadcast_to(cos, (n, 2, 2, n_freq)).reshape(n, HEAD_DIM)
    sin_s = jnp.concatenate([-sin, sin], axis=2).reshape(n, HEAD_DIM)
    reps = LANES // HEAD_DIM
    return jnp.tile(cos_f, (1, reps)), jnp.tile(sin_s, (1, reps))


def _pair_perm():
    pairs = SW_HEADS // SW_KV_HEADS
    order = [h for j in range(pairs) for h in (j, j + pairs)]
    return np.concatenate([np.arange(h * HEAD_DIM, (h + 1) * HEAD_DIM) for h in order])


def kernel(x, c, ctx, c_ctx, w_ada, b_ada, g_norm1, g_norm2, w_in, conv_w, na_rpb, sw_sink, g_mix, w_out,
           ffn_w_gate, ffn_w_up, ffn_w_down, w_router, moe_w_gate, moe_w_up, moe_w_down, g_final):
    bsz, n, d = x.shape
    ctx_len = ctx.shape[1]
    depth = w_in.shape[0]
    assert w_in.shape[2] == D_IN and n % GRID_W == 0
    assert depth == 2, "layer 0 dense on both streams, layer 1 routed experts on the latent stream only"
    tm = 512
    tm_in = 1024
    tm_ctx = ctx_len
    set_chunks = MOE_SET_CHUNKS
    assert n % tm == 0 and n % tm_in == 0 and (n // GRID_W) % (NA_ROWS_PER_STEP * NA_STEP_GROUPS) == 0
    assert n % (SW_QBLOCK * SW_STEP_BLOCKS) == 0 and tm % MOE_CHUNK == 0
    assert n % (MOE_CHUNK * MOE_STEP_CHUNKS) == 0 and (bsz * n) % (MOE_CHUNK * set_chunks) == 0
    assert (MOE_CHUNK * set_chunks) % MOE_M_TILE == 0

    cond_rows = 2 * SUBLANES
    cond = jnp.zeros((cond_rows, d), F32).at[:bsz].set(c).at[bsz].set(c_ctx)
    mods_all = _ada(cond, w_ada, b_ada).reshape(depth, cond_rows, 1, 6 * d)

    lat_group = lambda i: i // (n // tm)
    ctx_group = lambda i: bsz
    cos_t, sin_t = _rope_tables(n)
    perm = _pair_perm()

    xl = x.reshape(bsz * n, d)
    xc = ctx.reshape(bsz * ctx_len, d)
    out = None
    for l in range(depth):
        last = l == depth - 1
        mods = mods_all[l]
        w_l = w_in[l]
        w_l = jnp.concatenate([w_l[:, :OFF_SQ], w_l[:, OFF_SQ:OFF_NK][:, perm], w_l[:, OFF_NK:]], axis=1).astype(BF16)
        g_a = g_mix[l, :CONV_CH].reshape(1, CONV_CH)
        g_b = g_mix[l, CONV_CH:CONV_CH + NA_DIM].reshape(1, NA_DIM)
        g_c = g_mix[l, CONV_CH + NA_DIM:][perm].reshape(1, SW_DIM)
        wo = w_out[l]
        wa = wo[:CONV_CH].astype(BF16)
        wb = wo[CONV_CH:CONV_CH + NA_DIM].astype(BF16)
        wc = wo[CONV_CH + NA_DIM:][perm].astype(BF16)
        gn1 = g_norm1[l].reshape(1, d)
        gn2 = g_norm2[l].reshape(1, d)
        sink = sw_sink[l].astype(F32) * LOG2E
        tab = _na_bias_table(na_rpb[l])

        ya, qn, qs, kn, vn, ks, vs = _inproj(xl, mods, lambda i: i // (n // tm_in), gn1, w_l, conv_w[l], g_a,
                                             cos_t, sin_t, seq_len=n, tm=tm_in, rope=True)
        cya, cqn, cqs, ckn, cvn, cks, cvs = _inproj(xc, mods, ctx_group, gn1, w_l, conv_w[l], g_a, cos_t, sin_t,
                                                    seq_len=ctx_len, tm=tm_ctx, rope=False)
        yb = _na_attention(qn, kn, vn, ckn, cvn, tab, g_b, bsz=bsz, seq_len=n, ctx_len=ctx_len)
        yc = _sw_attention(sink, qs, ks, vs, cks, cvs, g_c, bsz=bsz, seq_len=n, ctx_len=ctx_len)

        if l % 2 == 0:
            i = l // 2
            wg, wu, wd = ffn_w_gate[i], ffn_w_up[i], ffn_w_down[i]
            xl = _outproj_ffn((ya, yb, yc), xl, mods, lat_group, gn2, (wa, wb, wc), wg, wu, wd, tm=tm)
            if not last:
                cyb, cyc = _ctx_attention(sink, cqn, ckn, cvn, cqs, cks, cvs, g_b, g_c, bsz=bsz, ctx_len=ctx_len)
                xc = _outproj_ffn((cya, cyb, cyc), xc, mods, ctx_group, gn2, (wa, wb, wc), wg, wu, wd,
                                  tm=tm_ctx)
        else:
            i = l // 2
            wr = jnp.zeros((d, LOGIT_LANES), F32).at[:, :N_EXPERTS].set(w_router[i])
            wr_hi = wr.astype(BF16)
            wr_lo = (wr - wr_hi.astype(F32)).astype(BF16)
            wr2 = jnp.concatenate([wr_hi, wr_lo], axis=1)
            wg, wu, wd = moe_w_gate[i], moe_w_up[i], moe_w_down[i]
            xl, h2, logits = _outproj_router((ya, yb, yc), xl, mods, lambda i: i // (n // tm_in), gn2,
                                             (wa, wb, wc), wr2, tm=tm_in)
            hl, pos, gsel, meta = _route(logits, h2)
            meta = meta[:, :2, :N_EXPERTS].reshape(-1)
            yl = _experts(meta, hl, wg, wu, wd, n_chunks=set_chunks)
            step_tokens = MOE_CHUNK * MOE_STEP_CHUNKS
            out = _scatter_final(yl, pos, gsel, xl, mods, lambda i: i // (n // step_tokens), g_final.reshape(1, d))
    return out.reshape(bsz, n, d)
```

```python
import functools

import numpy as np
import jax
import jax.numpy as jnp
from jax import lax
from jax.experimental import pallas as pl
from jax.experimental.pallas import tpu as pltpu

F32 = jnp.float32
BF16 = jnp.bfloat16

GRID_W = 64
HEAD_DIM = 64
CONV_CH = 256
NA_HEADS = 6
NA_DIM = NA_HEADS * HEAD_DIM
SW_HEADS = 6
SW_KV_HEADS = 2
SW_DIM = SW_HEADS * HEAD_DIM
SW_KV_DIM = SW_KV_HEADS * HEAD_DIM
NA_WIN_R = 8
NA_WIN_C = 16
SW_RADIUS = 128
ROPE_THETA = 10000.0
N_EXPERTS = 8
EPS = 1e-6
NEG_INF = -1e30
OFF_H, OFF_B, OFF_C = 0, CONV_CH, 2 * CONV_CH
OFF_NQ = 3 * CONV_CH
OFF_SQ = OFF_NQ + NA_DIM
OFF_NK = OFF_SQ + SW_DIM
OFF_NV = OFF_NK + NA_DIM
OFF_SK = OFF_NV + NA_DIM
OFF_SV = OFF_SK + SW_KV_DIM
D_IN = OFF_SV + SW_KV_DIM
LOG2E = 1.4426950408889634
Q_SCALE = HEAD_DIM ** -0.5 * LOG2E

LANES = 128
SUBLANES = 8
VMEM_BYTES_V7X = 64 * 1024 * 1024
VMEM_LIMIT = VMEM_BYTES_V7X * 7 // 8

NA_ROWS_PER_STEP = 4
NA_KEY_ROWS = NA_ROWS_PER_STEP + NA_WIN_R
NA_BIAS_MASKED = 2 * NA_WIN_R - 1
MOE_CHUNK = 256
MOE_GRANULE = 2 * SUBLANES
MOE_LOCAL_ROWS = 640
MOE_SET_CHUNKS = 16
MOE_STEP_CHUNKS = 4
MOE_M_TILE = 512
MOE_M_MIN = 128
MOE_FF_TILE = 512
NA_STEP_GROUPS = 2
SW_QBLOCK = 256
SW_STEP_BLOCKS = 8
SW_SPAN = SW_QBLOCK + 2 * SW_RADIUS
LOGIT_LANES = LANES


def _params(semantics):
    return pltpu.CompilerParams(dimension_semantics=semantics, vmem_limit_bytes=VMEM_LIMIT)


def _const_spec(shape):
    nd = len(shape)
    return pl.BlockSpec(shape, lambda *_: (0,) * nd)


def _resident_spec(shape):
    nd = len(shape)
    return pl.BlockSpec(shape, lambda *_: (0,) * nd, pipeline_mode=pl.Buffered(1))


def _rms(x, g):
    y = x * lax.rsqrt(jnp.mean(x * x, axis=-1, keepdims=True) + EPS)
    return y * g


def _silu(x):
    return x / (1.0 + jnp.exp(-x))


def _dot(a, b):
    return jnp.dot(a, b, preferred_element_type=F32)


def _dot_nt(a, b):
    return lax.dot_general(a, b, (((1,), (1,)), ((), ())), preferred_element_type=F32)


def _ada_kernel(c_ref, w_ref, b_ref, o_ref):
    a = _silu(c_ref[...])
    o_ref[0] = jnp.dot(a, w_ref[0], preferred_element_type=F32,
                       precision=lax.Precision.HIGHEST) + b_ref[0]


def _ada(cond, w_ada, b_ada):
    depth, d, n_out = w_ada.shape
    rows = cond.shape[0]
    tn = n_out // 6
    return pl.pallas_call(
        _ada_kernel,
        grid=(depth, n_out // tn),
        in_specs=[pl.BlockSpec((rows, d), lambda l, j: (0, 0)),
                  pl.BlockSpec((1, d, tn), lambda l, j: (l, 0, j)),
                  pl.BlockSpec((1, 1, tn), lambda l, j: (l, 0, j))],
        out_specs=pl.BlockSpec((1, rows, tn), lambda l, j: (l, 0, j)),
        out_shape=jax.ShapeDtypeStruct((depth, rows, n_out), F32),
        compiler_params=_params(("arbitrary", "arbitrary")),
        name="ada_modulation",
    )(cond, w_ada, b_ada.reshape(depth, 1, n_out))


def _rope(x, cos, sin, lane_lo):
    rot = jnp.where(lane_lo, pltpu.roll(x, LANES - 16, 1), pltpu.roll(x, 16, 1))
    return x * cos + rot * sin


def _inproj_kernel(x_ref, xp_ref, xn_ref, sh_ref, sc_ref, gn_ref, w_ref, cw_ref, ga_ref, cos_ref, sin_ref,
                   ya_ref, qn_ref, qs_ref, kn_ref, vn_ref, ks_ref, vs_ref, u_s, *, tm, tiles_per_seq, rope):
    i = pl.program_id(0)
    g = gn_ref[...]
    sh = sh_ref[0]
    sc = sc_ref[0]

    def normmod(x):
        return _rms(x, g) * (1.0 + sc) + sh

    h = normmod(x_ref[...]).astype(BF16)
    p = _dot(h, w_ref[...])

    xh = jnp.concatenate([xp_ref[...], xn_ref[...]], axis=0)
    ph = _dot(normmod(xh).astype(BF16), w_ref[:, :OFF_NQ])
    uh = ph[:, OFF_C:OFF_C + CONV_CH] * ph[:, OFF_H:OFF_H + CONV_CH]
    pos = i % tiles_per_seq
    u = p[:, OFF_C:OFF_C + CONV_CH] * p[:, OFF_H:OFF_H + CONV_CH]
    u_s[0:SUBLANES] = jnp.where(pos == 0, 0.0, uh[0:SUBLANES])
    u_s[SUBLANES:SUBLANES + tm] = u
    u_s[SUBLANES + tm:2 * SUBLANES + tm] = jnp.where(pos == tiles_per_seq - 1, 0.0, uh[SUBLANES:])
    cw = cw_ref[...]
    conv = (u_s[SUBLANES - 1:SUBLANES - 1 + tm] * cw[0:1] + u * cw[1:2]
            + u_s[SUBLANES + 1:SUBLANES + 1 + tm] * cw[2:3])
    ya = p[:, OFF_B:OFF_B + CONV_CH] * conv
    ya_ref[...] = _rms(ya, ga_ref[...]).astype(BF16)

    qn_ref[...] = (p[:, OFF_NQ:OFF_NQ + NA_DIM] * Q_SCALE).astype(BF16)
    kn_ref[...] = p[:, OFF_NK:OFF_NK + NA_DIM].astype(BF16)
    vn_ref[...] = p[:, OFF_NV:OFF_NV + NA_DIM].astype(BF16)
    vs_ref[...] = p[:, OFF_SV:OFF_SV + SW_KV_DIM].astype(BF16)
    if rope:
        cos = cos_ref[...]
        sin = sin_ref[...]
        lane_lo = (lax.broadcasted_iota(jnp.int32, (tm, LANES), 1) % 32) < 16
        for j in range(SW_DIM // LANES):
            q = p[:, OFF_SQ + j * LANES:OFF_SQ + (j + 1) * LANES]
            qs_ref[:, j * LANES:(j + 1) * LANES] = (_rope(q, cos, sin, lane_lo) * Q_SCALE).astype(BF16)
        ks_ref[...] = _rope(p[:, OFF_SK:OFF_SK + SW_KV_DIM], cos, sin, lane_lo).astype(BF16)
    else:
        qs_ref[...] = (p[:, OFF_SQ:OFF_SQ + SW_DIM] * Q_SCALE).astype(BF16)
        ks_ref[...] = p[:, OFF_SK:OFF_SK + SW_KV_DIM].astype(BF16)


def _inproj(x2d, mods, group_of_tile, g_norm, w, conv_w, g_a, cos_t, sin_t, *, seq_len, tm, rope):
    t, d = x2d.shape
    tiles_per_seq = seq_len // tm
    blk8 = tm // SUBLANES
    last8 = t // SUBLANES - 1
    kern = functools.partial(_inproj_kernel, tm=tm, tiles_per_seq=tiles_per_seq, rope=rope)
    tok = lambda width: pl.BlockSpec((tm, width), lambda i: (i, 0))
    out_widths = (CONV_CH, NA_DIM, SW_DIM, NA_DIM, NA_DIM, SW_KV_DIM, SW_KV_DIM)
    return pl.pallas_call(
        kern,
        grid=(t // tm,),
        in_specs=[tok(d),
                  pl.BlockSpec((SUBLANES, d), lambda i: (jnp.maximum(i * blk8 - 1, 0), 0)),
                  pl.BlockSpec((SUBLANES, d), lambda i: (jnp.minimum((i + 1) * blk8, last8), 0)),
                  pl.BlockSpec((1, 1, d), lambda i: (group_of_tile(i), 0, 0)),
                  pl.BlockSpec((1, 1, d), lambda i: (group_of_tile(i), 0, 1)),
                  _const_spec((1, d)),
                  _resident_spec(w.shape),
                  _const_spec(conv_w.shape),
                  _const_spec((1, CONV_CH)),
                  pl.BlockSpec((tm, LANES), lambda i: (i % tiles_per_seq, 0)),
                  pl.BlockSpec((tm, LANES), lambda i: (i % tiles_per_seq, 0))],
        out_specs=[tok(wd) for wd in out_widths],
        out_shape=[jax.ShapeDtypeStruct((t, wd), BF16) for wd in out_widths],
        scratch_shapes=[pltpu.VMEM((tm + 2 * SUBLANES, CONV_CH), F32)],
        compiler_params=_params(("arbitrary",)),
        name="in_projection",
    )(x2d, x2d, x2d, mods, mods, g_norm, w, conv_w, g_a, cos_t, sin_t)


def _masked_attention(qm, k_loc, v_loc, add_loc, k_ctx, v_ctx, sink):
    s_ctx = _dot_nt(qm, k_ctx)
    m = jnp.max(s_ctx, axis=-1, keepdims=True)
    if k_loc is not None:
        s_loc = _dot_nt(qm, k_loc) + add_loc
        m = jnp.maximum(m, jnp.max(s_loc, axis=-1, keepdims=True))
    if sink is not None:
        m = jnp.maximum(m, sink)
    p_ctx = jnp.exp2(s_ctx - m)
    den = jnp.sum(p_ctx, axis=-1, keepdims=True)
    o = _dot(p_ctx.astype(BF16), v_ctx)
    if k_loc is not None:
        p_loc = jnp.exp2(s_loc - m)
        den = den + jnp.sum(p_loc, axis=-1, keepdims=True)
        o = o + _dot(p_loc.astype(BF16), v_loc)
    if sink is not None:
        den = den + jnp.exp2(sink - m)
    return o / den


def _pair_attention(q, k_loc, v_loc, add_loc, k_ctx, v_ctx, sinks, lane_first):
    m = q.shape[0]
    zero = jnp.zeros_like(q)
    q2 = jnp.concatenate([jnp.where(lane_first, q, zero), jnp.where(lane_first, zero, q)], axis=0)
    sink2 = None
    if sinks is not None:
        second = lax.broadcasted_iota(jnp.int32, (2 * m, 1), 0) >= m
        sink2 = jnp.where(second, sinks[1], sinks[0])
    o = _masked_attention(q2, k_loc, v_loc, add_loc, k_ctx, v_ctx, sink2)
    return jnp.where(lane_first, o[:m], o[m:])


def _lane_first(m):
    return lax.broadcasted_iota(jnp.int32, (m, LANES), 1) < HEAD_DIM


def _na_kernel(*refs, rows):
    first = pl.program_id(1) * NA_STEP_GROUPS
    last_group = rows // NA_ROWS_PER_STEP - 1
    rebuilds = (first <= 1) | (first + NA_STEP_GROUPS - 1 >= last_group)

    @pl.when(rebuilds)
    def _():
        for t in range(NA_STEP_GROUPS):
            _na_group(t, first + t, *refs, rows=rows, may_rebuild=True)

    @pl.when(jnp.logical_not(rebuilds))
    def _():
        for t in range(NA_STEP_GROUPS):
            _na_group(t, first + t, *refs, rows=rows, may_rebuild=False)


def _na_group(t, gidx, q_ref, k_ref, v_ref, kc_ref, vc_ref, tab_ref, g_ref, o_ref, y_s, bias_s, *, rows,
              may_rebuild):
    base = jnp.clip(gidx * NA_ROWS_PER_STEP - NA_WIN_R // 2, 0, rows - NA_KEY_ROWS)
    start = pl.multiple_of(base * GRID_W, GRID_W)
    nk = NA_KEY_ROWS * GRID_W
    mq = NA_ROWS_PER_STEP * GRID_W
    lane_first = _lane_first(mq)
    tile_first = _lane_first(GRID_W)

    def fill_bias(head):
        for qr in range(NA_ROWS_PER_STEP):
            r = gidx * NA_ROWS_PER_STEP + qr
            rs = jnp.clip(r - NA_WIN_R // 2, 0, rows - NA_WIN_R)
            row0 = (head % 2) * mq + qr * GRID_W
            for m in range(NA_KEY_ROWS // 2):
                idx = []
                for kr in (base + 2 * m, base + 2 * m + 1):
                    ok = (kr >= rs) & (kr < rs + NA_WIN_R)
                    idx.append(jnp.where(ok, kr - r + NA_WIN_R - 1, NA_BIAS_MASKED))
                bias_s[head // 2, row0:row0 + GRID_W, m * LANES:(m + 1) * LANES] = jnp.where(
                    tile_first, tab_ref[head, idx[0]], tab_ref[head, idx[1]])

    if may_rebuild:
        @pl.when((gidx <= 1) | (gidx == rows // NA_ROWS_PER_STEP - 1))
        def _():
            for head in range(NA_HEADS):
                fill_bias(head)

    qrows = slice(t * mq, (t + 1) * mq)
    for j in range(NA_DIM // LANES):
        cols = slice(j * LANES, (j + 1) * LANES)
        k_loc = k_ref[pl.ds(start, nk), cols]
        v_loc = v_ref[pl.ds(start, nk), cols]
        y_s[qrows, cols] = _pair_attention(q_ref[qrows, cols], k_loc, v_loc, bias_s[j], kc_ref[:, cols],
                                           vc_ref[:, cols], None, lane_first)
    o_ref[qrows, :] = _rms(y_s[qrows, :], g_ref[...]).astype(BF16)


def _na_bias_table(rpb):
    n_heads, n_dr, n_dc = rpb.shape
    pad = GRID_W - NA_WIN_C
    rp = jnp.pad(rpb.astype(F32), ((0, 0), (0, 0), (pad, pad)))
    toep = jnp.stack([rp[:, :, GRID_W - 1 - qc:2 * GRID_W - 1 - qc] for qc in range(GRID_W)], axis=2)
    qc = np.arange(GRID_W)
    cstart = np.clip(qc - NA_WIN_C // 2, 0, GRID_W - NA_WIN_C)
    col_ok = (qc[None, :] >= cstart[:, None]) & (qc[None, :] < cstart[:, None] + NA_WIN_C)
    tab = jnp.where(col_ok[None, None], toep * LOG2E, NEG_INF)
    masked = jnp.full((n_heads, NA_BIAS_MASKED + 1 - n_dr, GRID_W, GRID_W), NEG_INF, F32)
    tab = jnp.concatenate([tab, masked], axis=1)
    return jnp.tile(tab, (1, 1, 1, LANES // GRID_W))


def _na_attention(qn, kn, vn, kn_c, vn_c, tab, g_b, *, bsz, seq_len, ctx_len):
    rows = seq_len // GRID_W
    n_groups = rows // (NA_ROWS_PER_STEP * NA_STEP_GROUPS)
    mq = NA_ROWS_PER_STEP * GRID_W * NA_STEP_GROUPS
    kern = functools.partial(_na_kernel, rows=rows)
    return pl.pallas_call(
        kern,
        grid=(bsz, n_groups),
        in_specs=[pl.BlockSpec((mq, NA_DIM), lambda b, g: (b * n_groups + g, 0)),
                  pl.BlockSpec((seq_len, NA_DIM), lambda b, g: (b, 0)),
                  pl.BlockSpec((seq_len, NA_DIM), lambda b, g: (b, 0)),
                  pl.BlockSpec((ctx_len, NA_DIM), lambda b, g: (b, 0)),
                  pl.BlockSpec((ctx_len, NA_DIM), lambda b, g: (b, 0)),
                  _resident_spec(tab.shape),
                  _const_spec((1, NA_DIM))],
        out_specs=pl.BlockSpec((mq, NA_DIM), lambda b, g: (b * n_groups + g, 0)),
        out_shape=jax.ShapeDtypeStruct(qn.shape, BF16),
        scratch_shapes=[pltpu.VMEM((mq, NA_DIM), F32),
                        pltpu.VMEM((NA_HEADS // 2, 2 * NA_ROWS_PER_STEP * GRID_W, NA_KEY_ROWS * GRID_W), F32)],
        compiler_params=_params(("arbitrary", "arbitrary")),
        name="na_attention",
    )(qn, kn, vn, kn_c, vn_c, tab, g_b)


def _sw_kernel(*refs, seq_len):
    for t in range(SW_STEP_BLOCKS):
        _sw_block(t, pl.program_id(1) * SW_STEP_BLOCKS + t, *refs, seq_len=seq_len)


def _sw_block(t, i, sink_ref, q_ref, k_ref, v_ref, kc_ref, vc_ref, g_ref, o_ref, y_s, *, seq_len):
    s0 = i * SW_QBLOCK
    kstart = pl.multiple_of(jnp.clip(s0 - SW_RADIUS, 0, seq_len - SW_SPAN), SW_RADIUS)
    k_loc = k_ref[pl.ds(kstart, SW_SPAN), :]
    v_loc = v_ref[pl.ds(kstart, SW_SPAN), :]
    qpos = s0 + lax.broadcasted_iota(jnp.int32, (SW_QBLOCK, SW_SPAN), 0)
    kpos = kstart + lax.broadcasted_iota(jnp.int32, (SW_QBLOCK, SW_SPAN), 1)
    band = jnp.where(jnp.abs(kpos - qpos) <= SW_RADIUS, 0.0, NEG_INF).astype(F32)
    band2 = jnp.concatenate([band, band], axis=0)
    lane_first = _lane_first(SW_QBLOCK)
    pairs = SW_HEADS // SW_KV_HEADS
    qrows = slice(t * SW_QBLOCK, (t + 1) * SW_QBLOCK)
    for j in range(pairs):
        cols = slice(j * LANES, (j + 1) * LANES)
        sinks = (sink_ref[j], sink_ref[j + pairs])
        y_s[qrows, cols] = _pair_attention(q_ref[qrows, cols], k_loc, v_loc, band2, kc_ref[...], vc_ref[...],
                                           sinks, lane_first)
    o_ref[qrows, :] = _rms(y_s[qrows, :], g_ref[...]).astype(BF16)


def _sw_attention(sink, qs, ks, vs, ks_c, vs_c, g_c, *, bsz, seq_len, ctx_len):
    step_tokens = SW_QBLOCK * SW_STEP_BLOCKS
    nblk = seq_len // step_tokens
    kern = functools.partial(_sw_kernel, seq_len=seq_len)
    return pl.pallas_call(
        kern,
        grid=(bsz, nblk),
        in_specs=[pl.BlockSpec(memory_space=pltpu.SMEM),
                  pl.BlockSpec((step_tokens, SW_DIM), lambda b, i: (b * nblk + i, 0)),
                  pl.BlockSpec((seq_len, SW_KV_DIM), lambda b, i: (b, 0)),
                  pl.BlockSpec((seq_len, SW_KV_DIM), lambda b, i: (b, 0)),
                  pl.BlockSpec((ctx_len, SW_KV_DIM), lambda b, i: (b, 0)),
                  pl.BlockSpec((ctx_len, SW_KV_DIM), lambda b, i: (b, 0)),
                  _const_spec((1, SW_DIM))],
        out_specs=pl.BlockSpec((step_tokens, SW_DIM), lambda b, i: (b * nblk + i, 0)),
        out_shape=jax.ShapeDtypeStruct(qs.shape, BF16),
        scratch_shapes=[pltpu.VMEM((step_tokens, SW_DIM), F32)],
        compiler_params=_params(("arbitrary", "arbitrary")),
        name="sw_attention",
    )(sink, qs, ks, vs, ks_c, vs_c, g_c)


def _ctx_attn_kernel(sink_ref, qn_ref, kn_ref, vn_ref, qs_ref, ks_ref, vs_ref, gb_ref, gc_ref,
                     yb_ref, yc_ref, y_s, *, ctx_len):
    lane_first = _lane_first(ctx_len)
    for j in range(NA_DIM // LANES):
        cols = slice(j * LANES, (j + 1) * LANES)
        y_s[:, cols] = _pair_attention(qn_ref[:, cols], None, None, None, kn_ref[:, cols], vn_ref[:, cols],
                                       None, lane_first)
    yb_ref[...] = _rms(y_s[...], gb_ref[...]).astype(BF16)
    pairs = SW_HEADS // SW_KV_HEADS
    for j in range(pairs):
        cols = slice(j * LANES, (j + 1) * LANES)
        sinks = (sink_ref[j], sink_ref[j + pairs])
        y_s[:, cols] = _pair_attention(qs_ref[:, cols], None, None, None, ks_ref[...], vs_ref[...],
                                       sinks, lane_first)
    yc_ref[...] = _rms(y_s[...], gc_ref[...]).astype(BF16)


def _ctx_attention(sink, qn, kn, vn, qs, ks, vs, g_b, g_c, *, bsz, ctx_len):
    kern = functools.partial(_ctx_attn_kernel, ctx_len=ctx_len)
    tok = lambda width: pl.BlockSpec((ctx_len, width), lambda b: (b, 0))
    return pl.pallas_call(
        kern,
        grid=(bsz,),
        in_specs=[pl.BlockSpec(memory_space=pltpu.SMEM),
                  tok(NA_DIM), tok(NA_DIM), tok(NA_DIM), tok(SW_DIM), tok(SW_KV_DIM), tok(SW_KV_DIM),
                  _const_spec((1, NA_DIM)), _const_spec((1, SW_DIM))],
        out_specs=[tok(NA_DIM), tok(SW_DIM)],
        out_shape=[jax.ShapeDtypeStruct(qn.shape, BF16), jax.ShapeDtypeStruct(qs.shape, BF16)],
        scratch_shapes=[pltpu.VMEM((ctx_len, NA_DIM), F32)],
        compiler_params=_params(("arbitrary",)),
        name="ctx_attention",
    )(sink, qn, kn, vn, qs, ks, vs, g_b, g_c)


def _mix_residual(ya_ref, yb_ref, yc_ref, x_ref, g1_ref, sh_ref, sc_ref, gn_ref, wa_ref, wb_ref, wc_ref):
    mix = _dot(ya_ref[...], wa_ref[...]) + _dot(yb_ref[...], wb_ref[...]) + _dot(yc_ref[...], wc_ref[...])
    xl = x_ref[...] + g1_ref[0] * mix
    h = _rms(xl, gn_ref[...]) * (1.0 + sc_ref[0]) + sh_ref[0]
    return xl, h


def _mix_specs(x2d, mods, group_of_tile, g_norm, ys, ws, tm):
    d = x2d.shape[1]
    tok = lambda width: pl.BlockSpec((tm, width), lambda i: (i, 0))
    mod = lambda j: pl.BlockSpec((1, 1, d), lambda i: (group_of_tile(i), 0, j))
    in_specs = ([tok(y.shape[1]) for y in ys] + [tok(d), mod(2), mod(3), mod(4), _const_spec((1, d))]
                + [_resident_spec(w.shape) for w in ws])
    return in_specs, [*ys, x2d, mods, mods, mods, g_norm, *ws]


def _outproj_router_kernel(*refs):
    *mix_refs, r_ref, xo_ref, h_ref, lg_ref = refs
    xl, h = _mix_residual(*mix_refs)
    xo_ref[...] = xl
    hi = h.astype(BF16)
    h_ref[...] = hi
    tm = h.shape[0]
    lo = (h - hi.astype(F32)).astype(BF16)
    r = _dot(jnp.concatenate([hi, lo], axis=0), r_ref[...])
    lg_ref[...] = (r[:tm, :LOGIT_LANES] + r[:tm, LOGIT_LANES:]) + (r[tm:, :LOGIT_LANES] + r[tm:, LOGIT_LANES:])


def _outproj_router(ys, x2d, mods, group_of_tile, g_norm, ws, router_w, *, tm):
    t, d = x2d.shape
    in_specs, args = _mix_specs(x2d, mods, group_of_tile, g_norm, ys, ws, tm)
    tok = lambda width: pl.BlockSpec((tm, width), lambda i: (i, 0))
    return pl.pallas_call(
        _outproj_router_kernel,
        grid=(t // tm,),
        in_specs=in_specs + [_resident_spec(router_w.shape)],
        out_specs=[tok(d), tok(d), tok(LOGIT_LANES)],
        out_shape=[jax.ShapeDtypeStruct((t, d), F32), jax.ShapeDtypeStruct((t, d), BF16),
                   jax.ShapeDtypeStruct((t, LOGIT_LANES), F32)],
        compiler_params=_params(("arbitrary",)),
        name="out_projection_router",
    )(*args, router_w)


def _outproj_ffn_kernel(*refs, chunks):
    *mix_refs, g2_ref, wg_ref, wu_ref, wd_ref, o_ref = refs
    xl, h = _mix_residual(*mix_refs)
    h = h.astype(BF16)
    acc = None
    for lo, hi in chunks:
        a = _silu(_dot(h, wg_ref[:, lo:hi].astype(BF16))) * _dot(h, wu_ref[:, lo:hi].astype(BF16))
        part = _dot(a.astype(BF16), wd_ref[lo:hi, :].astype(BF16))
        acc = part if acc is None else acc + part
    o_ref[...] = xl + g2_ref[0] * acc


def _outproj_ffn(ys, x2d, mods, group_of_tile, g_norm, ws, wg, wu, wd, *, tm):
    t, d = x2d.shape
    d_ff = wg.shape[1]
    step = 1024
    chunks = tuple((lo, min(lo + step, d_ff)) for lo in range(0, d_ff, step))
    in_specs, args = _mix_specs(x2d, mods, group_of_tile, g_norm, ys, ws, tm)
    return pl.pallas_call(
        functools.partial(_outproj_ffn_kernel, chunks=chunks),
        grid=(t // tm,),
        in_specs=in_specs + [pl.BlockSpec((1, 1, d), lambda i: (group_of_tile(i), 0, 5)),
                             _resident_spec(wg.shape), _resident_spec(wu.shape), _resident_spec(wd.shape)],
        out_specs=pl.BlockSpec((tm, d), lambda i: (i, 0)),
        out_shape=jax.ShapeDtypeStruct((t, d), F32),
        compiler_params=_params(("arbitrary",)),
        name="out_projection_ffn",
    )(*args, mods, wg, wu, wd)


def _route_kernel(*refs):
    for j in range(MOE_STEP_CHUNKS):
        _route_chunk(j, *refs)


def _route_chunk(j, lg_ref, h_ref, hl_ref, pos_ref, gsel_ref, meta_ref):
    toks = slice(j * MOE_CHUNK, (j + 1) * MOE_CHUNK)
    logits = lg_ref[toks, :]
    lane = lax.broadcasted_iota(jnp.int32, logits.shape, 1)
    lg = jnp.where(lane < N_EXPERTS, logits, -jnp.inf)
    v1 = jnp.max(lg, axis=-1, keepdims=True)
    i1 = jnp.min(jnp.where(lg == v1, lane, LOGIT_LANES), axis=-1, keepdims=True)
    lg2 = jnp.where(lane == i1, -jnp.inf, lg)
    v2 = jnp.max(lg2, axis=-1, keepdims=True)
    i2 = jnp.min(jnp.where(lg2 == v2, lane, LOGIT_LANES), axis=-1, keepdims=True)
    e2 = jnp.exp(v2 - v1)
    den = 1.0 + e2
    g_first = 1.0 / den
    g_second = e2 / den
    sel1 = lane == i1
    sel2 = lane == i2
    onehot = jnp.where(sel1, 1.0, jnp.where(sel2, 1.0, 0.0))

    r_i = lax.broadcasted_iota(jnp.int32, (MOE_CHUNK, MOE_CHUNK), 0)
    c_i = lax.broadcasted_iota(jnp.int32, (MOE_CHUNK, MOE_CHUNK), 1)
    earlier = jnp.where(c_i < r_i, 1.0, 0.0).astype(BF16)
    within = _dot(earlier, onehot.astype(BF16))
    count = jnp.sum(onehot, axis=0, keepdims=True)
    n_gran = jnp.floor((count + (MOE_GRANULE - 1)) * (1.0 / MOE_GRANULE))
    u_r = lax.broadcasted_iota(jnp.int32, (LOGIT_LANES, LOGIT_LANES), 0)
    u_c = lax.broadcasted_iota(jnp.int32, (LOGIT_LANES, LOGIT_LANES), 1)
    before = jnp.where(u_r < u_c, 1.0, 0.0).astype(BF16)
    n_gran8 = jnp.broadcast_to(n_gran, (SUBLANES, LOGIT_LANES))
    off_gran = _dot(n_gran8.astype(BF16), before)[0:1]
    lpos = off_gran * MOE_GRANULE + within
    p_first = jnp.sum(jnp.where(sel1, lpos, 0.0), axis=-1, keepdims=True)
    p_second = jnp.sum(jnp.where(sel2, lpos, 0.0), axis=-1, keepdims=True)
    pos = jnp.where(lane == 0, p_first, jnp.where(lane == 1, p_second, -1.0))
    pos_ref[toks, :] = pos
    gsel_ref[toks, :] = jnp.where(lane == 0, g_first, jnp.where(lane == 1, g_second, 0.0))
    rows8 = lax.broadcasted_iota(jnp.int32, (SUBLANES, LOGIT_LANES), 0)
    meta_ref[j] = jnp.where(rows8 == 0, off_gran, jnp.where(rows8 == 1, n_gran, 0.0)).astype(jnp.int32)

    pos_t = pos.T
    row_id = lax.broadcasted_iota(jnp.int32, (MOE_LOCAL_ROWS, MOE_CHUNK), 0).astype(F32)
    pick = jnp.where(row_id == pos_t[0:1], 1.0, jnp.where(row_id == pos_t[1:2], 1.0, 0.0)).astype(BF16)
    hl_ref[j] = _dot(pick, h_ref[toks, :]).astype(BF16)


def _route(logits, h2):
    t, d = h2.shape
    n_chunks = t // MOE_CHUNK
    per_step = MOE_STEP_CHUNKS
    tok = lambda width: pl.BlockSpec((per_step * MOE_CHUNK, width), lambda i: (i, 0))
    return pl.pallas_call(
        _route_kernel,
        grid=(n_chunks // per_step,),
        in_specs=[tok(LOGIT_LANES), tok(d)],
        out_specs=[pl.BlockSpec((per_step, MOE_LOCAL_ROWS, d), lambda i: (i, 0, 0)),
                   tok(LOGIT_LANES), tok(LOGIT_LANES),
                   pl.BlockSpec((per_step, SUBLANES, LOGIT_LANES), lambda i: (i, 0, 0))],
        out_shape=[jax.ShapeDtypeStruct((n_chunks, MOE_LOCAL_ROWS, d), BF16),
                   jax.ShapeDtypeStruct((t, LOGIT_LANES), F32),
                   jax.ShapeDtypeStruct((t, LOGIT_LANES), F32),
                   jax.ShapeDtypeStruct((n_chunks, SUBLANES, LOGIT_LANES), jnp.int32)],
        compiler_params=_params(("arbitrary",)),
        name="moe_route",
    )(logits, h2)


def _experts_kernel(meta_ref, hl_hbm, wg_ref, wu_ref, wd_ref, yl_hbm, he_s, ye_s, yb_s, zero_s,
                    gather_sem, store_sem, fill_sem, *, n_chunks):
    ws = pl.program_id(0)
    e = pl.program_id(1)
    f = pl.program_id(2)
    n_e = pl.num_programs(1)
    n_f = pl.num_programs(2)
    step = ws * n_e + e
    last_step = pl.num_programs(0) * n_e - 1

    def meta(ws_, e_, c, k):
        return meta_ref[((ws_ * n_chunks + c) * 2 + k) * N_EXPERTS + e_]

    def rows_of(granules):
        return pl.multiple_of(granules * MOE_GRANULE, MOE_GRANULE)

    def for_each_run(ws_, e_, fn, with_parity=False):
        dst = 0
        for c in range(n_chunks):
            src, n = meta(ws_, e_, c, 0), meta(ws_, e_, c, 1)

            @pl.when(n > 0)
            def _(c=c, src=src, n=n, dst=dst):
                fn(ws_ * n_chunks + c, rows_of(src), rows_of(dst), rows_of(n), *((c % 2,) if with_parity else ()))
            dst = dst + n
        return dst * MOE_GRANULE

    def gather_copy(slot):
        return lambda chunk, src, dst, rows: pltpu.make_async_copy(
            hl_hbm.at[chunk, pl.ds(src, rows)], he_s.at[slot, pl.ds(dst, rows)], gather_sem.at[slot])

    def store_copy(chunk, src, dst, rows):
        return pltpu.make_async_copy(yb_s.at[pl.ds(dst, rows)], yl_hbm.at[chunk, pl.ds(src, rows)], store_sem)

    def fill_tails(start):
        for c in range(n_chunks):
            used = meta(ws, N_EXPERTS - 1, c, 0) + meta(ws, N_EXPERTS - 1, c, 1)
            tail = MOE_LOCAL_ROWS // MOE_GRANULE - used

            @pl.when(tail > 0)
            def _(c=c, used=used, tail=tail):
                cp = pltpu.make_async_copy(zero_s.at[pl.ds(0, rows_of(tail))],
                                           yl_hbm.at[ws * n_chunks + c, pl.ds(rows_of(used), rows_of(tail))],
                                           fill_sem)
                cp.start() if start else cp.wait()

    slot = step % 2

    @pl.when(f == 0)
    def _():
        @pl.when(step == 0)
        def _():
            he_s[...] = jnp.zeros_like(he_s)
            zero_s[...] = jnp.zeros_like(zero_s)
            for_each_run(ws, e, lambda *run: gather_copy(slot)(*run).start())

        @pl.when(e == 0)
        def _():
            fill_tails(True)
            fill_tails(False)

        for_each_run(ws, e, lambda *run: gather_copy(slot)(*run).wait())

        @pl.when(step < last_step)
        def _():
            nxt = step + 1
            for_each_run(nxt // n_e, nxt % n_e, lambda *run: gather_copy(1 - slot)(*run).start())

    def m_tile(start, size, mode):
        rows = pl.ds(pl.multiple_of(start, MOE_M_MIN), size)
        hrow = he_s[slot, rows, :]
        a = _silu(_dot(hrow, wg_ref[0].astype(BF16))) * _dot(hrow, wu_ref[0].astype(BF16))
        part = _dot(a.astype(BF16), wd_ref[0].astype(BF16))
        if mode == "first":
            ye_s[rows, :] = part
        elif mode == "middle":
            ye_s[rows, :] += part
        else:
            yb_s[rows, :] = (ye_s[rows, :] + part).astype(BF16)

    n_rows = sum(meta(ws, e, c, 1) for c in range(n_chunks)) * MOE_GRANULE
    n_full = n_rows // MOE_M_TILE
    rem = n_rows - n_full * MOE_M_TILE

    def all_tiles(mode):
        def tile_pair(m, carry):
            m_tile(2 * m * MOE_M_TILE, MOE_M_TILE, mode)
            m_tile((2 * m + 1) * MOE_M_TILE, MOE_M_TILE, mode)
            return carry
        lax.fori_loop(0, n_full // 2, tile_pair, 0)

        @pl.when(n_full % 2 == 1)
        def _():
            m_tile((n_full - 1) * MOE_M_TILE, MOE_M_TILE, mode)

        size = MOE_M_TILE
        while size >= MOE_M_MIN:
            lo = 0 if size == MOE_M_MIN else size // 2

            @pl.when((rem > lo) & (rem <= size))
            def _(size=size):
                m_tile(n_full * MOE_M_TILE, size, mode)
            size //= 2

    @pl.when(f == 0)
    def _():
        all_tiles("first")

    @pl.when((f > 0) & (f < n_f - 1))
    def _():
        all_tiles("middle")

    @pl.when(f == n_f - 1)
    def _():
        @pl.when(step > 0)
        def _():
            prev = step - 1
            for_each_run(prev // n_e, prev % n_e, lambda *run: store_copy(*run).wait())

        all_tiles("last")
        for_each_run(ws, e, lambda *run: store_copy(*run[:4]).start(priority=run[4]), with_parity=True)

        @pl.when(step == last_step)
        def _():
            for_each_run(ws, e, lambda *run: store_copy(*run).wait())


def _experts(meta, hl, wg, wu, wd, *, n_chunks):
    total_chunks, local_rows, d = hl.shape
    n_exp, _, d_ff = wg.shape
    tf = MOE_FF_TILE
    assert d_ff // tf >= 2, "the first and the last hidden tile must be different grid steps"
    max_rows = n_chunks * MOE_CHUNK
    grid_spec = pltpu.PrefetchScalarGridSpec(
        num_scalar_prefetch=1,
        grid=(total_chunks // n_chunks, n_exp, d_ff // tf),
        in_specs=[pl.BlockSpec(memory_space=pl.ANY),
                  pl.BlockSpec((1, d, tf), lambda i, e, f, m: (e, 0, f)),
                  pl.BlockSpec((1, d, tf), lambda i, e, f, m: (e, 0, f)),
                  pl.BlockSpec((1, tf, d), lambda i, e, f, m: (e, f, 0))],
        out_specs=pl.BlockSpec(memory_space=pl.ANY),
        scratch_shapes=[pltpu.VMEM((2, max_rows, d), BF16), pltpu.VMEM((max_rows, d), F32),
                        pltpu.VMEM((max_rows, d), BF16), pltpu.VMEM((local_rows, d), BF16),
                        pltpu.SemaphoreType.DMA((2,)), pltpu.SemaphoreType.DMA(()), pltpu.SemaphoreType.DMA(())])
    return pl.pallas_call(
        functools.partial(_experts_kernel, n_chunks=n_chunks),
        grid_spec=grid_spec,
        out_shape=jax.ShapeDtypeStruct(hl.shape, BF16),
        compiler_params=_params(("arbitrary", "arbitrary", "arbitrary")),
        name="moe_experts",
    )(meta, hl, wg, wu, wd)


def _scatter_kernel(yl_ref, pos_ref, gsel_ref, x_ref, g2_ref, gf_ref, o_ref):
    row_id = lax.broadcasted_iota(jnp.int32, (MOE_CHUNK, MOE_LOCAL_ROWS), 1).astype(F32)
    for j in range(MOE_STEP_CHUNKS):
        toks = slice(j * MOE_CHUNK, (j + 1) * MOE_CHUNK)
        pos = pos_ref[toks, :]
        gsel = gsel_ref[toks, :]
        yl = yl_ref[j]
        y = None
        for k in range(2):
            pick = jnp.where(pos[:, k:k + 1] == row_id, 1.0, 0.0).astype(BF16)
            term = gsel[:, k:k + 1] * _dot(pick, yl)
            y = term if y is None else y + term
        o_ref[toks, :] = _rms(x_ref[toks, :] + g2_ref[0] * y, gf_ref[...])


def _scatter_final(yl, pos, gsel, x2d, mods, group_of_step, g_final):
    t, d = x2d.shape
    per_step = MOE_STEP_CHUNKS
    tok = lambda width: pl.BlockSpec((per_step * MOE_CHUNK, width), lambda i: (i, 0))
    return pl.pallas_call(
        _scatter_kernel,
        grid=(t // (per_step * MOE_CHUNK),),
        in_specs=[pl.BlockSpec((per_step, MOE_LOCAL_ROWS, d), lambda i: (i, 0, 0)),
                  tok(LOGIT_LANES), tok(LOGIT_LANES), tok(d),
                  pl.BlockSpec((1, 1, d), lambda i: (group_of_step(i), 0, 5)), _const_spec((1, d))],
        out_specs=tok(d),
        out_shape=jax.ShapeDtypeStruct((t, d), F32),
        compiler_params=_params(("arbitrary",)),
        name="moe_scatter_final",
    )(yl, pos, gsel, x2d, mods, g_final)


def _rope_tables(n):
    t = jnp.arange(n)
    row = (t // GRID_W).astype(F32)
    col = (t % GRID_W).astype(F32)
    n_freq = HEAD_DIM // 4
    inv = ROPE_THETA ** (-jnp.arange(n_freq, dtype=F32) / n_freq)
    ang = jnp.stack([row[:, None] * inv, col[:, None] * inv], axis=1)
    cos = jnp.cos(ang)[:, :, None, :]
    sin = jnp.sin(ang)[:, :, None, :]
    cos_f = jnp.broadcast_to(cos, (n, 2, 2, n_freq)).reshape(n, HEAD_DIM)
    sin_s = jnp.concatenate([-sin, sin], axis=2).reshape(n, HEAD_DIM)
    reps = LANES // HEAD_DIM
    return jnp.tile(cos_f, (1, reps)), jnp.tile(sin_s, (1, reps))


def _pair_perm():
    pairs = SW_HEADS // SW_KV_HEADS
    order = [h for j in range(pairs) for h in (j, j + pairs)]
    return np.concatenate([np.arange(h * HEAD_DIM, (h + 1) * HEAD_DIM) for h in order])


def kernel(x, c, ctx, c_ctx, w_ada, b_ada, g_norm1, g_norm2, w_in, conv_w, na_rpb, sw_sink, g_mix, w_out,
           ffn_w_gate, ffn_w_up, ffn_w_down, w_router, moe_w_gate, moe_w_up, moe_w_down, g_final):
    bsz, n, d = x.shape
    ctx_len = ctx.shape[1]
    depth = w_in.shape[0]
    assert w_in.shape[2] == D_IN and n % GRID_W == 0
    assert depth == 2, "layer 0 dense on both streams, layer 1 routed experts on the latent stream only"
    tm = 512
    tm_in = 1024
    tm_ctx = ctx_len
    set_chunks = MOE_SET_CHUNKS
    assert n % tm == 0 and n % tm_in == 0 and (n // GRID_W) % (NA_ROWS_PER_STEP * NA_STEP_GROUPS) == 0
    assert n % (SW_QBLOCK * SW_STEP_BLOCKS) == 0 and tm % MOE_CHUNK == 0
    assert n % (MOE_CHUNK * MOE_STEP_CHUNKS) == 0 and (bsz * n) % (MOE_CHUNK * set_chunks) == 0
    assert (MOE_CHUNK * set_chunks) % MOE_M_TILE == 0

    cond_rows = 2 * SUBLANES
    cond = jnp.zeros((cond_rows, d), F32).at[:bsz].set(c).at[bsz].set(c_ctx)
    mods_all = _ada(cond, w_ada, b_ada).reshape(depth, cond_rows, 1, 6 * d)

    lat_group = lambda i: i // (n // tm)
    ctx_group = lambda i: bsz
    cos_t, sin_t = _rope_tables(n)
    perm = _pair_perm()

    xl = x.reshape(bsz * n, d)
    xc = ctx.reshape(bsz * ctx_len, d)
    out = None
    for l in range(depth):
        last = l == depth - 1
        mods = mods_all[l]
        w_l = w_in[l]
        w_l = jnp.concatenate([w_l[:, :OFF_SQ], w_l[:, OFF_SQ:OFF_NK][:, perm], w_l[:, OFF_NK:]], axis=1).astype(BF16)
        g_a = g_mix[l, :CONV_CH].reshape(1, CONV_CH)
        g_b = g_mix[l, CONV_CH:CONV_CH + NA_DIM].reshape(1, NA_DIM)
        g_c = g_mix[l, CONV_CH + NA_DIM:][perm].reshape(1, SW_DIM)
        wo = w_out[l]
        wa = wo[:CONV_CH].astype(BF16)
        wb = wo[CONV_CH:CONV_CH + NA_DIM].astype(BF16)
        wc = wo[CONV_CH + NA_DIM:][perm].astype(BF16)
        gn1 = g_norm1[l].reshape(1, d)
        gn2 = g_norm2[l].reshape(1, d)
        sink = sw_sink[l].astype(F32) * LOG2E
        tab = _na_bias_table(na_rpb[l])

        ya, qn, qs, kn, vn, ks, vs = _inproj(xl, mods, lambda i: i // (n // tm_in), gn1, w_l, conv_w[l], g_a,
                                             cos_t, sin_t, seq_len=n, tm=tm_in, rope=True)
        cya, cqn, cqs, ckn, cvn, cks, cvs = _inproj(xc, mods, ctx_group, gn1, w_l, conv_w[l], g_a, cos_t, sin_t,
                                                    seq_len=ctx_len, tm=tm_ctx, rope=False)
        yb = _na_attention(qn, kn, vn, ckn, cvn, tab, g_b, bsz=bsz, seq_len=n, ctx_len=ctx_len)
        yc = _sw_attention(sink, qs, ks, vs, cks, cvs, g_c, bsz=bsz, seq_len=n, ctx_len=ctx_len)

        if l % 2 == 0:
            i = l // 2
            wg, wu, wd = ffn_w_gate[i], ffn_w_up[i], ffn_w_down[i]
            xl = _outproj_ffn((ya, yb, yc), xl, mods, lat_group, gn2, (wa, wb, wc), wg, wu, wd, tm=tm)
            if not last:
                cyb, cyc = _ctx_attention(sink, cqn, ckn, cvn, cqs, cks, cvs, g_b, g_c, bsz=bsz, ctx_len=ctx_len)
                xc = _outproj_ffn((cya, cyb, cyc), xc, mods, ctx_group, gn2, (wa, wb, wc), wg, wu, wd,
                                  tm=tm_ctx)
        else:
            i = l // 2
            wr = jnp.zeros((d, LOGIT_LANES), F32).at[:, :N_EXPERTS].set(w_router[i])
            wr_hi = wr.astype(BF16)
            wr_lo = (wr - wr_hi.astype(F32)).astype(BF16)
            wr2 = jnp.concatenate([wr_hi, wr_lo], axis=1)
            wg, wu, wd = moe_w_gate[i], moe_w_up[i], moe_w_down[i]
            xl, h2, logits = _outproj_router((ya, yb, yc), xl, mods, lambda i: i // (n // tm_in), gn2,
                                             (wa, wb, wc), wr2, tm=tm_in)
            hl, pos, gsel, meta = _route(logits, h2)
            meta = meta[:, :2, :N_EXPERTS].reshape(-1)
            yl = _experts(meta, hl, wg, wu, wd, n_chunks=set_chunks)
            step_tokens = MOE_CHUNK * MOE_STEP_CHUNKS
            out = _scatter_final(yl, pos, gsel, xl, mods, lambda i: i // (n // step_tokens), g_final.reshape(1, d))
    return out.reshape(bsz, n, d)
```
